```python
import jax, jax.numpy as jnp
from jax import lax
import numpy as np

D_MODEL = 2048
BATCH = 16
SEQ = 256
DEPTH = 1
DEC_BATCH = 2
DEC_SEQ = 4096
PAST_LEN = 512

GRID_W = 64
N_HEADS = 16
N_KV_HEADS = 4
HEAD_DIM = 128
GROUP = N_HEADS // N_KV_HEADS
ATTN_WIDTH = N_HEADS * HEAD_DIM
KV_WIDTH = N_KV_HEADS * HEAD_DIM
WINDOW = 128
BLOCK = 128
CONV_WIDTH = D_MODEL
CONV_K = 3
N_EXPERTS = 64
TOP_K = 8
N_EXPERT_GROUPS = 8
TOPK_GROUPS = 4
D_EXPERT = 512
D_SHARED = 512
ROUTED_SCALE = 2.5
MOE_CHUNK = 128
ROPE_THETA = 10000.0
EPS = 1e-6
NEG_INF = -1e30
ATTN_SCALE = HEAD_DIM ** -0.5
IN_WIDTHS = (CONV_WIDTH, CONV_WIDTH, CONV_WIDTH, ATTN_WIDTH, KV_WIDTH, KV_WIDTH, D_MODEL, D_MODEL)
IN_WIDTH = sum(IN_WIDTHS)
IN_SPLITS = tuple(int(s) for s in np.cumsum(IN_WIDTHS)[:-1])

kernel_name = "hybrid_dit_conv_swa_moe_step"


def rmsnorm(x, g):
    xf = x.astype(jnp.float32)
    y = xf * lax.rsqrt(jnp.mean(xf * xf, axis=-1, keepdims=True) + EPS)
    return (y * g.astype(jnp.float32)).astype(x.dtype)


def adaln_mods(cond, w_ada, b_ada):
    m = (jax.nn.silu(cond) @ w_ada + b_ada)[:, None, :]
    return jnp.split(m, 6, axis=-1)


def modulate(h, shift, scale):
    return h * (1.0 + scale) + shift


def centred_short_conv(u, w):
    up = jnp.pad(u, ((0, 0), (1, 1), (0, 0)))
    return up[:, :-2] * w[0] + up[:, 1:-1] * w[1] + up[:, 2:] * w[2]


def axial_rope_tables(n_tokens):
    rows = n_tokens // GRID_W
    row = jnp.repeat(jnp.arange(rows, dtype=jnp.float32), GRID_W)
    col = jnp.tile(jnp.arange(GRID_W, dtype=jnp.float32), rows)
    n_freq = HEAD_DIM // 4
    inv = ROPE_THETA ** (-jnp.arange(n_freq, dtype=jnp.float32) / n_freq)
    ang = jnp.concatenate([row[:, None] * inv, col[:, None] * inv], axis=-1)
    return jnp.cos(ang)[:, None, :], jnp.sin(ang)[:, None, :]


def apply_rope(x, cos, sin):
    xf = x.astype(jnp.float32)
    x1, x2 = xf[..., :HEAD_DIM // 2], xf[..., HEAD_DIM // 2:]
    return jnp.concatenate([x1 * cos - x2 * sin, x1 * sin + x2 * cos], axis=-1).astype(x.dtype)


def sink_softmax(s, sink):
    sk = sink.astype(jnp.float32).reshape(N_KV_HEADS, GROUP)[None, :, :, None, None]
    m = jnp.maximum(jnp.max(s, axis=-1, keepdims=True), sk)
    p = jnp.exp(s - m)
    return p / (jnp.sum(p, axis=-1, keepdims=True) + jnp.exp(sk - m))


def context_attention(q, k, v, sink):
    B, S = q.shape[0], q.shape[1]
    nb = S // BLOCK
    qb = jnp.moveaxis(q.reshape(B, nb, BLOCK, N_KV_HEADS, GROUP, HEAD_DIM), 1, 0)
    kf = k.astype(jnp.float32)
    vf = v.astype(jnp.float32)

    def one(qblk):
        s = jnp.einsum('bqkgd,bskd->bkgqs', qblk.astype(jnp.float32), kf) * ATTN_SCALE
        p = sink_softmax(s, sink)
        return jnp.einsum('bkgqs,bskd->bqkgd', p, vf)

    o = lax.map(one, qb)
    return jnp.moveaxis(o, 0, 1).reshape(B, S, ATTN_WIDTH).astype(q.dtype)


def latent_attention(q, k, v, kc, vc, sink):
    B, N = q.shape[0], q.shape[1]
    nb = N // BLOCK
    qb = jnp.moveaxis(q.reshape(B, nb, BLOCK, N_KV_HEADS, GROUP, HEAD_DIM), 1, 0)

    def band(t):
        tp = jnp.pad(t.astype(jnp.float32), ((0, 0), (BLOCK, BLOCK), (0, 0), (0, 0)))
        tp = tp.reshape(B, nb + 2, BLOCK, N_KV_HEADS, HEAD_DIM)
        bt = jnp.concatenate([tp[:, :-2], tp[:, 1:-1], tp[:, 2:]], axis=2)
        return jnp.moveaxis(bt, 1, 0)

    kb, vb = band(k), band(v)
    qpos = jnp.arange(nb)[:, None] * BLOCK + jnp.arange(BLOCK)[None, :]
    kpos = jnp.arange(nb)[:, None] * BLOCK - BLOCK + jnp.arange(3 * BLOCK)[None, :]
    valid = ((jnp.abs(qpos[:, :, None] - kpos[:, None, :]) <= WINDOW)
             & (kpos[:, None, :] >= 0) & (kpos[:, None, :] < N))
    kcf = kc.astype(jnp.float32)
    vcf = vc.astype(jnp.float32)

    def one(args):
        qblk, kblk, vblk, vmask = args
        qf = qblk.astype(jnp.float32)
        s_loc = jnp.einsum('bqkgd,bskd->bkgqs', qf, kblk) * ATTN_SCALE
        s_loc = jnp.where(vmask[None, None, None], s_loc, NEG_INF)
        s_ctx = jnp.einsum('bqkgd,bpkd->bkgqp', qf, kcf) * ATTN_SCALE
        p = sink_softmax(jnp.concatenate([s_loc, s_ctx], axis=-1), sink)
        return (jnp.einsum('bkgqs,bskd->bqkgd', p[..., :3 * BLOCK], vblk)
                + jnp.einsum('bkgqp,bpkd->bqkgd', p[..., 3 * BLOCK:], vcf))

    o = lax.map(one, (qb, kb, vb, valid))
    return jnp.moveaxis(o, 0, 1).reshape(B, N, ATTN_WIDTH).astype(q.dtype)


def mixer_inputs(h, w_in, conv_w, w_conv_out, q_norm_g, k_norm_g):
    B, S = h.shape[0], h.shape[1]
    cb, cc, cx, q, k, v, g_conv, g_attn = jnp.split(h @ w_in, IN_SPLITS, axis=-1)
    y_conv = (cb * centred_short_conv(cc * cx, conv_w)) @ w_conv_out
    q = rmsnorm(q.reshape(B, S, N_HEADS, HEAD_DIM), q_norm_g)
    k = rmsnorm(k.reshape(B, S, N_KV_HEADS, HEAD_DIM), k_norm_g)
    v = v.reshape(B, S, N_KV_HEADS, HEAD_DIM)
    return y_conv, q, k, v, g_conv, g_attn


def merge_branches(y_conv, attn, g_conv, g_attn, w_attn_out, w_out):
    y_attn = attn @ w_attn_out
    return (jax.nn.sigmoid(g_conv) * y_conv + jax.nn.sigmoid(g_attn) * y_attn) @ w_out


def moe_ffn(h, router_w, router_bias, w_eg, w_eu, w_ed, w_sg, w_su, w_sd):
    B, S, D = h.shape
    t = h.reshape(B * S, D)
    T = t.shape[0]
    scores = jax.nn.sigmoid(t.astype(jnp.float32) @ router_w.astype(jnp.float32))
    biased = scores + router_bias.astype(jnp.float32)
    grp = biased.reshape(T, N_EXPERT_GROUPS, N_EXPERTS // N_EXPERT_GROUPS)
    grp_score = jnp.sum(lax.top_k(grp, 2)[0], axis=-1)
    _, gidx = lax.top_k(grp_score, TOPK_GROUPS)
    gsel = jnp.any(jax.nn.one_hot(gidx, N_EXPERT_GROUPS, dtype=jnp.float32) > 0, axis=1)
    emask = jnp.repeat(gsel, N_EXPERTS // N_EXPERT_GROUPS, axis=1)
    _, eidx = lax.top_k(jnp.where(emask, biased, NEG_INF), TOP_K)
    sel = jnp.take_along_axis(scores, eidx, axis=-1)
    wts = sel / jnp.sum(sel, axis=-1, keepdims=True) * ROUTED_SCALE
    dense_w = jnp.sum(jax.nn.one_hot(eidx, N_EXPERTS, dtype=jnp.float32) * wts[..., None], axis=1)
    nch = T // MOE_CHUNK
    tc = t.reshape(nch, MOE_CHUNK, D)
    wc = dense_w.reshape(nch, MOE_CHUNK, N_EXPERTS).astype(t.dtype)

    def experts(args):
        xc, gw = args
        a = jax.nn.silu(jnp.einsum('cd,edf->cef', xc, w_eg)) * jnp.einsum('cd,edf->cef', xc, w_eu)
        return jnp.einsum('cef,efd->cd', a * gw[..., None], w_ed)

    routed = lax.map(experts, (tc, wc)).reshape(T, D)
    shared = (jax.nn.silu(t @ w_sg) * (t @ w_su)) @ w_sd
    return (routed + shared).reshape(B, S, D)


def setup_inputs(seed: int = 0) -> dict:
    key = jax.random.key(seed)
    ks = jax.random.split(key, 26)
    f32 = jnp.float32
    D = D_MODEL

    def nrm(k, shape, scale):
        return jax.random.normal(k, shape, f32) * scale

    return {
        "x_prompt": nrm(ks[0], (BATCH, SEQ, D), 1.0),
        "x_sample": nrm(ks[1], (DEC_BATCH, DEC_SEQ, D), 1.0),
        "cache_k": nrm(ks[2], (DEC_BATCH, DEPTH, PAST_LEN, N_KV_HEADS, HEAD_DIM), 1.0),
        "cache_v": nrm(ks[3], (DEC_BATCH, DEPTH, PAST_LEN, N_KV_HEADS, HEAD_DIM), 1.0),
        "c": nrm(ks[4], (DEC_BATCH, D), 1.0),
        "c_ctx": nrm(ks[5], (D,), 1.0),
        "w_ada": nrm(ks[6], (DEPTH, D, 6 * D), 0.5 * D ** -0.5),
        "b_ada": nrm(ks[7], (DEPTH, 6 * D), 0.02),
        "norm_mix_g": 1.0 + nrm(ks[8], (DEPTH, D), 0.02),
        "norm_ffn_g": 1.0 + nrm(ks[9], (DEPTH, D), 0.02),
        "w_in": nrm(ks[10], (DEPTH, D, IN_WIDTH), D ** -0.5),
        "conv_w": nrm(ks[11], (DEPTH, CONV_K, CONV_WIDTH), CONV_K ** -0.5),
        "q_norm_g": 1.0 + nrm(ks[12], (DEPTH, HEAD_DIM), 0.02),
        "k_norm_g": 1.0 + nrm(ks[13], (DEPTH, HEAD_DIM), 0.02),
        "attn_sink": nrm(ks[14], (DEPTH, N_HEADS), 0.5),
        "w_conv_out": nrm(ks[15], (DEPTH, CONV_WIDTH, D), CONV_WIDTH ** -0.5),
        "w_attn_out": nrm(ks[16], (DEPTH, ATTN_WIDTH, D), ATTN_WIDTH ** -0.5),
        "w_out": nrm(ks[17], (DEPTH, D, D), D ** -0.5),
        "router_w": nrm(ks[18], (DEPTH, D, N_EXPERTS), D ** -0.5),
        "router_bias": nrm(ks[19], (DEPTH, N_EXPERTS), 0.01),
        "w_exp_gate": nrm(ks[20], (DEPTH, N_EXPERTS, D, D_EXPERT), D ** -0.5),
        "w_exp_up": nrm(ks[21], (DEPTH, N_EXPERTS, D, D_EXPERT), D ** -0.5),
        "w_exp_down": nrm(ks[22], (DEPTH, N_EXPERTS, D_EXPERT, D), D_EXPERT ** -0.5),
        "w_sh_gate": nrm(ks[23], (DEPTH, D, D_SHARED), D ** -0.5),
        "w_sh_up": nrm(ks[24], (DEPTH, D, D_SHARED), D ** -0.5),
        "w_sh_down": nrm(ks[25], (DEPTH, D_SHARED, D), D_SHARED ** -0.5),
    }


def reference(x_prompt, x_sample, cache_k, cache_v, c, c_ctx, w_ada, b_ada, norm_mix_g, norm_ffn_g,
              w_in, conv_w, q_norm_g, k_norm_g, attn_sink, w_conv_out, w_attn_out, w_out,
              router_w, router_bias, w_exp_gate, w_exp_up, w_exp_down, w_sh_gate, w_sh_up, w_sh_down):
    cos, sin = axial_rope_tables(x_sample.shape[1])
    xp = x_prompt
    xs = x_sample
    ks_new = []
    vs_new = []
    for l in range(DEPTH):
        sh_a, sc_a, gt_a, sh_m, sc_m, gt_m = adaln_mods(c_ctx[None, :], w_ada[l], b_ada[l])
        h = modulate(rmsnorm(xp, norm_mix_g[l]), sh_a, sc_a)
        y_conv, q, k, v, g_conv, g_attn = mixer_inputs(h, w_in[l], conv_w[l], w_conv_out[l], q_norm_g[l], k_norm_g[l])
        attn = context_attention(q, k, v, attn_sink[l])
        xp = xp + gt_a * merge_branches(y_conv, attn, g_conv, g_attn, w_attn_out[l], w_out[l])
        h = modulate(rmsnorm(xp, norm_ffn_g[l]), sh_m, sc_m)
        xp = xp + gt_m * moe_ffn(h, router_w[l], router_bias[l], w_exp_gate[l], w_exp_up[l], w_exp_down[l],
                                 w_sh_gate[l], w_sh_up[l], w_sh_down[l])
        ks_new.append(k)
        vs_new.append(v)

        sh_a, sc_a, gt_a, sh_m, sc_m, gt_m = adaln_mods(c, w_ada[l], b_ada[l])
        h = modulate(rmsnorm(xs, norm_mix_g[l]), sh_a, sc_a)
        y_conv, q, k, v, g_conv, g_attn = mixer_inputs(h, w_in[l], conv_w[l], w_conv_out[l], q_norm_g[l], k_norm_g[l])
        q = apply_rope(q, cos, sin)
        k = apply_rope(k, cos, sin)
        attn = latent_attention(q, k, v, cache_k[:, l], cache_v[:, l], attn_sink[l])
        xs = xs + gt_a * merge_branches(y_conv, attn, g_conv, g_attn, w_attn_out[l], w_out[l])
        h = modulate(rmsnorm(xs, norm_ffn_g[l]), sh_m, sc_m)
        xs = xs + gt_m * moe_ffn(h, router_w[l], router_bias[l], w_exp_gate[l], w_exp_up[l], w_exp_down[l],
                                 w_sh_gate[l], w_sh_up[l], w_sh_down[l])
    new_k = jnp.stack(ks_new, axis=1)
    new_v = jnp.stack(vs_new, axis=1)
    return (xp, xs, new_k, new_v)
```

```python
import functools

import jax
import jax.numpy as jnp
import numpy as np
from jax import lax
from jax.experimental import pallas as pl
from jax.experimental.pallas import tpu as pltpu

F32 = jnp.float32
BF16 = jnp.bfloat16
U32 = jnp.uint32
I32 = jnp.int32

D_MODEL = 2048
BATCH = 16
SEQ = 256
DEC_BATCH = 2
DEC_SEQ = 4096
PAST_LEN = 512
GRID_W = 64
N_HEADS = 16
N_KV_HEADS = 4
HEAD_DIM = 128
GROUP = N_HEADS // N_KV_HEADS
ATTN_WIDTH = N_HEADS * HEAD_DIM
KV_WIDTH = N_KV_HEADS * HEAD_DIM
WINDOW = 128
CONV_WIDTH = D_MODEL
N_EXPERTS = 64
TOP_K = 8
N_EXPERT_GROUPS = 8
GROUP_SIZE = N_EXPERTS // N_EXPERT_GROUPS
TOPK_GROUPS = 4
D_EXPERT = 512
D_SHARED = 512
ROUTED_SCALE = 2.5
ROPE_THETA = 10000.0
EPS = 1e-6
NEG_INF = -1e30
ATTN_SCALE = HEAD_DIM ** -0.5

T_P = BATCH * SEQ
T_S = DEC_BATCH * DEC_SEQ
T_ALL = T_P + T_S
MOD_GROUP = 4096
N_MOD = 1 + DEC_BATCH
assert T_P == MOD_GROUP and DEC_SEQ == MOD_GROUP

COL_CB, COL_CC, COL_CX, COL_Q, COL_GC, COL_GA = (i * D_MODEL for i in range(6))
COL_K = 6 * D_MODEL
COL_V = COL_K + KV_WIDTH
IN_WIDTH = COL_V + KV_WIDTH

TM_IN = 512
TN_IN = 1024
TM = 256
TILE_E = 256
N_SORTED = T_ALL * TOP_K + N_EXPERTS * TILE_E
N_TILES_E = N_SORTED // TILE_E
TMAP_W = 512
HALF = D_MODEL // 2
VMEM_LIMIT = 56 * 1024 * 1024


def _cparams(sem):
    return pltpu.CompilerParams(dimension_semantics=sem, vmem_limit_bytes=VMEM_LIMIT)


def _silu(x):
    return x * jax.nn.sigmoid(x)


def _pack_pair(lo_f32, hi_f32):
    lo = pltpu.bitcast(lo_f32, U32) >> 16
    hi = pltpu.bitcast(hi_f32, U32) & jnp.uint32(0xFFFF0000)
    return lo | hi


def _unpack_pair(p_u32):
    lo = pltpu.bitcast(p_u32 << 16, F32)
    hi = pltpu.bitcast(p_u32 & jnp.uint32(0xFFFF0000), F32)
    return lo, hi


ADA_TN = 1024
ADA_CHUNK = 256


def _ada_kernel(ct_ref, w_ref, b_ref, o_ref):
    tn = w_ref.shape[1]

    def body(c, accs):
        k0 = pl.multiple_of(c * ADA_CHUNK, ADA_CHUNK)
        wch = w_ref[pl.ds(k0, ADA_CHUNK), :]
        sch = _silu(ct_ref[pl.ds(k0, ADA_CHUNK), :])
        out = []
        for r in range(N_MOD):
            p = wch * sch[:, r:r + 1]
            out.append(accs[r] + p.reshape(ADA_CHUNK // 8, 8, tn).sum(axis=0))
        return tuple(out)

    accs = lax.fori_loop(0, D_MODEL // ADA_CHUNK, body,
                         tuple(jnp.zeros((8, tn), F32) for _ in range(N_MOD)))
    o_ref[...] = jnp.zeros_like(o_ref)
    for r in range(N_MOD):
        o_ref[r:r + 1, :] = jnp.sum(accs[r], axis=0, keepdims=True) + b_ref[...]


def _ada(cond_t, w_ada, b_ada):
    n = w_ada.shape[1]
    return pl.pallas_call(
        _ada_kernel,
        grid=(n // ADA_TN,),
        in_specs=[pl.BlockSpec((D_MODEL, 8), lambda j: (0, 0)),
                  pl.BlockSpec((D_MODEL, ADA_TN), lambda j: (0, j)),
                  pl.BlockSpec((1, ADA_TN), lambda j: (0, j))],
        out_specs=pl.BlockSpec((8, ADA_TN), lambda j: (0, j)),
        out_shape=jax.ShapeDtypeStruct((8, n), F32),
        compiler_params=_cparams(("arbitrary",)),
        name="ada",
    )(cond_t, w_ada, b_ada)


def _rms_mod(x, g, shift, scale):
    ms = jnp.mean(x * x, axis=-1, keepdims=True)
    y = x * lax.rsqrt(ms + EPS) * g
    return y * (1.0 + scale) + shift


def _head_norm_rope(a, g, cos2, sin2, scale):
    ms = jnp.mean(a * a, axis=-1, keepdims=True)
    a = a * lax.rsqrt(ms + EPS) * g
    a = a * cos2 + pltpu.roll(a, HEAD_DIM // 2, 1) * sin2
    return a * scale


IN_ROW_CHUNK = 128
J_Q0 = COL_Q // TN_IN
J_Q1 = J_Q0 + ATTN_WIDTH // TN_IN
J_KV = COL_K // TN_IN


def _inproj_kernel(x_ref, mod_ref, g_ref, w_ref, qg_ref, kg_ref, cos_ref, sin_ref, o_ref, kv_ref, h_scr):
    j = pl.program_id(1)

    @pl.when(j == 0)
    def _():
        shift = mod_ref[0, 0:1, :]
        scale = mod_ref[0, 1:2, :]
        g = g_ref[...]
        for c in range(TM_IN // IN_ROW_CHUNK):
            rows = pl.ds(c * IN_ROW_CHUNK, IN_ROW_CHUNK)
            h_scr[rows, :] = _rms_mod(x_ref[rows, :], g, shift, scale).astype(BF16)

    acc = jnp.dot(h_scr[...], w_ref[...], preferred_element_type=F32)
    is_q = jnp.logical_and(j >= J_Q0, j < J_Q1)
    is_kv = j == J_KV

    @pl.when(jnp.logical_not(jnp.logical_or(is_q, is_kv)))
    def _():
        o_ref[...] = acc.astype(BF16)

    @pl.when(is_q)
    def _():
        cos2 = cos_ref[...]
        sin2 = sin_ref[...]
        g = qg_ref[...]
        for h in range(TN_IN // HEAD_DIM):
            cols = slice(h * HEAD_DIM, (h + 1) * HEAD_DIM)
            o_ref[:, cols] = _head_norm_rope(acc[:, cols], g, cos2, sin2, ATTN_SCALE).astype(BF16)

    @pl.when(is_kv)
    def _():
        cos2 = cos_ref[...]
        sin2 = sin_ref[...]
        g = kg_ref[...]
        for h in range(N_KV_HEADS):
            cols = slice(h * HEAD_DIM, (h + 1) * HEAD_DIM)
            kh = _head_norm_rope(acc[:, cols], g, cos2, sin2, 1.0)
            kv_ref[:, cols] = kh
            o_ref[:, cols] = kh.astype(BF16)
        v = acc[:, KV_WIDTH:]
        kv_ref[:, KV_WIDTH:] = v
        o_ref[:, KV_WIDTH:] = v.astype(BF16)


def _inproj(x_all, mods3, g_mix, w_in_b, qg, kg, cos_tab, sin_tab):
    tiles_per_group = MOD_GROUP // TM_IN
    pos_tiles = DEC_SEQ // TM_IN

    def tab_map(i, j):
        return (jnp.where(i < T_P // TM_IN, 0, 1 + i % pos_tiles), 0)

    return pl.pallas_call(
        _inproj_kernel,
        grid=(T_ALL // TM_IN, IN_WIDTH // TN_IN),
        in_specs=[pl.BlockSpec((TM_IN, D_MODEL), lambda i, j: (i, 0)),
                  pl.BlockSpec((1, 6, D_MODEL), lambda i, j: (i // tiles_per_group, 0, 0)),
                  pl.BlockSpec((1, D_MODEL), lambda i, j: (0, 0)),
                  pl.BlockSpec((D_MODEL, TN_IN), lambda i, j: (0, j)),
                  pl.BlockSpec((1, HEAD_DIM), lambda i, j: (0, 0)),
                  pl.BlockSpec((1, HEAD_DIM), lambda i, j: (0, 0)),
                  pl.BlockSpec((TM_IN, HEAD_DIM), tab_map),
                  pl.BlockSpec((TM_IN, HEAD_DIM), tab_map)],
        out_specs=[pl.BlockSpec((TM_IN, TN_IN), lambda i, j: (i, j)),
                   pl.BlockSpec((TM_IN, 2 * KV_WIDTH), lambda i, j: (i, 0))],
        out_shape=[jax.ShapeDtypeStruct((T_ALL, IN_WIDTH), BF16),
                   jax.ShapeDtypeStruct((T_ALL, 2 * KV_WIDTH), F32)],
        scratch_shapes=[pltpu.VMEM((TM_IN, D_MODEL), BF16)],
        compiler_params=_cparams(("arbitrary", "arbitrary")),
        name="inproj",
    )(x_all, mods3, g_mix, w_in_b, qg, kg, cos_tab, sin_tab)


def _dot_nt(a, b):
    return lax.dot_general(a, b, (((1,), (1,)), ((), ())), preferred_element_type=F32)


def _attn_p_kernel(sink_ref, q_ref, k_ref, v_ref, o_ref):
    for kh in range(N_KV_HEADS):
        kcols = slice(kh * HEAD_DIM, (kh + 1) * HEAD_DIM)
        k = k_ref[:, kcols]
        v = v_ref[:, kcols]
        for gi in range(GROUP):
            h = kh * GROUP + gi
            cols = slice(h * HEAD_DIM, (h + 1) * HEAD_DIM)
            s = _dot_nt(q_ref[:, cols], k)
            sk = sink_ref[h]
            m = jnp.maximum(jnp.max(s, axis=-1, keepdims=True), sk)
            p = jnp.exp(s - m)
            den = jnp.sum(p, axis=-1, keepdims=True) + jnp.exp(sk - m)
            o = jnp.dot(p.astype(BF16), v, preferred_element_type=F32) / den
            o_ref[:, cols] = o.astype(BF16)


def _attn_p(sink, proj):
    return pl.pallas_call(
        _attn_p_kernel,
        grid=(BATCH,),
        in_specs=[pl.BlockSpec(memory_space=pltpu.SMEM),
                  pl.BlockSpec((SEQ, ATTN_WIDTH), lambda b: (b, COL_Q // ATTN_WIDTH)),
                  pl.BlockSpec((SEQ, KV_WIDTH), lambda b: (b, COL_K // KV_WIDTH)),
                  pl.BlockSpec((SEQ, KV_WIDTH), lambda b: (b, COL_V // KV_WIDTH))],
        out_specs=pl.BlockSpec((SEQ, ATTN_WIDTH), lambda b: (b, 0)),
        out_shape=jax.ShapeDtypeStruct((T_P, ATTN_WIDTH), BF16),
        compiler_params=_cparams(("arbitrary",)),
        name="attn_p",
    )(sink, proj, proj, proj)


QB = 256
BAND = QB + 2 * WINDOW


def _attn_s_kernel(sink_ref, q_ref, k_ref, v_ref, kc_ref, vc_ref, o_ref):
    i = pl.program_id(1)
    start = jnp.clip(i * QB - WINDOW, 0, DEC_SEQ - BAND)
    start = pl.multiple_of(start, WINDOW)
    qpos = i * QB + lax.broadcasted_iota(I32, (QB, BAND), 0)
    kpos = start + lax.broadcasted_iota(I32, (QB, BAND), 1)
    valid = jnp.abs(qpos - kpos) <= WINDOW
    for kh in range(N_KV_HEADS):
        kcols = slice(kh * HEAD_DIM, (kh + 1) * HEAD_DIM)
        kb = k_ref[pl.ds(start, BAND), kcols]
        vb = v_ref[pl.ds(start, BAND), kcols]
        kc = kc_ref[0, :, kcols]
        vc = vc_ref[0, :, kcols]
        for gi in range(GROUP):
            h = kh * GROUP + gi
            cols = slice(h * HEAD_DIM, (h + 1) * HEAD_DIM)
            q = q_ref[:, cols]
            s_loc = jnp.where(valid, _dot_nt(q, kb), NEG_INF)
            s_ctx = _dot_nt(q, kc)
            sk = sink_ref[h]
            m = jnp.maximum(jnp.maximum(jnp.max(s_loc, axis=-1, keepdims=True),
                                        jnp.max(s_ctx, axis=-1, keepdims=True)), sk)
            p_loc = jnp.exp(s_loc - m)
            p_ctx = jnp.exp(s_ctx - m)
            den = (jnp.sum(p_loc, axis=-1, keepdims=True) + jnp.sum(p_ctx, axis=-1, keepdims=True)
                   + jnp.exp(sk - m))
            o = (jnp.dot(p_loc.astype(BF16), vb, preferred_element_type=F32)
                 + jnp.dot(p_ctx.astype(BF16), vc, preferred_element_type=F32)) / den
            o_ref[:, cols] = o.astype(BF16)


def _attn_s(sink, proj, kc, vc):
    nq = DEC_SEQ // QB
    first = T_P // QB
    seq_blk = T_P // DEC_SEQ
    return pl.pallas_call(
        _attn_s_kernel,
        grid=(DEC_BATCH, nq),
        in_specs=[pl.BlockSpec(memory_space=pltpu.SMEM),
                  pl.BlockSpec((QB, ATTN_WIDTH), lambda b, i: (first + b * nq + i, COL_Q // ATTN_WIDTH)),
                  pl.BlockSpec((DEC_SEQ, KV_WIDTH), lambda b, i: (seq_blk + b, COL_K // KV_WIDTH)),
                  pl.BlockSpec((DEC_SEQ, KV_WIDTH), lambda b, i: (seq_blk + b, COL_V // KV_WIDTH)),
                  pl.BlockSpec((1, PAST_LEN, KV_WIDTH), lambda b, i: (b, 0, 0)),
                  pl.BlockSpec((1, PAST_LEN, KV_WIDTH), lambda b, i: (b, 0, 0))],
        out_specs=pl.BlockSpec((QB, ATTN_WIDTH), lambda b, i: (b * nq + i, 0)),
        out_shape=jax.ShapeDtypeStruct((T_S, ATTN_WIDTH), BF16),
        compiler_params=_cparams(("arbitrary", "arbitrary")),
        name="attn_s",
    )(sink, proj, proj, proj, kc, vc)


HALO = 16


def _conv_kernel(cb_ref, cc_ref, cx_ref, ccp_ref, cxp_ref, ccn_ref, cxn_ref, gc_ref, cw_ref, w_ref, o_ref):
    i = pl.program_id(0)
    n_ctx_tiles = T_P // TM
    per_seq = DEC_SEQ // TM
    is_first = jnp.logical_or(i < n_ctx_tiles, (i - n_ctx_tiles) % per_seq == 0)
    is_last = jnp.logical_or(i < n_ctx_tiles, (i - n_ctx_tiles) % per_seq == per_seq - 1)
    p = cc_ref[...].astype(F32) * cx_ref[...].astype(F32)
    prev_row = ccp_ref[HALO - 1:HALO, :].astype(F32) * cxp_ref[HALO - 1:HALO, :].astype(F32)
    next_row = ccn_ref[0:1, :].astype(F32) * cxn_ref[0:1, :].astype(F32)
    prev_row = jnp.where(is_first, 0.0, prev_row)
    next_row = jnp.where(is_last, 0.0, next_row)
    rows = lax.broadcasted_iota(I32, (TM, 1), 0)
    p_prev = jnp.where(rows == 0, prev_row, pltpu.roll(p, 1, 0))
    p_next = jnp.where(rows == TM - 1, next_row, pltpu.roll(p, TM - 1, 0))
    conv = p_prev * cw_ref[0:1, :] + p * cw_ref[1:2, :] + p_next * cw_ref[2:3, :]
    u = cb_ref[...].astype(F32) * conv
    y = jnp.dot(u.astype(BF16), w_ref[...], preferred_element_type=F32)
    o_ref[...] = (jax.nn.sigmoid(gc_ref[...].astype(F32)) * y).astype(BF16)


def _conv(proj, conv_w, w_conv_out_b):
    hb = TM // HALO
    last_hb = T_ALL // HALO - 1
    wide = lambda c: pl.BlockSpec((TM, D_MODEL), lambda i: (i, c // D_MODEL))
    prev = lambda c: pl.BlockSpec((HALO, D_MODEL), lambda i: (jnp.maximum(i * hb - 1, 0), c // D_MODEL))
    nxt = lambda c: pl.BlockSpec((HALO, D_MODEL), lambda i: (jnp.minimum((i + 1) * hb, last_hb), c // D_MODEL))
    return pl.pallas_call(
        _conv_kernel,
        grid=(T_ALL // TM,),
        in_specs=[wide(COL_CB), wide(COL_CC), wide(COL_CX),
                  prev(COL_CC), prev(COL_CX), nxt(COL_CC), nxt(COL_CX),
                  wide(COL_GC),
                  pl.BlockSpec((3, CONV_WIDTH), lambda i: (0, 0)),
                  pl.BlockSpec((CONV_WIDTH, D_MODEL), lambda i: (0, 0))],
        out_specs=pl.BlockSpec((TM, D_MODEL), lambda i: (i, 0)),
        out_shape=jax.ShapeDtypeStruct((T_ALL, D_MODEL), BF16),
        compiler_params=_cparams(("arbitrary",)),
        name="conv",
    )(proj, proj, proj, proj, proj, proj, proj, proj, conv_w, w_conv_out_b)


def _merge_kernel(x_ref, attn_p_ref, attn_s_ref, z1_ref, ga_ref, mod_ref, g2_ref, wao_ref, wo_ref, wrh_ref,
                  wrl_ref, x1_ref, h2_ref, h2p_ref, lg_ref):
    attn = jnp.where(pl.program_id(0) < T_P // TM, attn_p_ref[...], attn_s_ref[...])
    ya = jnp.dot(attn, wao_ref[...], preferred_element_type=F32)
    z = z1_ref[...].astype(F32) + jax.nn.sigmoid(ga_ref[...].astype(F32)) * ya
    mix = jnp.dot(z.astype(BF16), wo_ref[...], preferred_element_type=F32)
    x1 = x_ref[...] + mod_ref[0, 2:3, :] * mix
    x1_ref[...] = x1
    h = _rms_mod(x1, g2_ref[...], mod_ref[0, 3:4, :], mod_ref[0, 4:5, :])
    h_hi = h.astype(BF16)
    h_hi32 = h_hi.astype(F32)
    h_lo = (h - h_hi32).astype(BF16)
    h2_ref[...] = h_hi
    h2p_ref[...] = _pack_pair(h_hi32[:, :HALF], h_hi32[:, HALF:])
    lg = (jnp.dot(h_hi, wrh_ref[...], preferred_element_type=F32)
          + jnp.dot(h_hi, wrl_ref[...], preferred_element_type=F32)
          + jnp.dot(h_lo, wrh_ref[...], preferred_element_type=F32))
    lg_ref[...] = lg


def _merge(x_all, attn_p, attn_s, z1, proj, mods3, g_ffn, wao_b, wo_b, wr_hi, wr_lo):
    tiles_per_group = MOD_GROUP // TM
    n_ctx = T_P // TM
    const = lambda shape: pl.BlockSpec(shape, lambda i: (0, 0))
    row = lambda w: pl.BlockSpec((TM, w), lambda i: (i, 0))
    return pl.pallas_call(
        _merge_kernel,
        grid=(T_ALL // TM,),
        in_specs=[row(D_MODEL),
                  pl.BlockSpec((TM, ATTN_WIDTH), lambda i: (jnp.minimum(i, n_ctx - 1), 0)),
                  pl.BlockSpec((TM, ATTN_WIDTH), lambda i: (jnp.maximum(i - n_ctx, 0), 0)),
                  row(D_MODEL),
                  pl.BlockSpec((TM, D_MODEL), lambda i: (i, COL_GA // D_MODEL)),
                  pl.BlockSpec((1, 6, D_MODEL), lambda i: (i // tiles_per_group, 0, 0)),
                  const((1, D_MODEL)), const((ATTN_WIDTH, D_MODEL)), const((D_MODEL, D_MODEL)),
                  const((D_MODEL, 128)), const((D_MODEL, 128))],
        out_specs=[row(D_MODEL), row(D_MODEL), row(HALF), row(128)],
        out_shape=[jax.ShapeDtypeStruct((T_ALL, D_MODEL), F32),
                   jax.ShapeDtypeStruct((T_ALL, D_MODEL), BF16),
                   jax.ShapeDtypeStruct((T_ALL, HALF), U32),
                   jax.ShapeDtypeStruct((T_ALL, 128), F32)],
        compiler_params=_cparams(("arbitrary",)),
        name="merge",
    )(x_all, attn_p, attn_s, z1, proj, mods3, g_ffn, wao_b, wo_b, wr_hi, wr_lo)


def _route1_kernel(lg_ref, bias_ref, enc_ref, cnt_ref):
    i = pl.program_id(0)
    lt = lg_ref[...].T[:N_EXPERTS, :]
    scores = jax.nn.sigmoid(lt)
    biased = scores + bias_ref[...]
    b3 = biased.reshape(N_EXPERT_GROUPS, GROUP_SIZE, TM)
    mi = lax.broadcasted_iota(I32, b3.shape, 1)
    m1 = jnp.max(b3, axis=1, keepdims=True)
    idx1 = jnp.min(jnp.where(b3 == m1, mi, GROUP_SIZE), axis=1, keepdims=True)
    m2 = jnp.max(jnp.where(mi == idx1, -jnp.inf, b3), axis=1, keepdims=True)
    gs = (m1 + m2).reshape(N_EXPERT_GROUPS, TM)
    gidx = lax.broadcasted_iota(I32, gs.shape, 0)
    grank = jnp.zeros(gs.shape, I32)
    for j in range(N_EXPERT_GROUPS):
        gj = gs[j:j + 1, :]
        beats = jnp.logical_or(gj > gs, jnp.logical_and(gj == gs, j < gidx))
        grank = grank + beats.astype(I32)
    gsel = grank < TOPK_GROUPS
    emask = jnp.broadcast_to(gsel[:, None, :], b3.shape).reshape(N_EXPERTS, TM)
    masked = jnp.where(emask, biased, NEG_INF)
    eidx = lax.broadcasted_iota(I32, masked.shape, 0)
    erank = jnp.zeros(masked.shape, I32)
    for j in range(N_EXPERTS):
        vj = masked[j:j + 1, :]
        beats = jnp.logical_or(vj > masked, jnp.logical_and(vj == masked, j < eidx))
        erank = erank + beats.astype(I32)
    sel = erank < TOP_K
    wsel = jnp.where(sel, scores, 0.0)
    den = jnp.sum(wsel, axis=0, keepdims=True)
    wts = wsel / den * ROUTED_SCALE
    enc_ref[...] = jnp.where(sel, wts, -1.0)

    @pl.when(i == 0)
    def _():
        cnt_ref[...] = jnp.zeros_like(cnt_ref)

    cnt = jnp.sum(sel.astype(F32), axis=1, keepdims=True)
    cnt_ref[...] += jnp.broadcast_to(cnt, cnt_ref.shape)


def _route1(logits, bias_col):
    return pl.pallas_call(
        _route1_kernel,
        grid=(T_ALL // TM,),
        in_specs=[pl.BlockSpec((TM, 128), lambda i: (i, 0)),
                  pl.BlockSpec((N_EXPERTS, 1), lambda i: (0, 0))],
        out_specs=[pl.BlockSpec((N_EXPERTS, TM), lambda i: (0, i)),
                   pl.BlockSpec((N_EXPERTS, 128), lambda i: (0, 0))],
        out_shape=[jax.ShapeDtypeStruct((N_EXPERTS, T_ALL), F32),
                   jax.ShapeDtypeStruct((N_EXPERTS, 128), F32)],
        compiler_params=_cparams(("arbitrary",)),
        name="route1",
    )(logits, bias_col)


def _route2_kernel(enc_ref, cnt_ref, meta_ref, tmap_ref, carry_ref):
    i = pl.program_id(0)

    @pl.when(i == 0)
    def _():
        carry_ref[...] = jnp.zeros_like(carry_ref)

    enc = enc_ref[...]
    sel = enc >= 0.0
    wts = jnp.maximum(enc, 0.0)
    sel_b = sel.astype(BF16)
    ntile = jnp.ceil(cnt_ref[...] * (1.0 / TILE_E))
    er = lax.broadcasted_iota(I32, (N_EXPERTS, N_EXPERTS), 0)
    ec = lax.broadcasted_iota(I32, (N_EXPERTS, N_EXPERTS), 1)
    lower = (ec < er).astype(BF16)
    off_t = jnp.dot(lower, ntile.astype(BF16), preferred_element_type=F32)
    tr = lax.broadcasted_iota(I32, (TM, TM), 0)
    tc = lax.broadcasted_iota(I32, (TM, TM), 1)
    upper = (tr < tc).astype(BF16)
    rank = jnp.dot(sel_b, upper, preferred_element_type=F32) + carry_ref[:, 0:1]
    carry_ref[...] += jnp.broadcast_to(jnp.sum(sel.astype(F32), axis=1, keepdims=True), carry_ref.shape)
    pos = off_t[:, 0:1] * float(TILE_E) + rank
    slot = jnp.dot(lower, sel_b, preferred_element_type=F32)
    rows = []
    for k in range(TOP_K):
        mk = jnp.logical_and(sel, slot == float(k))
        rows.append(jnp.sum(jnp.where(mk, wts, 0.0), axis=0, keepdims=True))
    for k in range(TOP_K):
        mk = jnp.logical_and(sel, slot == float(k))
        rows.append(jnp.sum(jnp.where(mk, pos, 0.0), axis=0, keepdims=True))
    rows.append(jnp.zeros((128 - 2 * TOP_K, TM), F32))
    meta_ref[...] = jnp.concatenate(rows, axis=0).T
    end_t = off_t[:, 0:1] + ntile[:, 0:1]
    ti = lax.broadcasted_iota(I32, (N_EXPERTS, TMAP_W), 1).astype(F32)
    te = jnp.sum((end_t <= ti).astype(F32), axis=0, keepdims=True)
    te = jnp.minimum(te, float(N_EXPERTS - 1))
    nact = jnp.sum(ntile[:, 0:1], axis=0, keepdims=True)
    tmap_ref[...] = jnp.zeros_like(tmap_ref)
    tmap_ref[0:1, :] = te.astype(I32)
    tmap_ref[1:2, :] = jnp.broadcast_to(nact, (1, TMAP_W)).astype(I32)


def _route2(enc, cnt):
    return pl.pallas_call(
        _route2_kernel,
        grid=(T_ALL // TM,),
        in_specs=[pl.BlockSpec((N_EXPERTS, TM), lambda i: (0, i)),
                  pl.BlockSpec((N_EXPERTS, 128), lambda i: (0, 0))],
        out_specs=[pl.BlockSpec((TM, 128), lambda i: (i, 0)),
                   pl.BlockSpec((8, TMAP_W), lambda i: (0, 0))],
        out_shape=[jax.ShapeDtypeStruct((T_ALL, 128), F32),
                   jax.ShapeDtypeStruct((8, TMAP_W), I32)],
        scratch_shapes=[pltpu.VMEM((N_EXPERTS, 128), F32)],
        compiler_params=_cparams(("arbitrary",)),
        name="route2",
    )(enc, cnt)


def _row_copies_wait(src_tile_like, dst_tile_like, sem):
    pltpu.make_async_copy(src_tile_like, dst_tile_like, sem).wait()


def _dispatch_kernel(pos_ref, h_ref, xs_in_ref, xs_ref, sem):
    del xs_in_ref

    def body(t, carry):
        for k in range(TOP_K):
            r = pos_ref[t * TOP_K + k]
            pltpu.make_async_copy(h_ref.at[pl.ds(t, 1)], xs_ref.at[pl.ds(r, 1)], sem).start()
        return carry

    lax.fori_loop(0, TM, body, 0)
    for k in range(TOP_K):
        _row_copies_wait(h_ref, xs_ref.at[pl.ds(0, TM)], sem)


def _dispatch(pos_flat, h2p, xs_init):
    return pl.pallas_call(
        _dispatch_kernel,
        grid=(T_ALL // TM,),
        in_specs=[pl.BlockSpec((TM * TOP_K,), lambda i: (i,), memory_space=pltpu.SMEM),
                  pl.BlockSpec((TM, HALF), lambda i: (i, 0)),
                  pl.BlockSpec(memory_space=pl.ANY)],
        out_specs=pl.BlockSpec(memory_space=pl.ANY),
        out_shape=jax.ShapeDtypeStruct((N_SORTED, HALF), U32),
        scratch_shapes=[pltpu.SemaphoreType.DMA(())],
        input_output_aliases={2: 0},
        compiler_params=_cparams(("arbitrary",)),
        name="dispatch",
    )(pos_flat, h2p, xs_init)


def _expert_kernel(te_ref, na_ref, x_ref, wg_ref, wu_ref, wd_ref, o_ref):
    del te_ref
    i = pl.program_id(0)

    @pl.when(i < na_ref[0])
    def _():
        lo, hi = _unpack_pair(x_ref[...])
        a = lo.astype(BF16)
        b = hi.astype(BF16)
        g = (jnp.dot(a, wg_ref[0, :HALF, :], preferred_element_type=F32)
             + jnp.dot(b, wg_ref[0, HALF:, :], preferred_element_type=F32))
        u = (jnp.dot(a, wu_ref[0, :HALF, :], preferred_element_type=F32)
             + jnp.dot(b, wu_ref[0, HALF:, :], preferred_element_type=F32))
        act = (_silu(g) * u).astype(BF16)
        y = jnp.dot(act, wd_ref[0], preferred_element_type=F32)
        yb = y.astype(BF16).astype(F32)
        o_ref[...] = _pack_pair(yb[:, :HALF], yb[:, HALF:])

    @pl.when(i >= na_ref[0])
    def _():
        o_ref[...] = jnp.zeros_like(o_ref)


def _expert(tile_expert, n_active, xs, wg_b, wu_b, wd_b):
    def row_map(i, te, na):
        return (jnp.minimum(i, na[0] - 1), 0)

    def out_map(i, te, na):
        return (i, 0)

    def w_map(i, te, na):
        return (te[i], 0, 0)

    grid_spec = pltpu.PrefetchScalarGridSpec(
        num_scalar_prefetch=2,
        grid=(N_TILES_E,),
        in_specs=[pl.BlockSpec((TILE_E, HALF), row_map),
                  pl.BlockSpec((1, D_MODEL, D_EXPERT), w_map),
                  pl.BlockSpec((1, D_MODEL, D_EXPERT), w_map),
                  pl.BlockSpec((1, D_EXPERT, D_MODEL), w_map)],
        out_specs=pl.BlockSpec((TILE_E, HALF), out_map),
    )
    return pl.pallas_call(
        _expert_kernel,
        grid_spec=grid_spec,
        out_shape=jax.ShapeDtypeStruct((N_SORTED, HALF), U32),
        compiler_params=_cparams(("arbitrary",)),
        name="expert",
    )(tile_expert, n_active, xs, wg_b, wu_b, wd_b)


def _combine_kernel(pos_ref, meta_ref, ys_ref, h2_ref, x1_ref, mod_ref, wsg_ref, wsu_ref, wsd_ref,
                    o_ref, buf, sem):
    def body(t, carry):
        for k in range(TOP_K):
            r = pos_ref[t * TOP_K + k]
            pltpu.make_async_copy(ys_ref.at[pl.ds(r, 1)], buf.at[k, pl.ds(t, 1)], sem).start()
        return carry

    lax.fori_loop(0, TM, body, 0)
    h = h2_ref[...]
    sg = jnp.dot(h, wsg_ref[...], preferred_element_type=F32)
    su = jnp.dot(h, wsu_ref[...], preferred_element_type=F32)
    shared = jnp.dot((_silu(sg) * su).astype(BF16), wsd_ref[...], preferred_element_type=F32)
    for k in range(TOP_K):
        _row_copies_wait(ys_ref.at[pl.ds(0, TM)], buf.at[k], sem)
    r_lo = shared[:, :HALF]
    r_hi = shared[:, HALF:]
    for k in range(TOP_K):
        lo, hi = _unpack_pair(buf[k])
        w = meta_ref[:, k:k + 1]
        r_lo = r_lo + w * lo
        r_hi = r_hi + w * hi
    gate = mod_ref[0, 5:6, :]
    o_ref[:, :HALF] = x1_ref[:, :HALF] + gate[:, :HALF] * r_lo
    o_ref[:, HALF:] = x1_ref[:, HALF:] + gate[:, HALF:] * r_hi


def _combine(pos_flat, meta, ys, h2, x1, mods3, wsg_b, wsu_b, wsd_b):
    tiles_per_group = MOD_GROUP // TM
    const = lambda shape: pl.BlockSpec(shape, lambda i: (0, 0))
    row = lambda w: pl.BlockSpec((TM, w), lambda i: (i, 0))
    return pl.pallas_call(
        _combine_kernel,
        grid=(T_ALL // TM,),
        in_specs=[pl.BlockSpec((TM * TOP_K,), lambda i: (i,), memory_space=pltpu.SMEM),
                  row(128),
                  pl.BlockSpec(memory_space=pl.ANY),
                  row(D_MODEL), row(D_MODEL),
                  pl.BlockSpec((1, 6, D_MODEL), lambda i: (i // tiles_per_group, 0, 0)),
                  const((D_MODEL, D_SHARED)), const((D_MODEL, D_SHARED)), const((D_SHARED, D_MODEL))],
        out_specs=row(D_MODEL),
        out_shape=jax.ShapeDtypeStruct((T_ALL, D_MODEL), F32),
        scratch_shapes=[pltpu.VMEM((TOP_K, TM, HALF), U32), pltpu.SemaphoreType.DMA(())],
        compiler_params=_cparams(("arbitrary",)),
        name="combine",
    )(pos_flat, meta, ys, h2, x1, mods3, wsg_b, wsu_b, wsd_b)


def _rope_tables():
    rows = DEC_SEQ // GRID_W
    row = jnp.repeat(jnp.arange(rows, dtype=F32), GRID_W)
    col = jnp.tile(jnp.arange(GRID_W, dtype=F32), rows)
    n_freq = HEAD_DIM // 4
    inv = ROPE_THETA ** (-jnp.arange(n_freq, dtype=F32) / n_freq)
    ang = jnp.concatenate([row[:, None] * inv, col[:, None] * inv], axis=-1)
    cos, sin = jnp.cos(ang), jnp.sin(ang)
    cos2 = jnp.concatenate([cos, cos], axis=-1)
    sin2 = jnp.concatenate([-sin, sin], axis=-1)
    cos_tab = jnp.concatenate([jnp.ones((TM_IN, HEAD_DIM), F32), cos2], axis=0)
    sin_tab = jnp.concatenate([jnp.zeros((TM_IN, HEAD_DIM), F32), sin2], axis=0)
    return cos_tab, sin_tab


def kernel(x_prompt, x_sample, cache_k, cache_v, c, c_ctx, w_ada, b_ada, norm_mix_g, norm_ffn_g, w_in, conv_w,
           q_norm_g, k_norm_g, attn_sink, w_conv_out, w_attn_out, w_out, router_w, router_bias, w_exp_gate,
           w_exp_up, w_exp_down, w_sh_gate, w_sh_up, w_sh_down):
    l = 0
    x_all = jnp.concatenate([x_prompt.reshape(T_P, D_MODEL), x_sample.reshape(T_S, D_MODEL)], axis=0)

    cond = jnp.concatenate([c_ctx[None, :], c, jnp.zeros((8 - N_MOD, D_MODEL), F32)], axis=0)
    mods = _ada(cond.T, w_ada[l], b_ada[l][None, :])
    mods3 = mods[:N_MOD].reshape(N_MOD, 6, D_MODEL)

    s = np.cumsum((0, CONV_WIDTH, CONV_WIDTH, CONV_WIDTH, ATTN_WIDTH, KV_WIDTH, KV_WIDTH, D_MODEL, D_MODEL))
    wi = w_in[l]
    part = lambda n: wi[:, s[n]:s[n + 1]]
    w_in_b = jnp.concatenate([part(0), part(1), part(2), part(3), part(6), part(7), part(4), part(5)],
                             axis=1).astype(BF16)
    cos_tab, sin_tab = _rope_tables()
    proj, kv32 = _inproj(x_all, mods3, norm_mix_g[l][None, :], w_in_b, q_norm_g[l][None, :],
                         k_norm_g[l][None, :], cos_tab, sin_tab)

    sink = attn_sink[l]
    kc = cache_k[:, l].reshape(DEC_BATCH, PAST_LEN, KV_WIDTH).astype(BF16)
    vc = cache_v[:, l].reshape(DEC_BATCH, PAST_LEN, KV_WIDTH).astype(BF16)
    attn_p = _attn_p(sink, proj)
    attn_s = _attn_s(sink, proj, kc, vc)

    z1 = _conv(proj, conv_w[l], w_conv_out[l].astype(BF16))

    wr = router_w[l]
    wr_hi = wr.astype(BF16)
    wr_lo = (wr - wr_hi.astype(F32)).astype(BF16)
    pad = lambda a: jnp.pad(a, ((0, 0), (0, 128 - N_EXPERTS)))
    x1, h2, h2p, logits = _merge(x_all, attn_p, attn_s, z1, proj, mods3, norm_ffn_g[l][None, :],
                                 w_attn_out[l].astype(BF16), w_out[l].astype(BF16), pad(wr_hi), pad(wr_lo))

    enc, cnt = _route1(logits, router_bias[l][:, None])
    meta, tmap = _route2(enc, cnt)
    pos_flat = meta[:, TOP_K:2 * TOP_K].astype(I32).reshape(T_ALL * TOP_K)
    tile_expert = tmap[0, :N_TILES_E]
    n_active = tmap[1, :1]

    xs = _dispatch(pos_flat, h2p, jnp.zeros((N_SORTED, HALF), U32))
    ys = _expert(tile_expert, n_active, xs, w_exp_gate[l].astype(BF16), w_exp_up[l].astype(BF16),
                 w_exp_down[l].astype(BF16))
    y_all = _combine(pos_flat, meta, ys, h2, x1, mods3, w_sh_gate[l].astype(BF16), w_sh_up[l].astype(BF16),
                     w_sh_down[l].astype(BF16))

    y_prompt = y_all[:T_P].reshape(BATCH, SEQ, D_MODEL)
    y_sample = y_all[T_P:].reshape(DEC_BATCH, DEC_SEQ, D_MODEL)
    new_k = kv32[:T_P, :KV_WIDTH].reshape(BATCH, 1, SEQ, N_KV_HEADS, HEAD_DIM)
    new_v = kv32[:T_P, KV_WIDTH:].reshape(BATCH, 1, SEQ, N_KV_HEADS, HEAD_DIM)
    return (y_prompt, y_sample, new_k, new_v)
```

```python
import jax
import jax.numpy as jnp
from jax import lax
from jax.experimental import pallas as pl
from jax.experimental.pallas import tpu as pltpu

F32 = jnp.float32
BF16 = jnp.bfloat16
I32 = jnp.int32

D_MODEL = 2048
BATCH = 16
SEQ = 256
DEC_BATCH = 2
DEC_SEQ = 4096
PAST_LEN = 512
GRID_W = 64
N_HEADS = 16
N_KV_HEADS = 4
HEAD_DIM = 128
GROUP = N_HEADS // N_KV_HEADS
ATTN_WIDTH = N_HEADS * HEAD_DIM
KV_WIDTH = N_KV_HEADS * HEAD_DIM
WINDOW = 128
CONV_WIDTH = D_MODEL
N_EXPERTS = 64
TOP_K = 8
N_EXPERT_GROUPS = 8
GROUP_SIZE = N_EXPERTS // N_EXPERT_GROUPS
TOPK_GROUPS = 4
D_EXPERT = 512
D_SHARED = 512
ROUTED_SCALE = 2.5
ROPE_THETA = 10000.0
EPS = 1e-6
NEG_INF = -1e30
ATTN_SCALE = HEAD_DIM ** -0.5

T_P = BATCH * SEQ
T_S = DEC_BATCH * DEC_SEQ
T_ALL = T_P + T_S
MOD_GROUP = 4096
N_MOD = 1 + DEC_BATCH
assert T_P == MOD_GROUP and DEC_SEQ == MOD_GROUP

COL_CB, COL_CC, COL_CX, COL_Q, COL_GC, COL_GA = (i * D_MODEL for i in range(6))
COL_K = 6 * D_MODEL
COL_V = COL_K + KV_WIDTH
IN_WIDTH = COL_V + KV_WIDTH
W_COL_KV = 3 * CONV_WIDTH + ATTN_WIDTH

TM_IN = 1024
TN_IN = 1024
TM = 256
TM_C = 128
TILE_E = 256
N_SORTED = T_ALL * TOP_K + N_EXPERTS * TILE_E
N_TILES_E = N_SORTED // TILE_E
TMAP_W = 512
CAST_ROWS = 512
VMEM_LIMIT = 56 * 1024 * 1024


def _cparams(sem):
    return pltpu.CompilerParams(dimension_semantics=sem, vmem_limit_bytes=VMEM_LIMIT)


def _silu(x):
    return x * jax.nn.sigmoid(x)


def _load_cast_weight(w_hbm, w_scr, stage, sem):
    rows = stage.shape[0]
    for c in range(w_hbm.shape[0] // rows):
        cp = pltpu.make_async_copy(w_hbm.at[pl.ds(c * rows, rows)], stage, sem)
        cp.start()
        cp.wait()
        w_scr[pl.ds(c * rows, rows), :] = stage[...].astype(BF16)


ADA_TN = 1024
ADA_CHUNK = 256


def _ada_kernel(ct_ref, w_ref, b_ref, o_ref):
    tn = w_ref.shape[1]

    def body(c, accs):
        k0 = pl.multiple_of(c * ADA_CHUNK, ADA_CHUNK)
        wch = w_ref[pl.ds(k0, ADA_CHUNK), :]
        sch = _silu(ct_ref[pl.ds(k0, ADA_CHUNK), :])
        out = []
        for r in range(N_MOD):
            p = wch * sch[:, r:r + 1]
            out.append(accs[r] + p.reshape(ADA_CHUNK // 8, 8, tn).sum(axis=0))
        return tuple(out)

    accs = lax.fori_loop(0, D_MODEL // ADA_CHUNK, body,
                         tuple(jnp.zeros((8, tn), F32) for _ in range(N_MOD)))
    o_ref[...] = jnp.zeros_like(o_ref)
    for r in range(N_MOD):
        o_ref[r:r + 1, :] = jnp.sum(accs[r], axis=0, keepdims=True) + b_ref[...]


def _ada(cond_t, w_ada, b_ada):
    n = w_ada.shape[1]
    return pl.pallas_call(
        _ada_kernel,
        grid=(n // ADA_TN,),
        in_specs=[pl.BlockSpec((D_MODEL, 8), lambda j: (0, 0)),
                  pl.BlockSpec((D_MODEL, ADA_TN), lambda j: (0, j)),
                  pl.BlockSpec((1, ADA_TN), lambda j: (0, j))],
        out_specs=pl.BlockSpec((8, ADA_TN), lambda j: (0, j)),
        out_shape=jax.ShapeDtypeStruct((8, n), F32),
        compiler_params=_cparams(("arbitrary",)),
        name="ada",
    )(cond_t, w_ada, b_ada)


def _rms_mod(x, g, shift, scale):
    ms = jnp.mean(x * x, axis=-1, keepdims=True)
    y = x * lax.rsqrt(ms + EPS) * g
    return y * (1.0 + scale) + shift


def _two_group_specs(tile, width):
    n_ctx = T_P // tile
    return (pl.BlockSpec((tile, width), lambda i: (jnp.minimum(i, n_ctx - 1), 0)),
            pl.BlockSpec((tile, width), lambda i: (jnp.maximum(i - n_ctx, 0), 0)))


def _prenorm_kernel(xp_ref, xs_ref, mod_ref, g_ref, h_ref):
    i = pl.program_id(0)
    shift = mod_ref[0, 0:1, :]
    scale = mod_ref[0, 1:2, :]
    g = g_ref[...]

    @pl.when(i < T_P // TM)
    def _():
        h_ref[...] = _rms_mod(xp_ref[...], g, shift, scale).astype(BF16)

    @pl.when(i >= T_P // TM)
    def _():
        h_ref[...] = _rms_mod(xs_ref[...], g, shift, scale).astype(BF16)


def _prenorm(xp, xs, mods3, g_mix):
    tiles_per_group = MOD_GROUP // TM
    sp, ss = _two_group_specs(TM, D_MODEL)
    return pl.pallas_call(
        _prenorm_kernel,
        grid=(T_ALL // TM,),
        in_specs=[sp, ss,
                  pl.BlockSpec((1, 6, D_MODEL), lambda i: (i // tiles_per_group, 0, 0)),
                  pl.BlockSpec((1, D_MODEL), lambda i: (0, 0))],
        out_specs=pl.BlockSpec((TM, D_MODEL), lambda i: (i, 0)),
        out_shape=jax.ShapeDtypeStruct((T_ALL, D_MODEL), BF16),
        compiler_params=_cparams(("arbitrary",)),
        name="prenorm",
    )(xp, xs, mods3, g_mix)


def _head_norm_rope(a, g, cos2, sin2, scale):
    ms = jnp.mean(a * a, axis=-1, keepdims=True)
    a = a * lax.rsqrt(ms + EPS) * g
    a = a * cos2 + pltpu.roll(a, HEAD_DIM // 2, 1) * sin2
    return a * scale


N_J = IN_WIDTH // TN_IN
J_Q0 = COL_Q // TN_IN
J_Q1 = J_Q0 + ATTN_WIDTH // TN_IN
J_KV = COL_K // TN_IN
assert J_KV == N_J - 1
N_CTX_IN = T_P // TM_IN


def _inproj_kernel(h_ref, w_ref, qg_ref, kg_ref, cos_ref, sin_ref, o_ref, k_ref, v_ref, w_scr):
    j = pl.program_id(0)
    i = pl.program_id(1)

    @pl.when(i == 0)
    def _():
        for c in range(D_MODEL // CAST_ROWS):
            rows = pl.ds(c * CAST_ROWS, CAST_ROWS)
            w_scr[rows, :] = w_ref[rows, :].astype(BF16)

    acc = jnp.dot(h_ref[...], w_scr[...], preferred_element_type=F32)
    is_q = jnp.logical_and(j >= J_Q0, j < J_Q1)
    is_kv = j == J_KV

    @pl.when(jnp.logical_not(jnp.logical_or(is_q, is_kv)))
    def _():
        o_ref[...] = acc.astype(BF16)

    @pl.when(is_q)
    def _():
        cos2 = cos_ref[...]
        sin2 = sin_ref[...]
        g = qg_ref[...]
        for h in range(TN_IN // HEAD_DIM):
            cols = slice(h * HEAD_DIM, (h + 1) * HEAD_DIM)
            o_ref[:, cols] = _head_norm_rope(acc[:, cols], g, cos2, sin2, ATTN_SCALE).astype(BF16)

    @pl.when(is_kv)
    def _():
        cos2 = cos_ref[...]
        sin2 = sin_ref[...]
        g = kg_ref[...]
        is_ctx = i < N_CTX_IN
        for h in range(N_KV_HEADS):
            cols = slice(h * HEAD_DIM, (h + 1) * HEAD_DIM)
            kh = _head_norm_rope(acc[:, cols], g, cos2, sin2, 1.0)
            o_ref[:, cols] = kh.astype(BF16)

            @pl.when(is_ctx)
            def _():
                k_ref[:, cols] = kh

        v = acc[:, KV_WIDTH:]
        o_ref[:, KV_WIDTH:] = v.astype(BF16)

        @pl.when(is_ctx)
        def _():
            v_ref[...] = v


def _inproj(h, w_in, qg, kg, cos_tab, sin_tab):
    pos_tiles = DEC_SEQ // TM_IN

    def tab_map(j, i):
        return (jnp.where(i < N_CTX_IN, 0, 1 + i % pos_tiles), 0)

    def w_map(j, i):
        return (0, jnp.where(j < W_COL_KV // TN_IN, j, jnp.where(j == J_KV, W_COL_KV // TN_IN, j + 1)))

    def kv_map(j, i):
        return (jnp.where(j == J_KV, jnp.minimum(i, N_CTX_IN - 1), 0), 0)

    return pl.pallas_call(
        _inproj_kernel,
        grid=(N_J, T_ALL // TM_IN),
        in_specs=[pl.BlockSpec((TM_IN, D_MODEL), lambda j, i: (i, 0)),
                  pl.BlockSpec((D_MODEL, TN_IN), w_map, pipeline_mode=pl.Buffered(1)),
                  pl.BlockSpec((1, HEAD_DIM), lambda j, i: (0, 0)),
                  pl.BlockSpec((1, HEAD_DIM), lambda j, i: (0, 0)),
                  pl.BlockSpec((TM_IN, HEAD_DIM), tab_map),
                  pl.BlockSpec((TM_IN, HEAD_DIM), tab_map)],
        out_specs=[pl.BlockSpec((TM_IN, TN_IN), lambda j, i: (i, j)),
                   pl.BlockSpec((TM_IN, KV_WIDTH), kv_map),
                   pl.BlockSpec((TM_IN, KV_WIDTH), kv_map)],
        out_shape=[jax.ShapeDtypeStruct((T_ALL, IN_WIDTH), BF16),
                   jax.ShapeDtypeStruct((T_P, KV_WIDTH), F32),
                   jax.ShapeDtypeStruct((T_P, KV_WIDTH), F32)],
        scratch_shapes=[pltpu.VMEM((D_MODEL, TN_IN), BF16)],
        compiler_params=_cparams(("arbitrary", "arbitrary")),
        name="inproj",
    )(h, w_in, qg, kg, cos_tab, sin_tab)


def _dot_nt(a, b):
    return lax.dot_general(a, b, (((1,), (1,)), ((), ())), preferred_element_type=F32)


def _attn_p_kernel(sink_ref, q_ref, k_ref, v_ref, o_ref):
    for kh in range(N_KV_HEADS):
        kcols = slice(kh * HEAD_DIM, (kh + 1) * HEAD_DIM)
        k = k_ref[:, kcols]
        v = v_ref[:, kcols]
        for gi in range(GROUP):
            h = kh * GROUP + gi
            cols = slice(h * HEAD_DIM, (h + 1) * HEAD_DIM)
            s = _dot_nt(q_ref[:, cols], k)
            sk = sink_ref[h]
            m = jnp.maximum(jnp.max(s, axis=-1, keepdims=True), sk)
            p = jnp.exp(s - m)
            den = jnp.sum(p, axis=-1, keepdims=True) + jnp.exp(sk - m)
            o = jnp.dot(p.astype(BF16), v, preferred_element_type=F32) / den
            o_ref[:, cols] = o.astype(BF16)


def _attn_p(sink, proj):
    return pl.pallas_call(
        _attn_p_kernel,
        grid=(BATCH,),
        in_specs=[pl.BlockSpec(memory_space=pltpu.SMEM),
                  pl.BlockSpec((SEQ, ATTN_WIDTH), lambda b: (b, COL_Q // ATTN_WIDTH)),
                  pl.BlockSpec((SEQ, KV_WIDTH), lambda b: (b, COL_K // KV_WIDTH)),
                  pl.BlockSpec((SEQ, KV_WIDTH), lambda b: (b, COL_V // KV_WIDTH))],
        out_specs=pl.BlockSpec((SEQ, ATTN_WIDTH), lambda b: (b, 0)),
        out_shape=jax.ShapeDtypeStruct((T_P, ATTN_WIDTH), BF16),
        compiler_params=_cparams(("arbitrary",)),
        name="attn_p",
    )(sink, proj, proj, proj)


QB = 256
BAND = QB + 2 * WINDOW


def _attn_s_kernel(sink_ref, q_ref, k_ref, v_ref, kc_ref, vc_ref, o_ref):
    i = pl.program_id(1)
    start = jnp.clip(i * QB - WINDOW, 0, DEC_SEQ - BAND)
    start = pl.multiple_of(start, WINDOW)
    qpos = i * QB + lax.broadcasted_iota(I32, (QB, BAND), 0)
    kpos = start + lax.broadcasted_iota(I32, (QB, BAND), 1)
    valid = jnp.abs(qpos - kpos) <= WINDOW
    for kh in range(N_KV_HEADS):
        kcols = slice(kh * HEAD_DIM, (kh + 1) * HEAD_DIM)
        kb = k_ref[pl.ds(start, BAND), kcols]
        vb = v_ref[pl.ds(start, BAND), kcols]
        kc = kc_ref[0, :, kcols].astype(BF16)
        vc = vc_ref[0, :, kcols].astype(BF16)
        for gi in range(GROUP):
            h = kh * GROUP + gi
            cols = slice(h * HEAD_DIM, (h + 1) * HEAD_DIM)
            q = q_ref[:, cols]
            s_loc = jnp.where(valid, _dot_nt(q, kb), NEG_INF)
            s_ctx = _dot_nt(q, kc)
            sk = sink_ref[h]
            m = jnp.maximum(jnp.maximum(jnp.max(s_loc, axis=-1, keepdims=True),
                                        jnp.max(s_ctx, axis=-1, keepdims=True)), sk)
            p_loc = jnp.exp(s_loc - m)
            p_ctx = jnp.exp(s_ctx - m)
            den = (jnp.sum(p_loc, axis=-1, keepdims=True) + jnp.sum(p_ctx, axis=-1, keepdims=True)
                   + jnp.exp(sk - m))
            o = (jnp.dot(p_loc.astype(BF16), vb, preferred_element_type=F32)
                 + jnp.dot(p_ctx.astype(BF16), vc, preferred_element_type=F32)) / den
            o_ref[:, cols] = o.astype(BF16)


def _attn_s(sink, proj, kc, vc):
    nq = DEC_SEQ // QB
    first = T_P // QB
    seq_blk = T_P // DEC_SEQ
    return pl.pallas_call(
        _attn_s_kernel,
        grid=(DEC_BATCH, nq),
        in_specs=[pl.BlockSpec(memory_space=pltpu.SMEM),
                  pl.BlockSpec((QB, ATTN_WIDTH), lambda b, i: (first + b * nq + i, COL_Q // ATTN_WIDTH)),
                  pl.BlockSpec((DEC_SEQ, KV_WIDTH), lambda b, i: (seq_blk + b, COL_K // KV_WIDTH)),
                  pl.BlockSpec((DEC_SEQ, KV_WIDTH), lambda b, i: (seq_blk + b, COL_V // KV_WIDTH)),
                  pl.BlockSpec((1, PAST_LEN, KV_WIDTH), lambda b, i: (b, 0, 0)),
                  pl.BlockSpec((1, PAST_LEN, KV_WIDTH), lambda b, i: (b, 0, 0))],
        out_specs=pl.BlockSpec((QB, ATTN_WIDTH), lambda b, i: (b * nq + i, 0)),
        out_shape=jax.ShapeDtypeStruct((T_S, ATTN_WIDTH), BF16),
        compiler_params=_cparams(("arbitrary", "arbitrary")),
        name="attn_s",
    )(sink, proj, proj, proj, kc, vc)


HALO = 16


def _conv_kernel(cb_ref, cc_ref, cx_ref, ccp_ref, cxp_ref, ccn_ref, cxn_ref, gc_ref, cw_ref, w_hbm, o_ref,
                 w_scr, stage, sem):
    i = pl.program_id(0)

    @pl.when(i == 0)
    def _():
        _load_cast_weight(w_hbm, w_scr, stage, sem)

    n_ctx_tiles = T_P // TM
    per_seq = DEC_SEQ // TM
    is_first = jnp.logical_or(i < n_ctx_tiles, (i - n_ctx_tiles) % per_seq == 0)
    is_last = jnp.logical_or(i < n_ctx_tiles, (i - n_ctx_tiles) % per_seq == per_seq - 1)
    p = cc_ref[...].astype(F32) * cx_ref[...].astype(F32)
    prev_row = ccp_ref[HALO - 1:HALO, :].astype(F32) * cxp_ref[HALO - 1:HALO, :].astype(F32)
    next_row = ccn_ref[0:1, :].astype(F32) * cxn_ref[0:1, :].astype(F32)
    prev_row = jnp.where(is_first, 0.0, prev_row)
    next_row = jnp.where(is_last, 0.0, next_row)
    rows = lax.broadcasted_iota(I32, (TM, 1), 0)
    p_prev = jnp.where(rows == 0, prev_row, pltpu.roll(p, 1, 0))
    p_next = jnp.where(rows == TM - 1, next_row, pltpu.roll(p, TM - 1, 0))
    conv = p_prev * cw_ref[0:1, :] + p * cw_ref[1:2, :] + p_next * cw_ref[2:3, :]
    u = cb_ref[...].astype(F32) * conv
    y = jnp.dot(u.astype(BF16), w_scr[...], preferred_element_type=F32)
    o_ref[...] = (jax.nn.sigmoid(gc_ref[...].astype(F32)) * y).astype(BF16)


def _conv(proj, conv_w, w_conv_out):
    hb = TM // HALO
    last_hb = T_ALL // HALO - 1
    wide = lambda c: pl.BlockSpec((TM, D_MODEL), lambda i: (i, c // D_MODEL))
    prev = lambda c: pl.BlockSpec((HALO, D_MODEL), lambda i: (jnp.maximum(i * hb - 1, 0), c // D_MODEL))
    nxt = lambda c: pl.BlockSpec((HALO, D_MODEL), lambda i: (jnp.minimum((i + 1) * hb, last_hb), c // D_MODEL))
    return pl.pallas_call(
        _conv_kernel,
        grid=(T_ALL // TM,),
        in_specs=[wide(COL_CB), wide(COL_CC), wide(COL_CX),
                  prev(COL_CC), prev(COL_CX), nxt(COL_CC), nxt(COL_CX),
                  wide(COL_GC),
                  pl.BlockSpec((3, CONV_WIDTH), lambda i: (0, 0)),
                  pl.BlockSpec(memory_space=pl.ANY)],
        out_specs=pl.BlockSpec((TM, D_MODEL), lambda i: (i, 0)),
        out_shape=jax.ShapeDtypeStruct((T_ALL, D_MODEL), BF16),
        scratch_shapes=[pltpu.VMEM((CONV_WIDTH, D_MODEL), BF16),
                        pltpu.VMEM((CAST_ROWS, D_MODEL), F32),
                        pltpu.SemaphoreType.DMA(())],
        compiler_params=_cparams(("arbitrary",)),
        name="conv",
    )(proj, proj, proj, proj, proj, proj, proj, proj, conv_w, w_conv_out)


def _merge_kernel(xp_ref, xs_ref, attn_p_ref, attn_s_ref, z1_ref, ga_ref, mod_ref, g2_ref, wr_ref,
                  wao_hbm, wo_hbm, x1_ref, h2_ref, lg_ref, wao_scr, wo_scr, stage, sem):
    i = pl.program_id(0)

    @pl.when(i == 0)
    def _():
        _load_cast_weight(wao_hbm, wao_scr, stage, sem)
        _load_cast_weight(wo_hbm, wo_scr, stage, sem)

    is_ctx = i < T_P // TM
    attn = jnp.where(is_ctx, attn_p_ref[...], attn_s_ref[...])
    x = jnp.where(is_ctx, xp_ref[...], xs_ref[...])
    ya = jnp.dot(attn, wao_scr[...], preferred_element_type=F32)
    z = z1_ref[...].astype(F32) + jax.nn.sigmoid(ga_ref[...].astype(F32)) * ya
    mix = jnp.dot(z.astype(BF16), wo_scr[...], preferred_element_type=F32)
    x1 = x + mod_ref[0, 2:3, :] * mix
    x1_ref[...] = x1
    h = _rms_mod(x1, g2_ref[...], mod_ref[0, 3:4, :], mod_ref[0, 4:5, :])
    h2_ref[...] = h
    h_hi = h.astype(BF16)
    h_lo = (h - h_hi.astype(F32)).astype(BF16)
    wr = wr_ref[...]
    wr_hi = wr.astype(BF16)
    wr_lo = (wr - wr_hi.astype(F32)).astype(BF16)
    lg_ref[...] = (jnp.dot(h_hi, wr_hi, preferred_element_type=F32)
                   + jnp.dot(h_hi, wr_lo, preferred_element_type=F32)
                   + jnp.dot(h_lo, wr_hi, preferred_element_type=F32))


def _merge(xp, xs, attn_p, attn_s, z1, proj, mods3, g_ffn, wr_pad, w_attn_out, w_out):
    tiles_per_group = MOD_GROUP // TM
    const = lambda shape: pl.BlockSpec(shape, lambda i: (0, 0))
    row = lambda w: pl.BlockSpec((TM, w), lambda i: (i, 0))
    xsp, xss = _two_group_specs(TM, D_MODEL)
    asp, ass = _two_group_specs(TM, ATTN_WIDTH)
    return pl.pallas_call(
        _merge_kernel,
        grid=(T_ALL // TM,),
        in_specs=[xsp, xss, asp, ass, row(D_MODEL),
                  pl.BlockSpec((TM, D_MODEL), lambda i: (i, COL_GA // D_MODEL)),
                  pl.BlockSpec((1, 6, D_MODEL), lambda i: (i // tiles_per_group, 0, 0)),
                  const((1, D_MODEL)), const((D_MODEL, 128)),
                  pl.BlockSpec(memory_space=pl.ANY), pl.BlockSpec(memory_space=pl.ANY)],
        out_specs=[row(D_MODEL), row(D_MODEL), row(128)],
        out_shape=[jax.ShapeDtypeStruct((T_ALL, D_MODEL), F32),
                   jax.ShapeDtypeStruct((T_ALL, D_MODEL), F32),
                   jax.ShapeDtypeStruct((T_ALL, 128), F32)],
        scratch_shapes=[pltpu.VMEM((ATTN_WIDTH, D_MODEL), BF16),
                        pltpu.VMEM((D_MODEL, D_MODEL), BF16),
                        pltpu.VMEM((CAST_ROWS, D_MODEL), F32),
                        pltpu.SemaphoreType.DMA(())],
        compiler_params=_cparams(("arbitrary",)),
        name="merge",
    )(xp, xs, attn_p, attn_s, z1, proj, mods3, g_ffn, wr_pad, w_attn_out, w_out)


def _route1_kernel(lg_ref, bias_ref, enc_ref, cnt_ref):
    i = pl.program_id(0)
    lt = lg_ref[...].T[:N_EXPERTS, :]
    scores = jax.nn.sigmoid(lt)
    biased = scores + bias_ref[...]
    b3 = biased.reshape(N_EXPERT_GROUPS, GROUP_SIZE, TM)
    mi = lax.broadcasted_iota(I32, b3.shape, 1)
    m1 = jnp.max(b3, axis=1, keepdims=True)
    idx1 = jnp.min(jnp.where(b3 == m1, mi, GROUP_SIZE), axis=1, keepdims=True)
    m2 = jnp.max(jnp.where(mi == idx1, -jnp.inf, b3), axis=1, keepdims=True)
    gs = (m1 + m2).reshape(N_EXPERT_GROUPS, TM)
    gidx = lax.broadcasted_iota(I32, gs.shape, 0)
    grank = jnp.zeros(gs.shape, I32)
    for j in range(N_EXPERT_GROUPS):
        gj = gs[j:j + 1, :]
        beats = jnp.logical_or(gj > gs, jnp.logical_and(gj == gs, j < gidx))
        grank = grank + beats.astype(I32)
    gsel = grank < TOPK_GROUPS
    emask = jnp.broadcast_to(gsel[:, None, :], b3.shape).reshape(N_EXPERTS, TM)
    masked = jnp.where(emask, biased, NEG_INF)
    eidx = lax.broadcasted_iota(I32, masked.shape, 0)
    erank = jnp.zeros(masked.shape, I32)
    for j in range(N_EXPERTS):
        vj = masked[j:j + 1, :]
        beats = jnp.logical_or(vj > masked, jnp.logical_and(vj == masked, j < eidx))
        erank = erank + beats.astype(I32)
    sel = erank < TOP_K
    wsel = jnp.where(sel, scores, 0.0)
    den = jnp.sum(wsel, axis=0, keepdims=True)
    wts = wsel / den * ROUTED_SCALE
    enc_ref[...] = jnp.where(sel, wts, -1.0)

    @pl.when(i == 0)
    def _():
        cnt_ref[...] = jnp.zeros_like(cnt_ref)

    cnt = jnp.sum(sel.astype(F32), axis=1, keepdims=True)
    cnt_ref[...] += jnp.broadcast_to(cnt, cnt_ref.shape)


def _route1(logits, bias_col):
    return pl.pallas_call(
        _route1_kernel,
        grid=(T_ALL // TM,),
        in_specs=[pl.BlockSpec((TM, 128), lambda i: (i, 0)),
                  pl.BlockSpec((N_EXPERTS, 1), lambda i: (0, 0))],
        out_specs=[pl.BlockSpec((N_EXPERTS, TM), lambda i: (0, i)),
                   pl.BlockSpec((N_EXPERTS, 128), lambda i: (0, 0))],
        out_shape=[jax.ShapeDtypeStruct((N_EXPERTS, T_ALL), F32),
                   jax.ShapeDtypeStruct((N_EXPERTS, 128), F32)],
        compiler_params=_cparams(("arbitrary",)),
        name="route1",
    )(logits, bias_col)


def _route2_kernel(enc_ref, cnt_ref, meta_ref, tmap_ref, carry_ref):
    i = pl.program_id(0)

    @pl.when(i == 0)
    def _():
        carry_ref[...] = jnp.zeros_like(carry_ref)

    enc = enc_ref[...]
    sel = enc >= 0.0
    wts = jnp.maximum(enc, 0.0)
    sel_b = sel.astype(BF16)
    ntile = jnp.ceil(cnt_ref[...] * (1.0 / TILE_E))
    er = lax.broadcasted_iota(I32, (N_EXPERTS, N_EXPERTS), 0)
    ec = lax.broadcasted_iota(I32, (N_EXPERTS, N_EXPERTS), 1)
    lower = (ec < er).astype(BF16)
    off_t = jnp.dot(lower, ntile.astype(BF16), preferred_element_type=F32)
    tr = lax.broadcasted_iota(I32, (TM, TM), 0)
    tc = lax.broadcasted_iota(I32, (TM, TM), 1)
    upper = (tr < tc).astype(BF16)
    rank = jnp.dot(sel_b, upper, preferred_element_type=F32) + carry_ref[:, 0:1]
    carry_ref[...] += jnp.broadcast_to(jnp.sum(sel.astype(F32), axis=1, keepdims=True), carry_ref.shape)
    pos = off_t[:, 0:1] * float(TILE_E) + rank
    slot = jnp.dot(lower, sel_b, preferred_element_type=F32)
    rows = []
    for k in range(TOP_K):
        mk = jnp.logical_and(sel, slot == float(k))
        rows.append(jnp.sum(jnp.where(mk, wts, 0.0), axis=0, keepdims=True))
    for k in range(TOP_K):
        mk = jnp.logical_and(sel, slot == float(k))
        rows.append(jnp.sum(jnp.where(mk, pos, 0.0), axis=0, keepdims=True))
    rows.append(jnp.zeros((128 - 2 * TOP_K, TM), F32))
    meta_ref[...] = jnp.concatenate(rows, axis=0).T
    nt = ntile[:, 0:1]
    end_t = off_t[:, 0:1] + nt
    lane = lax.broadcasted_iota(I32, (N_EXPERTS, TMAP_W), 1)
    te = jnp.sum((end_t <= lane.astype(F32)).astype(F32), axis=0, keepdims=True)
    te = jnp.minimum(te, float(N_EXPERTS - 1))
    nact = jnp.sum(nt, axis=0, keepdims=True)
    last_col = jnp.where(nt > 0.0, end_t - 1.0, -1.0)
    erow = lax.broadcasted_iota(I32, (N_EXPERTS, TMAP_W), 0)
    last = jnp.sum(jnp.where(erow == lane, last_col, 0.0), axis=0, keepdims=True)
    tmap_ref[...] = jnp.zeros_like(tmap_ref)
    tmap_ref[0:1, :] = te.astype(I32)
    tmap_ref[1:2, :] = jnp.broadcast_to(nact, (1, TMAP_W)).astype(I32)
    tmap_ref[2:3, :] = last.astype(I32)


def _route2(enc, cnt):
    return pl.pallas_call(
        _route2_kernel,
        grid=(T_ALL // TM,),
        in_specs=[pl.BlockSpec((N_EXPERTS, TM), lambda i: (0, i)),
                  pl.BlockSpec((N_EXPERTS, 128), lambda i: (0, 0))],
        out_specs=[pl.BlockSpec((TM, 128), lambda i: (i, 0)),
                   pl.BlockSpec((8, TMAP_W), lambda i: (0, 0))],
        out_shape=[jax.ShapeDtypeStruct((T_ALL, 128), F32),
                   jax.ShapeDtypeStruct((8, TMAP_W), I32)],
        scratch_shapes=[pltpu.VMEM((N_EXPERTS, 128), F32)],
        compiler_params=_cparams(("arbitrary",)),
        name="route2",
    )(enc, cnt)


def _dispatch_kernel(last_ref, na_ref, pos_ref, h_ref, xs_ref, zbuf, sem):
    i = pl.program_id(0)

    @pl.when(i == 0)
    def _():
        zbuf[...] = jnp.zeros_like(zbuf)

        def tail_copy(t):
            return pltpu.make_async_copy(zbuf, xs_ref.at[pl.ds(t * TILE_E, TILE_E)], sem)

        @pl.loop(na_ref[0], N_TILES_E)
        def _(t):
            tail_copy(t).start()

        @pl.loop(na_ref[0], N_TILES_E)
        def _(t):
            tail_copy(t).wait()

        for e in range(N_EXPERTS):
            @pl.when(last_ref[e] >= 0)
            def _():
                pltpu.make_async_copy(zbuf, xs_ref.at[pl.ds(last_ref[e] * TILE_E, TILE_E)], sem).start()
        for e in range(N_EXPERTS):
            @pl.when(last_ref[e] >= 0)
            def _():
                pltpu.make_async_copy(zbuf, xs_ref.at[pl.ds(last_ref[e] * TILE_E, TILE_E)], sem).wait()

    def body(t, carry):
        for k in range(TOP_K):
            r = pos_ref[t * TOP_K + k]
            pltpu.make_async_copy(h_ref.at[pl.ds(t, 1)], xs_ref.at[pl.ds(r, 1)], sem).start()
        return carry

    lax.fori_loop(0, TM, body, 0)
    for k in range(TOP_K):
        pltpu.make_async_copy(h_ref, xs_ref.at[pl.ds(0, TM)], sem).wait()


def _dispatch(last_tile, n_active, pos_flat, h2):
    grid_spec = pltpu.PrefetchScalarGridSpec(
        num_scalar_prefetch=2,
        grid=(T_ALL // TM,),
        in_specs=[pl.BlockSpec((TM * TOP_K,), lambda i, lt, na: (i,), memory_space=pltpu.SMEM),
                  pl.BlockSpec((TM, D_MODEL), lambda i, lt, na: (i, 0))],
        out_specs=pl.BlockSpec(memory_space=pl.ANY),
        scratch_shapes=[pltpu.VMEM((TILE_E, D_MODEL), F32), pltpu.SemaphoreType.DMA(())],
    )
    return pl.pallas_call(
        _dispatch_kernel,
        grid_spec=grid_spec,
        out_shape=jax.ShapeDtypeStruct((N_SORTED, D_MODEL), F32),
        compiler_params=_cparams(("arbitrary",)),
        name="dispatch",
    )(last_tile, n_active, pos_flat, h2)


def _expert_kernel(te_ref, na_ref, x_ref, wg_ref, wu_ref, wd_ref, o_ref, wg_scr, wu_scr, wd_scr):
    i = pl.program_id(0)
    active = i < na_ref[0]
    new_expert = jnp.logical_or(i == 0, te_ref[i] != te_ref[jnp.maximum(i - 1, 0)])

    @pl.when(jnp.logical_and(active, new_expert))
    def _():
        for c in range(D_MODEL // CAST_ROWS):
            rows = pl.ds(c * CAST_ROWS, CAST_ROWS)
            wg_scr[rows, :] = wg_ref[0, rows, :].astype(BF16)
            wu_scr[rows, :] = wu_ref[0, rows, :].astype(BF16)
        wd_scr[...] = wd_ref[0].astype(BF16)

    @pl.when(active)
    def _():
        x = x_ref[...].astype(BF16)
        g = jnp.dot(x, wg_scr[...], preferred_element_type=F32)
        u = jnp.dot(x, wu_scr[...], preferred_element_type=F32)
        act = (_silu(g) * u).astype(BF16)
        o_ref[...] = jnp.dot(act, wd_scr[...], preferred_element_type=F32)

    @pl.when(jnp.logical_not(active))
    def _():
        o_ref[...] = jnp.zeros_like(o_ref)


def _expert(tile_expert, n_active, xs, wg, wu, wd):
    def row_map(i, te, na):
        return (jnp.minimum(i, na[0] - 1), 0)

    def out_map(i, te, na):
        return (i, 0)

    def w_map(i, te, na):
        return (te[i], 0, 0)

    grid_spec = pltpu.PrefetchScalarGridSpec(
        num_scalar_prefetch=2,
        grid=(N_TILES_E,),
        in_specs=[pl.BlockSpec((TILE_E, D_MODEL), row_map),
                  pl.BlockSpec((1, D_MODEL, D_EXPERT), w_map),
                  pl.BlockSpec((1, D_MODEL, D_EXPERT), w_map),
                  pl.BlockSpec((1, D_EXPERT, D_MODEL), w_map)],
        out_specs=pl.BlockSpec((TILE_E, D_MODEL), out_map),
        scratch_shapes=[pltpu.VMEM((D_MODEL, D_EXPERT), BF16),
                        pltpu.VMEM((D_MODEL, D_EXPERT), BF16),
                        pltpu.VMEM((D_EXPERT, D_MODEL), BF16)],
    )
    return pl.pallas_call(
        _expert_kernel,
        grid_spec=grid_spec,
        out_shape=jax.ShapeDtypeStruct((N_SORTED, D_MODEL), F32),
        compiler_params=_cparams(("arbitrary",)),
        name="expert",
    )(tile_expert, n_active, xs, wg, wu, wd)


def _combine_kernel(pos_ref, meta_ref, ys_ref, h2_ref, x1_ref, mod_ref, wsg_hbm, wsu_hbm, wsd_hbm,
                    op_ref, os_ref, buf, wsg_scr, wsu_scr, wsd_scr, stage_a, stage_b, sem, wsem):
    i = pl.program_id(0)

    @pl.when(i == 0)
    def _():
        _load_cast_weight(wsg_hbm, wsg_scr, stage_a, wsem)
        _load_cast_weight(wsu_hbm, wsu_scr, stage_a, wsem)
        _load_cast_weight(wsd_hbm, wsd_scr, stage_b, wsem)

    def body(t, carry):
        for k in range(TOP_K):
            r = pos_ref[t * TOP_K + k]
            pltpu.make_async_copy(ys_ref.at[pl.ds(r, 1)], buf.at[k, pl.ds(t, 1)], sem).start()
        return carry

    lax.fori_loop(0, TM_C, body, 0)
    h = h2_ref[...].astype(BF16)
    sg = jnp.dot(h, wsg_scr[...], preferred_element_type=F32)
    su = jnp.dot(h, wsu_scr[...], preferred_element_type=F32)
    moe = jnp.dot((_silu(sg) * su).astype(BF16), wsd_scr[...], preferred_element_type=F32)
    for k in range(TOP_K):
        pltpu.make_async_copy(ys_ref.at[pl.ds(0, TM_C)], buf.at[k], sem).wait()
    for k in range(TOP_K):
        moe = moe + meta_ref[:, k:k + 1] * buf[k]
    y = x1_ref[...] + mod_ref[0, 5:6, :] * moe

    @pl.when(i < T_P // TM_C)
    def _():
        op_ref[...] = y

    @pl.when(i >= T_P // TM_C)
    def _():
        os_ref[...] = y


def _combine(pos_flat, meta, ys, h2, x1, mods3, wsg, wsu, wsd):
    tiles_per_group = MOD_GROUP // TM_C
    row = lambda w: pl.BlockSpec((TM_C, w), lambda i: (i, 0))
    osp, oss = _two_group_specs(TM_C, D_MODEL)
    hbm = pl.BlockSpec(memory_space=pl.ANY)
    return pl.pallas_call(
        _combine_kernel,
        grid=(T_ALL // TM_C,),
        in_specs=[pl.BlockSpec((TM_C * TOP_K,), lambda i: (i,), memory_space=pltpu.SMEM),
                  row(128), hbm, row(D_MODEL), row(D_MODEL),
                  pl.BlockSpec((1, 6, D_MODEL), lambda i: (i // tiles_per_group, 0, 0)),
                  hbm, hbm, hbm],
        out_specs=[osp, oss],
        out_shape=[jax.ShapeDtypeStruct((T_P, D_MODEL), F32),
                   jax.ShapeDtypeStruct((T_S, D_MODEL), F32)],
        scratch_shapes=[pltpu.VMEM((TOP_K, TM_C, D_MODEL), F32),
                        pltpu.VMEM((D_MODEL, D_SHARED), BF16),
                        pltpu.VMEM((D_MODEL, D_SHARED), BF16),
                        pltpu.VMEM((D_SHARED, D_MODEL), BF16),
                        pltpu.VMEM((D_MODEL, D_SHARED), F32),
                        pltpu.VMEM((D_SHARED, D_MODEL), F32),
                        pltpu.SemaphoreType.DMA(()),
                        pltpu.SemaphoreType.DMA(())],
        compiler_params=_cparams(("arbitrary",)),
        name="combine",
    )(pos_flat, meta, ys, h2, x1, mods3, wsg, wsu, wsd)


def _rope_tables():
    rows = DEC_SEQ // GRID_W
    row = jnp.repeat(jnp.arange(rows, dtype=F32), GRID_W)
    col = jnp.tile(jnp.arange(GRID_W, dtype=F32), rows)
    n_freq = HEAD_DIM // 4
    inv = ROPE_THETA ** (-jnp.arange(n_freq, dtype=F32) / n_freq)
    ang = jnp.concatenate([row[:, None] * inv, col[:, None] * inv], axis=-1)
    cos, sin = jnp.cos(ang), jnp.sin(ang)
    cos2 = jnp.concatenate([cos, cos], axis=-1)
    sin2 = jnp.concatenate([-sin, sin], axis=-1)
    cos_tab = jnp.concatenate([jnp.ones((TM_IN, HEAD_DIM), F32), cos2], axis=0)
    sin_tab = jnp.concatenate([jnp.zeros((TM_IN, HEAD_DIM), F32), sin2], axis=0)
    return cos_tab, sin_tab


def kernel(x_prompt, x_sample, cache_k, cache_v, c, c_ctx, w_ada, b_ada, norm_mix_g, norm_ffn_g, w_in, conv_w,
           q_norm_g, k_norm_g, attn_sink, w_conv_out, w_attn_out, w_out, router_w, router_bias, w_exp_gate,
           w_exp_up, w_exp_down, w_sh_gate, w_sh_up, w_sh_down):
    l = 0
    xp = x_prompt.reshape(T_P, D_MODEL)
    xs = x_sample.reshape(T_S, D_MODEL)

    cond = jnp.concatenate([c_ctx[None, :], c, jnp.zeros((8 - N_MOD, D_MODEL), F32)], axis=0)
    mods = _ada(cond.T, w_ada[l], b_ada[l][None, :])
    mods3 = mods[:N_MOD].reshape(N_MOD, 6, D_MODEL)

    h = _prenorm(xp, xs, mods3, norm_mix_g[l][None, :])
    cos_tab, sin_tab = _rope_tables()
    proj, k32, v32 = _inproj(h, w_in[l], q_norm_g[l][None, :], k_norm_g[l][None, :], cos_tab, sin_tab)

    sink = attn_sink[l]
    kc = cache_k[:, l].reshape(DEC_BATCH, PAST_LEN, KV_WIDTH)
    vc = cache_v[:, l].reshape(DEC_BATCH, PAST_LEN, KV_WIDTH)
    attn_p = _attn_p(sink, proj)
    attn_s = _attn_s(sink, proj, kc, vc)

    z1 = _conv(proj, conv_w[l], w_conv_out[l])

    wr_pad = jnp.pad(router_w[l], ((0, 0), (0, 128 - N_EXPERTS)))
    x1, h2, logits = _merge(xp, xs, attn_p, attn_s, z1, proj, mods3, norm_ffn_g[l][None, :], wr_pad,
                            w_attn_out[l], w_out[l])

    enc, cnt = _route1(logits, router_bias[l][:, None])
    meta, tmap = _route2(enc, cnt)
    pos_flat = meta[:, TOP_K:2 * TOP_K].astype(I32).reshape(T_ALL * TOP_K)
    tile_expert = tmap[0, :N_TILES_E]
    n_active = tmap[1, :1]
    last_tile = tmap[2, :N_EXPERTS]

    xsort = _dispatch(last_tile, n_active, pos_flat, h2)
    ysort = _expert(tile_expert, n_active, xsort, w_exp_gate[l], w_exp_up[l], w_exp_down[l])
    y_p, y_s = _combine(pos_flat, meta, ysort, h2, x1, mods3, w_sh_gate[l], w_sh_up[l], w_sh_down[l])

    y_prompt = y_p.reshape(BATCH, SEQ, D_MODEL)
    y_sample = y_s.reshape(DEC_BATCH, DEC_SEQ, D_MODEL)
    new_k = k32.reshape(BATCH, 1, SEQ, N_KV_HEADS, HEAD_DIM)
    new_v = v32.reshape(BATCH, 1, SEQ, N_KV_HEADS, HEAD_DIM)
    return (y_prompt, y_sample, new_k, new_v)
```

```python
import jax
import jax.numpy as jnp
from jax import lax
from jax.experimental import pallas as pl
from jax.experimental.pallas import tpu as pltpu

F32 = jnp.float32
BF16 = jnp.bfloat16
I32 = jnp.int32

D_MODEL = 2048
BATCH = 16
SEQ = 256
DEC_BATCH = 2
DEC_SEQ = 4096
PAST_LEN = 512
GRID_W = 64
N_HEADS = 16
N_KV_HEADS = 4
HEAD_DIM = 128
GROUP = N_HEADS // N_KV_HEADS
ATTN_WIDTH = N_HEADS * HEAD_DIM
KV_WIDTH = N_KV_HEADS * HEAD_DIM
WINDOW = 128
CONV_WIDTH = D_MODEL
N_EXPERTS = 64
TOP_K = 8
N_EXPERT_GROUPS = 8
GROUP_SIZE = N_EXPERTS // N_EXPERT_GROUPS
TOPK_GROUPS = 4
D_EXPERT = 512
D_SHARED = 512
ROUTED_SCALE = 2.5
ROPE_THETA = 10000.0
EPS = 1e-6
NEG_INF = -1e30
ATTN_SCALE = HEAD_DIM ** -0.5

T_P = BATCH * SEQ
T_S = DEC_BATCH * DEC_SEQ
T_ALL = T_P + T_S
MOD_GROUP = 4096
N_MOD = 1 + DEC_BATCH
assert T_P == MOD_GROUP and DEC_SEQ == MOD_GROUP

COL_CB, COL_CC, COL_CX, COL_Q, COL_GC, COL_GA = (i * D_MODEL for i in range(6))
COL_K = 6 * D_MODEL
COL_V = COL_K + KV_WIDTH
IN_WIDTH = COL_V + KV_WIDTH
W_COL_KV = 3 * CONV_WIDTH + ATTN_WIDTH

TM_IN = 1024
TN_IN = 1024
TM = 256
TM_C = 128
TILE_E = 256
N_SORTED = T_ALL * TOP_K + N_EXPERTS * TILE_E
N_TILES_E = N_SORTED // TILE_E
TMAP_W = 512
CAST_ROWS = 512
VMEM_LIMIT = 56 * 1024 * 1024


def _cparams(sem):
    return pltpu.CompilerParams(dimension_semantics=sem, vmem_limit_bytes=VMEM_LIMIT)


def _silu(x):
    return x * jax.nn.sigmoid(x)


def _load_cast_weight(w_hbm, w_scr, stage, sem):
    rows = stage.shape[0]
    for c in range(w_hbm.shape[0] // rows):
        cp = pltpu.make_async_copy(w_hbm.at[pl.ds(c * rows, rows)], stage, sem)
        cp.start()
        cp.wait()
        w_scr[pl.ds(c * rows, rows), :] = stage[...].astype(BF16)


ADA_TN = 1024
ADA_CHUNK = 256


def _ada_kernel(ct_ref, w_ref, b_ref, o_ref):
    tn = w_ref.shape[1]

    def body(c, accs):
        k0 = pl.multiple_of(c * ADA_CHUNK, ADA_CHUNK)
        wch = w_ref[pl.ds(k0, ADA_CHUNK), :]
        sch = _silu(ct_ref[pl.ds(k0, ADA_CHUNK), :])
        out = []
        for r in range(N_MOD):
            p = wch * sch[:, r:r + 1]
            out.append(accs[r] + p.reshape(ADA_CHUNK // 8, 8, tn).sum(axis=0))
        return tuple(out)

    accs = lax.fori_loop(0, D_MODEL // ADA_CHUNK, body,
                         tuple(jnp.zeros((8, tn), F32) for _ in range(N_MOD)))
    o_ref[...] = jnp.zeros_like(o_ref)
    for r in range(N_MOD):
        o_ref[r:r + 1, :] = jnp.sum(accs[r], axis=0, keepdims=True) + b_ref[...]


def _ada(cond_t, w_ada, b_ada):
    n = w_ada.shape[1]
    return pl.pallas_call(
        _ada_kernel,
        grid=(n // ADA_TN,),
        in_specs=[pl.BlockSpec((D_MODEL, 8), lambda j: (0, 0)),
                  pl.BlockSpec((D_MODEL, ADA_TN), lambda j: (0, j)),
                  pl.BlockSpec((1, ADA_TN), lambda j: (0, j))],
        out_specs=pl.BlockSpec((8, ADA_TN), lambda j: (0, j)),
        out_shape=jax.ShapeDtypeStruct((8, n), F32),
        compiler_params=_cparams(("arbitrary",)),
        name="ada",
    )(cond_t, w_ada, b_ada)


def _rms_mod(x, g, shift, scale):
    ms = jnp.mean(x * x, axis=-1, keepdims=True)
    y = x * lax.rsqrt(ms + EPS) * g
    return y * (1.0 + scale) + shift


def _two_group_specs(tile, width):
    n_ctx = T_P // tile
    return (pl.BlockSpec((tile, width), lambda i: (jnp.minimum(i, n_ctx - 1), 0)),
            pl.BlockSpec((tile, width), lambda i: (jnp.maximum(i - n_ctx, 0), 0)))


def _prenorm_kernel(xp_ref, xs_ref, mod_ref, g_ref, h_ref):
    i = pl.program_id(0)
    shift = mod_ref[0, 0:1, :]
    scale = mod_ref[0, 1:2, :]
    g = g_ref[...]

    @pl.when(i < T_P // TM)
    def _():
        h_ref[...] = _rms_mod(xp_ref[...], g, shift, scale).astype(BF16)

    @pl.when(i >= T_P // TM)
    def _():
        h_ref[...] = _rms_mod(xs_ref[...], g, shift, scale).astype(BF16)


def _prenorm(xp, xs, mods3, g_mix):
    tiles_per_group = MOD_GROUP // TM
    sp, ss = _two_group_specs(TM, D_MODEL)
    return pl.pallas_call(
        _prenorm_kernel,
        grid=(T_ALL // TM,),
        in_specs=[sp, ss,
                  pl.BlockSpec((1, 6, D_MODEL), lambda i: (i // tiles_per_group, 0, 0)),
                  pl.BlockSpec((1, D_MODEL), lambda i: (0, 0))],
        out_specs=pl.BlockSpec((TM, D_MODEL), lambda i: (i, 0)),
        out_shape=jax.ShapeDtypeStruct((T_ALL, D_MODEL), BF16),
        compiler_params=_cparams(("arbitrary",)),
        name="prenorm",
    )(xp, xs, mods3, g_mix)


def _head_norm_rope(a, g, cos2, sin2, scale):
    ms = jnp.mean(a * a, axis=-1, keepdims=True)
    a = a * lax.rsqrt(ms + EPS) * g
    a = a * cos2 + pltpu.roll(a, HEAD_DIM // 2, 1) * sin2
    return a * scale


N_J = IN_WIDTH // TN_IN
J_Q0 = COL_Q // TN_IN
J_Q1 = J_Q0 + ATTN_WIDTH // TN_IN
J_KV = COL_K // TN_IN
assert J_KV == N_J - 1
N_CTX_IN = T_P // TM_IN


def _inproj_kernel(h_ref, w_ref, qg_ref, kg_ref, cos_ref, sin_ref, o_ref, k_ref, v_ref, w_scr):
    j = pl.program_id(0)
    i = pl.program_id(1)

    @pl.when(i == 0)
    def _():
        for c in range(D_MODEL // CAST_ROWS):
            rows = pl.ds(c * CAST_ROWS, CAST_ROWS)
            w_scr[rows, :] = w_ref[rows, :].astype(BF16)

    acc = jnp.dot(h_ref[...], w_scr[...], preferred_element_type=F32)
    is_q = jnp.logical_and(j >= J_Q0, j < J_Q1)
    is_kv = j == J_KV

    @pl.when(jnp.logical_not(jnp.logical_or(is_q, is_kv)))
    def _():
        o_ref[...] = acc.astype(BF16)

    @pl.when(is_q)
    def _():
        cos2 = cos_ref[...]
        sin2 = sin_ref[...]
        g = qg_ref[...]
        for h in range(TN_IN // HEAD_DIM):
            cols = slice(h * HEAD_DIM, (h + 1) * HEAD_DIM)
            o_ref[:, cols] = _head_norm_rope(acc[:, cols], g, cos2, sin2, ATTN_SCALE).astype(BF16)

    @pl.when(is_kv)
    def _():
        cos2 = cos_ref[...]
        sin2 = sin_ref[...]
        g = kg_ref[...]
        is_ctx = i < N_CTX_IN
        for h in range(N_KV_HEADS):
            cols = slice(h * HEAD_DIM, (h + 1) * HEAD_DIM)
            kh = _head_norm_rope(acc[:, cols], g, cos2, sin2, 1.0)
            o_ref[:, cols] = kh.astype(BF16)

            @pl.when(is_ctx)
            def _():
                k_ref[:, cols] = kh

        v = acc[:, KV_WIDTH:]
        o_ref[:, KV_WIDTH:] = v.astype(BF16)

        @pl.when(is_ctx)
        def _():
            v_ref[...] = v


def _inproj(h, w_in, qg, kg, cos_tab, sin_tab):
    pos_tiles = DEC_SEQ // TM_IN

    def tab_map(j, i):
        return (jnp.where(i < N_CTX_IN, 0, 1 + i % pos_tiles), 0)

    def w_map(j, i):
        return (0, jnp.where(j < W_COL_KV // TN_IN, j, jnp.where(j == J_KV, W_COL_KV // TN_IN, j + 1)))

    def kv_map(j, i):
        return (jnp.where(j == J_KV, jnp.minimum(i, N_CTX_IN - 1), 0), 0)

    return pl.pallas_call(
        _inproj_kernel,
        grid=(N_J, T_ALL // TM_IN),
        in_specs=[pl.BlockSpec((TM_IN, D_MODEL), lambda j, i: (i, 0)),
                  pl.BlockSpec((D_MODEL, TN_IN), w_map, pipeline_mode=pl.Buffered(1)),
                  pl.BlockSpec((1, HEAD_DIM), lambda j, i: (0, 0)),
                  pl.BlockSpec((1, HEAD_DIM), lambda j, i: (0, 0)),
                  pl.BlockSpec((TM_IN, HEAD_DIM), tab_map),
                  pl.BlockSpec((TM_IN, HEAD_DIM), tab_map)],
        out_specs=[pl.BlockSpec((TM_IN, TN_IN), lambda j, i: (i, j)),
                   pl.BlockSpec((TM_IN, KV_WIDTH), kv_map),
                   pl.BlockSpec((TM_IN, KV_WIDTH), kv_map)],
        out_shape=[jax.ShapeDtypeStruct((T_ALL, IN_WIDTH), BF16),
                   jax.ShapeDtypeStruct((T_P, KV_WIDTH), F32),
                   jax.ShapeDtypeStruct((T_P, KV_WIDTH), F32)],
        scratch_shapes=[pltpu.VMEM((D_MODEL, TN_IN), BF16)],
        compiler_params=_cparams(("arbitrary", "arbitrary")),
        name="inproj",
    )(h, w_in, qg, kg, cos_tab, sin_tab)


def _dot_nt(a, b):
    return lax.dot_general(a, b, (((1,), (1,)), ((), ())), preferred_element_type=F32)


def _attn_p_kernel(sink_ref, q_ref, k_ref, v_ref, o_ref):
    for kh in range(N_KV_HEADS):
        kcols = slice(kh * HEAD_DIM, (kh + 1) * HEAD_DIM)
        k = k_ref[:, kcols]
        v = v_ref[:, kcols]
        for gi in range(GROUP):
            h = kh * GROUP + gi
            cols = slice(h * HEAD_DIM, (h + 1) * HEAD_DIM)
            s = _dot_nt(q_ref[:, cols], k)
            sk = sink_ref[h]
            m = jnp.maximum(jnp.max(s, axis=-1, keepdims=True), sk)
            p = jnp.exp(s - m)
            den = jnp.sum(p, axis=-1, keepdims=True) + jnp.exp(sk - m)
            o = jnp.dot(p.astype(BF16), v, preferred_element_type=F32) / den
            o_ref[:, cols] = o.astype(BF16)


def _attn_p(sink, proj):
    return pl.pallas_call(
        _attn_p_kernel,
        grid=(BATCH,),
        in_specs=[pl.BlockSpec(memory_space=pltpu.SMEM),
                  pl.BlockSpec((SEQ, ATTN_WIDTH), lambda b: (b, COL_Q // ATTN_WIDTH)),
                  pl.BlockSpec((SEQ, KV_WIDTH), lambda b: (b, COL_K // KV_WIDTH)),
                  pl.BlockSpec((SEQ, KV_WIDTH), lambda b: (b, COL_V // KV_WIDTH))],
        out_specs=pl.BlockSpec((SEQ, ATTN_WIDTH), lambda b: (b, 0)),
        out_shape=jax.ShapeDtypeStruct((T_P, ATTN_WIDTH), BF16),
        compiler_params=_cparams(("arbitrary",)),
        name="attn_p",
    )(sink, proj, proj, proj)


QB = 256
BAND = QB + 2 * WINDOW


def _attn_s_kernel(sink_ref, q_ref, k_ref, v_ref, kc_ref, vc_ref, o_ref):
    i = pl.program_id(1)
    start = jnp.clip(i * QB - WINDOW, 0, DEC_SEQ - BAND)
    start = pl.multiple_of(start, WINDOW)
    qpos = i * QB + lax.broadcasted_iota(I32, (QB, BAND), 0)
    kpos = start + lax.broadcasted_iota(I32, (QB, BAND), 1)
    valid = jnp.abs(qpos - kpos) <= WINDOW
    for kh in range(N_KV_HEADS):
        kcols = slice(kh * HEAD_DIM, (kh + 1) * HEAD_DIM)
        kb = k_ref[pl.ds(start, BAND), kcols]
        vb = v_ref[pl.ds(start, BAND), kcols]
        kc = kc_ref[0, :, kcols].astype(BF16)
        vc = vc_ref[0, :, kcols].astype(BF16)
        for gi in range(GROUP):
            h = kh * GROUP + gi
            cols = slice(h * HEAD_DIM, (h + 1) * HEAD_DIM)
            q = q_ref[:, cols]
            s_loc = jnp.where(valid, _dot_nt(q, kb), NEG_INF)
            s_ctx = _dot_nt(q, kc)
            sk = sink_ref[h]
            m = jnp.maximum(jnp.maximum(jnp.max(s_loc, axis=-1, keepdims=True),
                                        jnp.max(s_ctx, axis=-1, keepdims=True)), sk)
            p_loc = jnp.exp(s_loc - m)
            p_ctx = jnp.exp(s_ctx - m)
            den = (jnp.sum(p_loc, axis=-1, keepdims=True) + jnp.sum(p_ctx, axis=-1, keepdims=True)
                   + jnp.exp(sk - m))
            o = (jnp.dot(p_loc.astype(BF16), vb, preferred_element_type=F32)
                 + jnp.dot(p_ctx.astype(BF16), vc, preferred_element_type=F32)) / den
            o_ref[:, cols] = o.astype(BF16)


def _attn_s(sink, proj, kc, vc):
    nq = DEC_SEQ // QB
    first = T_P // QB
    seq_blk = T_P // DEC_SEQ
    return pl.pallas_call(
        _attn_s_kernel,
        grid=(DEC_BATCH, nq),
        in_specs=[pl.BlockSpec(memory_space=pltpu.SMEM),
                  pl.BlockSpec((QB, ATTN_WIDTH), lambda b, i: (first + b * nq + i, COL_Q // ATTN_WIDTH)),
                  pl.BlockSpec((DEC_SEQ, KV_WIDTH), lambda b, i: (seq_blk + b, COL_K // KV_WIDTH)),
                  pl.BlockSpec((DEC_SEQ, KV_WIDTH), lambda b, i: (seq_blk + b, COL_V // KV_WIDTH)),
                  pl.BlockSpec((1, PAST_LEN, KV_WIDTH), lambda b, i: (b, 0, 0)),
                  pl.BlockSpec((1, PAST_LEN, KV_WIDTH), lambda b, i: (b, 0, 0))],
        out_specs=pl.BlockSpec((QB, ATTN_WIDTH), lambda b, i: (b * nq + i, 0)),
        out_shape=jax.ShapeDtypeStruct((T_S, ATTN_WIDTH), BF16),
        compiler_params=_cparams(("arbitrary", "arbitrary")),
        name="attn_s",
    )(sink, proj, proj, proj, kc, vc)


HALO = 16


def _conv_kernel(cb_ref, cc_ref, cx_ref, ccp_ref, cxp_ref, ccn_ref, cxn_ref, gc_ref, cw_ref, w_hbm, o_ref,
                 w_scr, stage, sem):
    i = pl.program_id(0)

    @pl.when(i == 0)
    def _():
        _load_cast_weight(w_hbm, w_scr, stage, sem)

    n_ctx_tiles = T_P // TM
    per_seq = DEC_SEQ // TM
    is_first = jnp.logical_or(i < n_ctx_tiles, (i - n_ctx_tiles) % per_seq == 0)
    is_last = jnp.logical_or(i < n_ctx_tiles, (i - n_ctx_tiles) % per_seq == per_seq - 1)
    p = cc_ref[...].astype(F32) * cx_ref[...].astype(F32)
    prev_row = ccp_ref[HALO - 1:HALO, :].astype(F32) * cxp_ref[HALO - 1:HALO, :].astype(F32)
    next_row = ccn_ref[0:1, :].astype(F32) * cxn_ref[0:1, :].astype(F32)
    prev_row = jnp.where(is_first, 0.0, prev_row)
    next_row = jnp.where(is_last, 0.0, next_row)
    rows = lax.broadcasted_iota(I32, (TM, 1), 0)
    p_prev = jnp.where(rows == 0, prev_row, pltpu.roll(p, 1, 0))
    p_next = jnp.where(rows == TM - 1, next_row, pltpu.roll(p, TM - 1, 0))
    conv = p_prev * cw_ref[0:1, :] + p * cw_ref[1:2, :] + p_next * cw_ref[2:3, :]
    u = cb_ref[...].astype(F32) * conv
    y = jnp.dot(u.astype(BF16), w_scr[...], preferred_element_type=F32)
    o_ref[...] = (jax.nn.sigmoid(gc_ref[...].astype(F32)) * y).astype(BF16)


def _conv(proj, conv_w, w_conv_out):
    hb = TM // HALO
    last_hb = T_ALL // HALO - 1
    wide = lambda c: pl.BlockSpec((TM, D_MODEL), lambda i: (i, c // D_MODEL))
    prev = lambda c: pl.BlockSpec((HALO, D_MODEL), lambda i: (jnp.maximum(i * hb - 1, 0), c // D_MODEL))
    nxt = lambda c: pl.BlockSpec((HALO, D_MODEL), lambda i: (jnp.minimum((i + 1) * hb, last_hb), c // D_MODEL))
    return pl.pallas_call(
        _conv_kernel,
        grid=(T_ALL // TM,),
        in_specs=[wide(COL_CB), wide(COL_CC), wide(COL_CX),
                  prev(COL_CC), prev(COL_CX), nxt(COL_CC), nxt(COL_CX),
                  wide(COL_GC),
                  pl.BlockSpec((3, CONV_WIDTH), lambda i: (0, 0)),
                  pl.BlockSpec(memory_space=pl.ANY)],
        out_specs=pl.BlockSpec((TM, D_MODEL), lambda i: (i, 0)),
        out_shape=jax.ShapeDtypeStruct((T_ALL, D_MODEL), BF16),
        scratch_shapes=[pltpu.VMEM((CONV_WIDTH, D_MODEL), BF16),
                        pltpu.VMEM((CAST_ROWS, D_MODEL), F32),
                        pltpu.SemaphoreType.DMA(())],
        compiler_params=_cparams(("arbitrary",)),
        name="conv",
    )(proj, proj, proj, proj, proj, proj, proj, proj, conv_w, w_conv_out)


def _merge_kernel(xp_ref, xs_ref, attn_p_ref, attn_s_ref, z1_ref, ga_ref, mod_ref, g2_ref, wr_ref,
                  wao_hbm, wo_hbm, x1_ref, h2_ref, lg_ref, wao_scr, wo_scr, stage, sem):
    i = pl.program_id(0)

    @pl.when(i == 0)
    def _():
        _load_cast_weight(wao_hbm, wao_scr, stage, sem)
        _load_cast_weight(wo_hbm, wo_scr, stage, sem)

    is_ctx = i < T_P // TM
    attn = jnp.where(is_ctx, attn_p_ref[...], attn_s_ref[...])
    x = jnp.where(is_ctx, xp_ref[...], xs_ref[...])
    ya = jnp.dot(attn, wao_scr[...], preferred_element_type=F32)
    z = z1_ref[...].astype(F32) + jax.nn.sigmoid(ga_ref[...].astype(F32)) * ya
    mix = jnp.dot(z.astype(BF16), wo_scr[...], preferred_element_type=F32)
    x1 = x + mod_ref[0, 2:3, :] * mix
    x1_ref[...] = x1
    h = _rms_mod(x1, g2_ref[...], mod_ref[0, 3:4, :], mod_ref[0, 4:5, :])
    h2_ref[...] = h
    h_hi = h.astype(BF16)
    h_lo = (h - h_hi.astype(F32)).astype(BF16)
    wr = wr_ref[...]
    wr_hi = wr.astype(BF16)
    wr_lo = (wr - wr_hi.astype(F32)).astype(BF16)
    lg_ref[...] = (jnp.dot(h_hi, wr_hi, preferred_element_type=F32)
                   + jnp.dot(h_hi, wr_lo, preferred_element_type=F32)
                   + jnp.dot(h_lo, wr_hi, preferred_element_type=F32))


def _merge(xp, xs, attn_p, attn_s, z1, proj, mods3, g_ffn, wr_pad, w_attn_out, w_out):
    tiles_per_group = MOD_GROUP // TM
    const = lambda shape: pl.BlockSpec(shape, lambda i: (0, 0))
    row = lambda w: pl.BlockSpec((TM, w), lambda i: (i, 0))
    xsp, xss = _two_group_specs(TM, D_MODEL)
    asp, ass = _two_group_specs(TM, ATTN_WIDTH)
    return pl.pallas_call(
        _merge_kernel,
        grid=(T_ALL // TM,),
        in_specs=[xsp, xss, asp, ass, row(D_MODEL),
                  pl.BlockSpec((TM, D_MODEL), lambda i: (i, COL_GA // D_MODEL)),
                  pl.BlockSpec((1, 6, D_MODEL), lambda i: (i // tiles_per_group, 0, 0)),
                  const((1, D_MODEL)), const((D_MODEL, 128)),
                  pl.BlockSpec(memory_space=pl.ANY), pl.BlockSpec(memory_space=pl.ANY)],
        out_specs=[row(D_MODEL), row(D_MODEL), row(128)],
        out_shape=[jax.ShapeDtypeStruct((T_ALL, D_MODEL), F32),
                   jax.ShapeDtypeStruct((T_ALL, D_MODEL), F32),
                   jax.ShapeDtypeStruct((T_ALL, 128), F32)],
        scratch_shapes=[pltpu.VMEM((ATTN_WIDTH, D_MODEL), BF16),
                        pltpu.VMEM((D_MODEL, D_MODEL), BF16),
                        pltpu.VMEM((CAST_ROWS, D_MODEL), F32),
                        pltpu.SemaphoreType.DMA(())],
        compiler_params=_cparams(("arbitrary",)),
        name="merge",
    )(xp, xs, attn_p, attn_s, z1, proj, mods3, g_ffn, wr_pad, w_attn_out, w_out)


def _route1_kernel(lg_ref, bias_ref, enc_ref, cnt_ref):
    i = pl.program_id(0)
    lt = lg_ref[...].T[:N_EXPERTS, :]
    scores = jax.nn.sigmoid(lt)
    biased = scores + bias_ref[...]
    b3 = biased.reshape(N_EXPERT_GROUPS, GROUP_SIZE, TM)
    mi = lax.broadcasted_iota(I32, b3.shape, 1)
    m1 = jnp.max(b3, axis=1, keepdims=True)
    idx1 = jnp.min(jnp.where(b3 == m1, mi, GROUP_SIZE), axis=1, keepdims=True)
    m2 = jnp.max(jnp.where(mi == idx1, -jnp.inf, b3), axis=1, keepdims=True)
    gs = (m1 + m2).reshape(N_EXPERT_GROUPS, TM)
    gidx = lax.broadcasted_iota(I32, gs.shape, 0)
    grank = jnp.zeros(gs.shape, I32)
    for j in range(N_EXPERT_GROUPS):
        gj = gs[j:j + 1, :]
        beats = jnp.logical_or(gj > gs, jnp.logical_and(gj == gs, j < gidx))
        grank = grank + beats.astype(I32)
    gsel = grank < TOPK_GROUPS
    emask = jnp.broadcast_to(gsel[:, None, :], b3.shape).reshape(N_EXPERTS, TM)
    masked = jnp.where(emask, biased, NEG_INF)
    eidx = lax.broadcasted_iota(I32, masked.shape, 0)
    erank = jnp.zeros(masked.shape, I32)
    for j in range(N_EXPERTS):
        vj = masked[j:j + 1, :]
        beats = jnp.logical_or(vj > masked, jnp.logical_and(vj == masked, j < eidx))
        erank = erank + beats.astype(I32)
    sel = erank < TOP_K
    wsel = jnp.where(sel, scores, 0.0)
    den = jnp.sum(wsel, axis=0, keepdims=True)
    wts = wsel / den * ROUTED_SCALE
    enc_ref[...] = jnp.where(sel, wts, -1.0)

    @pl.when(i == 0)
    def _():
        cnt_ref[...] = jnp.zeros_like(cnt_ref)

    cnt = jnp.sum(sel.astype(F32), axis=1, keepdims=True)
    cnt_ref[...] += jnp.broadcast_to(cnt, cnt_ref.shape)


def _route1(logits, bias_col):
    return pl.pallas_call(
        _route1_kernel,
        grid=(T_ALL // TM,),
        in_specs=[pl.BlockSpec((TM, 128), lambda i: (i, 0)),
                  pl.BlockSpec((N_EXPERTS, 1), lambda i: (0, 0))],
        out_specs=[pl.BlockSpec((N_EXPERTS, TM), lambda i: (0, i)),
                   pl.BlockSpec((N_EXPERTS, 128), lambda i: (0, 0))],
        out_shape=[jax.ShapeDtypeStruct((N_EXPERTS, T_ALL), F32),
                   jax.ShapeDtypeStruct((N_EXPERTS, 128), F32)],
        compiler_params=_cparams(("arbitrary",)),
        name="route1",
    )(logits, bias_col)


def _route2_kernel(enc_ref, cnt_ref, meta_ref, tmap_ref, carry_ref):
    i = pl.program_id(0)

    @pl.when(i == 0)
    def _():
        carry_ref[...] = jnp.zeros_like(carry_ref)

    enc = enc_ref[...]
    sel = enc >= 0.0
    wts = jnp.maximum(enc, 0.0)
    sel_b = sel.astype(BF16)
    ntile = jnp.ceil(cnt_ref[...] * (1.0 / TILE_E))
    er = lax.broadcasted_iota(I32, (N_EXPERTS, N_EXPERTS), 0)
    ec = lax.broadcasted_iota(I32, (N_EXPERTS, N_EXPERTS), 1)
    lower = (ec < er).astype(BF16)
    off_t = jnp.dot(lower, ntile.astype(BF16), preferred_element_type=F32)
    tr = lax.broadcasted_iota(I32, (TM, TM), 0)
    tc = lax.broadcasted_iota(I32, (TM, TM), 1)
    upper = (tr < tc).astype(BF16)
    rank = jnp.dot(sel_b, upper, preferred_element_type=F32) + carry_ref[:, 0:1]
    carry_ref[...] += jnp.broadcast_to(jnp.sum(sel.astype(F32), axis=1, keepdims=True), carry_ref.shape)
    pos = off_t[:, 0:1] * float(TILE_E) + rank
    slot = jnp.dot(lower, sel_b, preferred_element_type=F32)
    rows = []
    for k in range(TOP_K):
        mk = jnp.logical_and(sel, slot == float(k))
        rows.append(jnp.sum(jnp.where(mk, wts, 0.0), axis=0, keepdims=True))
    for k in range(TOP_K):
        mk = jnp.logical_and(sel, slot == float(k))
        rows.append(jnp.sum(jnp.where(mk, pos, 0.0), axis=0, keepdims=True))
    rows.append(jnp.zeros((128 - 2 * TOP_K, TM), F32))
    meta_ref[...] = jnp.concatenate(rows, axis=0).T
    nt = ntile[:, 0:1]
    end_t = off_t[:, 0:1] + nt
    lane = lax.broadcasted_iota(I32, (N_EXPERTS, TMAP_W), 1)
    te = jnp.sum((end_t <= lane.astype(F32)).astype(F32), axis=0, keepdims=True)
    te = jnp.minimum(te, float(N_EXPERTS - 1))
    nact = jnp.sum(nt, axis=0, keepdims=True)
    tmap_ref[...] = jnp.zeros_like(tmap_ref)
    tmap_ref[0:1, :] = te.astype(I32)
    tmap_ref[1:2, :] = jnp.broadcast_to(nact, (1, TMAP_W)).astype(I32)


def _route2(enc, cnt):
    return pl.pallas_call(
        _route2_kernel,
        grid=(T_ALL // TM,),
        in_specs=[pl.BlockSpec((N_EXPERTS, TM), lambda i: (0, i)),
                  pl.BlockSpec((N_EXPERTS, 128), lambda i: (0, 0))],
        out_specs=[pl.BlockSpec((TM, 128), lambda i: (i, 0)),
                   pl.BlockSpec((8, TMAP_W), lambda i: (0, 0))],
        out_shape=[jax.ShapeDtypeStruct((T_ALL, 128), F32),
                   jax.ShapeDtypeStruct((8, TMAP_W), I32)],
        scratch_shapes=[pltpu.VMEM((N_EXPERTS, 128), F32)],
        compiler_params=_cparams(("arbitrary",)),
        name="route2",
    )(enc, cnt)


TOKEN_BITS = 14
TOKEN_MASK = (1 << TOKEN_BITS) - 1
NBUF = 3
TRASH_BASE = TOP_K * T_ALL
SPARE_BASE = TRASH_BASE + NBUF * TILE_E
YS_ROWS = SPARE_BASE + TILE_E
assert T_ALL <= 1 << TOKEN_BITS and YS_ROWS << TOKEN_BITS < 1 << 31


def _pad_codes():
    r = jnp.arange(N_SORTED, dtype=I32)
    q = r % TILE_E
    out_row = TRASH_BASE + ((r // TILE_E) % NBUF) * TILE_E + q
    return (out_row << TOKEN_BITS) | q


def _inv_kernel(pos_ref, init_hbm, inv_hbm, inv_smem, sem):
    i = pl.program_id(0)

    @pl.when(i == 0)
    def _():
        cp = pltpu.make_async_copy(init_hbm, inv_smem, sem)
        cp.start()
        cp.wait()

    def body(t, carry):
        for k in range(TOP_K):
            tok = i * TM + t
            inv_smem[pos_ref[t * TOP_K + k]] = ((k * T_ALL + tok) << TOKEN_BITS) | tok
        return carry

    lax.fori_loop(0, TM, body, 0)

    @pl.when(i == pl.num_programs(0) - 1)
    def _():
        cp = pltpu.make_async_copy(inv_smem, inv_hbm, sem)
        cp.start()
        cp.wait()


def _inv(pos_flat, inv_init):
    return pl.pallas_call(
        _inv_kernel,
        grid=(T_ALL // TM,),
        in_specs=[pl.BlockSpec((TM * TOP_K,), lambda i: (i,), memory_space=pltpu.SMEM),
                  pl.BlockSpec(memory_space=pl.ANY)],
        out_specs=pl.BlockSpec(memory_space=pl.ANY),
        out_shape=jax.ShapeDtypeStruct((N_SORTED,), I32),
        scratch_shapes=[pltpu.SMEM((N_SORTED,), I32), pltpu.SemaphoreType.DMA(())],
        compiler_params=_cparams(("arbitrary",)),
        name="inv",
    )(pos_flat, inv_init)


N_CHUNK = 256
GU_PIECES = D_EXPERT // N_CHUNK
DN_PIECES = D_MODEL // N_CHUNK
N_PIECES = GU_PIECES + DN_PIECES

PIECE_WORK = (D_MODEL * 2,) * GU_PIECES + (D_EXPERT,) * DN_PIECES
PIECE_ROWS = tuple(round(TILE_E * sum(PIECE_WORK[:p]) / sum(PIECE_WORK)) for p in range(N_PIECES + 1))


def _moe_kernel(te_ref, na_ref, inv_ref, h2_hbm, wg_ref, wu_ref, wd_ref, ys_hbm,
                xbuf0, xbuf1, xbuf2, ybuf0, ybuf1, ybuf2, xb, act, wg_scr, wu_scr, wd_scr, gsem, ssem):
    i = pl.program_id(0)
    na = na_ref[0]
    active = i < na
    xbufs = (xbuf0, xbuf1, xbuf2)
    ybufs = (ybuf0, ybuf1, ybuf2)

    def gather_row(tile, s, r):
        tok = inv_ref[tile * TILE_E + r] & TOKEN_MASK
        pltpu.make_async_copy(h2_hbm.at[pl.ds(tok, 1)], xbufs[s].at[pl.ds(r, 1)], gsem.at[s]).start()

    def scatter_row(dst, s, r):
        pltpu.make_async_copy(ybufs[s].at[pl.ds(r, 1)], ys_hbm.at[pl.ds(dst, 1)], ssem.at[s]).start()

    def wait_gather(s):
        pltpu.make_async_copy(h2_hbm.at[pl.ds(0, TILE_E)], xbufs[s], gsem.at[s]).wait()

    def wait_scatter(s):
        pltpu.make_async_copy(ybufs[s], ys_hbm.at[pl.ds(0, TILE_E)], ssem.at[s]).wait()

    @pl.when(i == 0)
    def _():
        zeros = ybufs[NBUF - 1]
        zeros[...] = jnp.zeros_like(zeros)
        for m in range(NBUF):
            cp = pltpu.make_async_copy(zeros, ys_hbm.at[pl.ds(TRASH_BASE + m * TILE_E, TILE_E)], ssem.at[0])
            cp.start()
            cp.wait()
        for t in range(2):
            def body(r, carry):
                gather_row(t, t, r)
                return carry

            lax.fori_loop(0, TILE_E, body, 0)

    new_expert = jnp.logical_or(i == 0, te_ref[i] != te_ref[jnp.maximum(i - 1, 0)])

    @pl.when(jnp.logical_and(active, new_expert))
    def _():
        for c in range(D_MODEL // CAST_ROWS):
            rows = pl.ds(c * CAST_ROWS, CAST_ROWS)
            wg_scr[rows, :] = wg_ref[0, rows, :].astype(BF16)
            wu_scr[rows, :] = wu_ref[0, rows, :].astype(BF16)
        wd_scr[...] = wd_ref[0].astype(BF16)

    def compute_tile(slot):
        s_next = (slot + 2) % NBUF
        s_prev = (slot - 1) % NBUF
        wait_gather(slot)

        @pl.when(i >= 2)
        def _():
            wait_scatter(slot)

        xb[...] = xbufs[slot][...].astype(BF16)
        nxt = jnp.minimum(i + 2, N_TILES_E - 1)
        prev = jnp.maximum(i - 1, 0)
        first = i == 0

        def issue(piece):
            for r in range(PIECE_ROWS[piece], PIECE_ROWS[piece + 1]):
                gather_row(nxt, s_next, r)
                dst = jnp.where(first, SPARE_BASE + r, inv_ref[prev * TILE_E + r] >> TOKEN_BITS)
                scatter_row(dst, s_prev, r)

        for c in range(GU_PIECES):
            cols = slice(c * N_CHUNK, (c + 1) * N_CHUNK)
            g = jnp.dot(xb[...], wg_scr[:, cols], preferred_element_type=F32)
            u = jnp.dot(xb[...], wu_scr[:, cols], preferred_element_type=F32)
            act[:, cols] = (_silu(g) * u).astype(BF16)
            issue(c)
        for c in range(DN_PIECES):
            cols = slice(c * N_CHUNK, (c + 1) * N_CHUNK)
            ybufs[slot][:, cols] = jnp.dot(act[...], wd_scr[:, cols], preferred_element_type=F32)
            issue(GU_PIECES + c)

    def drain(slot):
        s_last = (slot - 1) % NBUF

        def body(r, carry):
            scatter_row(inv_ref[(na - 1) * TILE_E + r] >> TOKEN_BITS, s_last, r)
            return carry

        lax.fori_loop(0, TILE_E, body, 0)
        wait_scatter(s_last)
        wait_scatter((slot - 2) % NBUF)

        @pl.when(na >= 2)
        def _():
            wait_scatter(slot)

        wait_gather(slot)
        wait_gather((slot + 1) % NBUF)

    for s in range(NBUF):
        @pl.when(jnp.logical_and(active, i % NBUF == s))
        def _():
            compute_tile(s)

        @pl.when(jnp.logical_and(i == na, i % NBUF == s))
        def _():
            drain(s)


def _moe(tile_expert, n_active, inv, h2, wg, wu, wd):
    def w_map(i, te, na, iv):
        return (te[i], 0, 0)

    hbm = pl.BlockSpec(memory_space=pl.ANY)
    grid_spec = pltpu.PrefetchScalarGridSpec(
        num_scalar_prefetch=3,
        grid=(N_TILES_E + 1,),
        in_specs=[hbm,
                  pl.BlockSpec((1, D_MODEL, D_EXPERT), w_map),
                  pl.BlockSpec((1, D_MODEL, D_EXPERT), w_map),
                  pl.BlockSpec((1, D_EXPERT, D_MODEL), w_map)],
        out_specs=hbm,
        scratch_shapes=[pltpu.VMEM((TILE_E, D_MODEL), F32)] * (2 * NBUF) + [
                        pltpu.VMEM((TILE_E, D_MODEL), BF16),
                        pltpu.VMEM((TILE_E, D_EXPERT), BF16),
                        pltpu.VMEM((D_MODEL, D_EXPERT), BF16),
                        pltpu.VMEM((D_MODEL, D_EXPERT), BF16),
                        pltpu.VMEM((D_EXPERT, D_MODEL), BF16),
                        pltpu.SemaphoreType.DMA((NBUF,)),
                        pltpu.SemaphoreType.DMA((NBUF,))],
    )
    return pl.pallas_call(
        _moe_kernel,
        grid_spec=grid_spec,
        out_shape=jax.ShapeDtypeStruct((YS_ROWS, D_MODEL), F32),
        compiler_params=_cparams(("arbitrary",)),
        name="moe",
    )(tile_expert, n_active, inv, h2, wg, wu, wd)


def _combine_kernel(meta_ref, *refs):
    ys_refs = refs[:TOP_K]
    h2_ref, x1_ref, mod_ref, wsg_hbm, wsu_hbm, wsd_hbm = refs[TOP_K:TOP_K + 6]
    op_ref, os_ref, wsg_scr, wsu_scr, wsd_scr, stage_a, stage_b, wsem = refs[TOP_K + 6:]
    i = pl.program_id(0)

    @pl.when(i == 0)
    def _():
        _load_cast_weight(wsg_hbm, wsg_scr, stage_a, wsem)
        _load_cast_weight(wsu_hbm, wsu_scr, stage_a, wsem)
        _load_cast_weight(wsd_hbm, wsd_scr, stage_b, wsem)

    h = h2_ref[...].astype(BF16)
    sg = jnp.dot(h, wsg_scr[...], preferred_element_type=F32)
    su = jnp.dot(h, wsu_scr[...], preferred_element_type=F32)
    moe = jnp.dot((_silu(sg) * su).astype(BF16), wsd_scr[...], preferred_element_type=F32)
    for k in range(TOP_K):
        moe = moe + meta_ref[:, k:k + 1] * ys_refs[k][...]
    y = x1_ref[...] + mod_ref[0, 5:6, :] * moe

    @pl.when(i < T_P // TM_C)
    def _():
        op_ref[...] = y

    @pl.when(i >= T_P // TM_C)
    def _():
        os_ref[...] = y


def _combine(meta, ys, h2, x1, mods3, wsg, wsu, wsd):
    tiles_per_group = MOD_GROUP // TM_C
    row = lambda w: pl.BlockSpec((TM_C, w), lambda i: (i, 0))
    slot_rows = lambda k: pl.BlockSpec((TM_C, D_MODEL), lambda i: (k * (T_ALL // TM_C) + i, 0))
    osp, oss = _two_group_specs(TM_C, D_MODEL)
    hbm = pl.BlockSpec(memory_space=pl.ANY)
    return pl.pallas_call(
        _combine_kernel,
        grid=(T_ALL // TM_C,),
        in_specs=[row(128)] + [slot_rows(k) for k in range(TOP_K)] + [
                  row(D_MODEL), row(D_MODEL),
                  pl.BlockSpec((1, 6, D_MODEL), lambda i: (i // tiles_per_group, 0, 0)),
                  hbm, hbm, hbm],
        out_specs=[osp, oss],
        out_shape=[jax.ShapeDtypeStruct((T_P, D_MODEL), F32),
                   jax.ShapeDtypeStruct((T_S, D_MODEL), F32)],
        scratch_shapes=[pltpu.VMEM((D_MODEL, D_SHARED), BF16),
                        pltpu.VMEM((D_MODEL, D_SHARED), BF16),
                        pltpu.VMEM((D_SHARED, D_MODEL), BF16),
                        pltpu.VMEM((D_MODEL, D_SHARED), F32),
                        pltpu.VMEM((D_SHARED, D_MODEL), F32),
                        pltpu.SemaphoreType.DMA(())],
        compiler_params=_cparams(("arbitrary",)),
        name="combine",
    )(meta, *([ys] * TOP_K), h2, x1, mods3, wsg, wsu, wsd)


def _rope_tables():
    rows = DEC_SEQ // GRID_W
    row = jnp.repeat(jnp.arange(rows, dtype=F32), GRID_W)
    col = jnp.tile(jnp.arange(GRID_W, dtype=F32), rows)
    n_freq = HEAD_DIM // 4
    inv = ROPE_THETA ** (-jnp.arange(n_freq, dtype=F32) / n_freq)
    ang = jnp.concatenate([row[:, None] * inv, col[:, None] * inv], axis=-1)
    cos, sin = jnp.cos(ang), jnp.sin(ang)
    cos2 = jnp.concatenate([cos, cos], axis=-1)
    sin2 = jnp.concatenate([-sin, sin], axis=-1)
    cos_tab = jnp.concatenate([jnp.ones((TM_IN, HEAD_DIM), F32), cos2], axis=0)
    sin_tab = jnp.concatenate([jnp.zeros((TM_IN, HEAD_DIM), F32), sin2], axis=0)
    return cos_tab, sin_tab


def kernel(x_prompt, x_sample, cache_k, cache_v, c, c_ctx, w_ada, b_ada, norm_mix_g, norm_ffn_g, w_in, conv_w,
           q_norm_g, k_norm_g, attn_sink, w_conv_out, w_attn_out, w_out, router_w, router_bias, w_exp_gate,
           w_exp_up, w_exp_down, w_sh_gate, w_sh_up, w_sh_down):
    l = 0
    xp = x_prompt.reshape(T_P, D_MODEL)
    xs = x_sample.reshape(T_S, D_MODEL)

    cond = jnp.concatenate([c_ctx[None, :], c, jnp.zeros((8 - N_MOD, D_MODEL), F32)], axis=0)
    mods = _ada(cond.T, w_ada[l], b_ada[l][None, :])
    mods3 = mods[:N_MOD].reshape(N_MOD, 6, D_MODEL)

    h = _prenorm(xp, xs, mods3, norm_mix_g[l][None, :])
    cos_tab, sin_tab = _rope_tables()
    proj, k32, v32 = _inproj(h, w_in[l], q_norm_g[l][None, :], k_norm_g[l][None, :], cos_tab, sin_tab)

    sink = attn_sink[l]
    kc = cache_k[:, l].reshape(DEC_BATCH, PAST_LEN, KV_WIDTH)
    vc = cache_v[:, l].reshape(DEC_BATCH, PAST_LEN, KV_WIDTH)
    attn_p = _attn_p(sink, proj)
    attn_s = _attn_s(sink, proj, kc, vc)

    z1 = _conv(proj, conv_w[l], w_conv_out[l])

    wr_pad = jnp.pad(router_w[l], ((0, 0), (0, 128 - N_EXPERTS)))
    x1, h2, logits = _merge(xp, xs, attn_p, attn_s, z1, proj, mods3, norm_ffn_g[l][None, :], wr_pad,
                            w_attn_out[l], w_out[l])

    enc, cnt = _route1(logits, router_bias[l][:, None])
    meta, tmap = _route2(enc, cnt)
    pos_flat = meta[:, TOP_K:2 * TOP_K].astype(I32).reshape(T_ALL * TOP_K)
    tile_expert = tmap[0]
    n_active = tmap[1, :1]

    inv = _inv(pos_flat, _pad_codes())
    ys = _moe(tile_expert, n_active, inv, h2, w_exp_gate[l], w_exp_up[l], w_exp_down[l])
    y_p, y_s = _combine(meta, ys, h2, x1, mods3, w_sh_gate[l], w_sh_up[l], w_sh_down[l])

    y_prompt = y_p.reshape(BATCH, SEQ, D_MODEL)
    y_sample = y_s.reshape(DEC_BATCH, DEC_SEQ, D_MODEL)
    new_k = k32.reshape(BATCH, 1, SEQ, N_KV_HEADS, HEAD_DIM)
    new_v = v32.reshape(BATCH, 1, SEQ, N_KV_HEADS, HEAD_DIM)
    return (y_prompt, y_sample, new_k, new_v)
```

```python
import jax
import jax.numpy as jnp
from jax import lax
from jax.experimental import pallas as pl
from jax.experimental.pallas import tpu as pltpu

F32 = jnp.float32
BF16 = jnp.bfloat16
I32 = jnp.int32

D_MODEL = 2048
BATCH = 16
SEQ = 256
DEC_BATCH = 2
DEC_SEQ = 4096
PAST_LEN = 512
GRID_W = 64
N_HEADS = 16
N_KV_HEADS = 4
HEAD_DIM = 128
GROUP = N_HEADS // N_KV_HEADS
ATTN_WIDTH = N_HEADS * HEAD_DIM
KV_WIDTH = N_KV_HEADS * HEAD_DIM
WINDOW = 128
CONV_WIDTH = D_MODEL
N_EXPERTS = 64
TOP_K = 8
N_EXPERT_GROUPS = 8
GROUP_SIZE = N_EXPERTS // N_EXPERT_GROUPS
TOPK_GROUPS = 4
D_EXPERT = 512
D_SHARED = 512
ROUTED_SCALE = 2.5
ROPE_THETA = 10000.0
EPS = 1e-6
NEG_INF = -1e30
ATTN_SCALE = HEAD_DIM ** -0.5

T_P = BATCH * SEQ
T_S = DEC_BATCH * DEC_SEQ
T_ALL = T_P + T_S
MOD_GROUP = 4096
N_MOD = 1 + DEC_BATCH
assert T_P == MOD_GROUP and DEC_SEQ == MOD_GROUP

COL_CB, COL_CC, COL_CX, COL_Q, COL_GC, COL_GA = (i * D_MODEL for i in range(6))
COL_K = 6 * D_MODEL
COL_V = COL_K + KV_WIDTH
IN_WIDTH = COL_V + KV_WIDTH
W_COL_KV = 3 * CONV_WIDTH + ATTN_WIDTH

TM_IN = 1024
TN_IN = 1024
TM = 256
TM_C = 128
TILE_E = 256
N_SORTED = T_ALL * TOP_K + N_EXPERTS * TILE_E
N_TILES_E = N_SORTED // TILE_E
TMAP_W = 512
CAST_ROWS = 512
VMEM_LIMIT = 56 * 1024 * 1024


def _cparams(sem):
    return pltpu.CompilerParams(dimension_semantics=sem, vmem_limit_bytes=VMEM_LIMIT)


def _silu(x):
    return x * jax.nn.sigmoid(x)


def _load_cast_weight(w_hbm, w_scr, stage, sem):
    rows = stage.shape[0]
    for c in range(w_hbm.shape[0] // rows):
        cp = pltpu.make_async_copy(w_hbm.at[pl.ds(c * rows, rows)], stage, sem)
        cp.start()
        cp.wait()
        w_scr[pl.ds(c * rows, rows), :] = stage[...].astype(BF16)


ADA_TN = 1024
ADA_CHUNK = 256


def _ada_kernel(ct_ref, w_ref, b_ref, o_ref):
    tn = w_ref.shape[1]

    def body(c, accs):
        k0 = pl.multiple_of(c * ADA_CHUNK, ADA_CHUNK)
        wch = w_ref[pl.ds(k0, ADA_CHUNK), :]
        sch = _silu(ct_ref[pl.ds(k0, ADA_CHUNK), :])
        out = []
        for r in range(N_MOD):
            p = wch * sch[:, r:r + 1]
            out.append(accs[r] + p.reshape(ADA_CHUNK // 8, 8, tn).sum(axis=0))
        return tuple(out)

    accs = lax.fori_loop(0, D_MODEL // ADA_CHUNK, body,
                         tuple(jnp.zeros((8, tn), F32) for _ in range(N_MOD)))
    o_ref[...] = jnp.zeros_like(o_ref)
    for r in range(N_MOD):
        o_ref[r:r + 1, :] = jnp.sum(accs[r], axis=0, keepdims=True) + b_ref[...]


def _ada(cond_t, w_ada, b_ada):
    n = w_ada.shape[1]
    return pl.pallas_call(
        _ada_kernel,
        grid=(n // ADA_TN,),
        in_specs=[pl.BlockSpec((D_MODEL, 8), lambda j: (0, 0)),
                  pl.BlockSpec((D_MODEL, ADA_TN), lambda j: (0, j)),
                  pl.BlockSpec((1, ADA_TN), lambda j: (0, j))],
        out_specs=pl.BlockSpec((8, ADA_TN), lambda j: (0, j)),
        out_shape=jax.ShapeDtypeStruct((8, n), F32),
        compiler_params=_cparams(("arbitrary",)),
        name="ada",
    )(cond_t, w_ada, b_ada)


def _rms_mod(x, g, shift, scale):
    ms = jnp.mean(x * x, axis=-1, keepdims=True)
    y = x * lax.rsqrt(ms + EPS) * g
    return y * (1.0 + scale) + shift


def _two_group_specs(tile, width):
    n_ctx = T_P // tile
    return (pl.BlockSpec((tile, width), lambda i: (jnp.minimum(i, n_ctx - 1), 0)),
            pl.BlockSpec((tile, width), lambda i: (jnp.maximum(i - n_ctx, 0), 0)))


def _prenorm_kernel(xp_ref, xs_ref, mod_ref, g_ref, h_ref):
    i = pl.program_id(0)
    shift = mod_ref[0, 0:1, :]
    scale = mod_ref[0, 1:2, :]
    g = g_ref[...]

    @pl.when(i < T_P // TM)
    def _():
        h_ref[...] = _rms_mod(xp_ref[...], g, shift, scale).astype(BF16)

    @pl.when(i >= T_P // TM)
    def _():
        h_ref[...] = _rms_mod(xs_ref[...], g, shift, scale).astype(BF16)


def _prenorm(xp, xs, mods3, g_mix):
    tiles_per_group = MOD_GROUP // TM
    sp, ss = _two_group_specs(TM, D_MODEL)
    return pl.pallas_call(
        _prenorm_kernel,
        grid=(T_ALL // TM,),
        in_specs=[sp, ss,
                  pl.BlockSpec((1, 6, D_MODEL), lambda i: (i // tiles_per_group, 0, 0)),
                  pl.BlockSpec((1, D_MODEL), lambda i: (0, 0))],
        out_specs=pl.BlockSpec((TM, D_MODEL), lambda i: (i, 0)),
        out_shape=jax.ShapeDtypeStruct((T_ALL, D_MODEL), BF16),
        compiler_params=_cparams(("arbitrary",)),
        name="prenorm",
    )(xp, xs, mods3, g_mix)


def _head_norm_rope(a, g, cos2, sin2, scale):
    ms = jnp.mean(a * a, axis=-1, keepdims=True)
    a = a * lax.rsqrt(ms + EPS) * g
    a = a * cos2 + pltpu.roll(a, HEAD_DIM // 2, 1) * sin2
    return a * scale


N_J = IN_WIDTH // TN_IN
J_Q0 = COL_Q // TN_IN
J_Q1 = J_Q0 + ATTN_WIDTH // TN_IN
J_KV = COL_K // TN_IN
assert J_KV == N_J - 1
N_CTX_IN = T_P // TM_IN


def _inproj_kernel(h_ref, w_ref, qg_ref, kg_ref, cos_ref, sin_ref, o_ref, k_ref, v_ref, w_scr):
    j = pl.program_id(0)
    i = pl.program_id(1)

    @pl.when(i == 0)
    def _():
        for c in range(D_MODEL // CAST_ROWS):
            rows = pl.ds(c * CAST_ROWS, CAST_ROWS)
            w_scr[rows, :] = w_ref[rows, :].astype(BF16)

    acc = jnp.dot(h_ref[...], w_scr[...], preferred_element_type=F32)
    is_q = jnp.logical_and(j >= J_Q0, j < J_Q1)
    is_kv = j == J_KV

    @pl.when(jnp.logical_not(jnp.logical_or(is_q, is_kv)))
    def _():
        o_ref[...] = acc.astype(BF16)

    @pl.when(is_q)
    def _():
        cos2 = cos_ref[...]
        sin2 = sin_ref[...]
        g = qg_ref[...]
        for h in range(TN_IN // HEAD_DIM):
            cols = slice(h * HEAD_DIM, (h + 1) * HEAD_DIM)
            o_ref[:, cols] = _head_norm_rope(acc[:, cols], g, cos2, sin2, ATTN_SCALE).astype(BF16)

    @pl.when(is_kv)
    def _():
        cos2 = cos_ref[...]
        sin2 = sin_ref[...]
        g = kg_ref[...]
        is_ctx = i < N_CTX_IN
        for h in range(N_KV_HEADS):
            cols = slice(h * HEAD_DIM, (h + 1) * HEAD_DIM)
            kh = _head_norm_rope(acc[:, cols], g, cos2, sin2, 1.0)
            o_ref[:, cols] = kh.astype(BF16)

            @pl.when(is_ctx)
            def _():
                k_ref[:, cols] = kh

        v = acc[:, KV_WIDTH:]
        o_ref[:, KV_WIDTH:] = v.astype(BF16)

        @pl.when(is_ctx)
        def _():
            v_ref[...] = v


def _inproj(h, w_in, qg, kg, cos_tab, sin_tab):
    pos_tiles = DEC_SEQ // TM_IN

    def tab_map(j, i):
        return (jnp.where(i < N_CTX_IN, 0, 1 + i % pos_tiles), 0)

    def w_map(j, i):
        return (0, jnp.where(j < W_COL_KV // TN_IN, j, jnp.where(j == J_KV, W_COL_KV // TN_IN, j + 1)))

    def kv_map(j, i):
        return (jnp.where(j == J_KV, jnp.minimum(i, N_CTX_IN - 1), 0), 0)

    return pl.pallas_call(
        _inproj_kernel,
        grid=(N_J, T_ALL // TM_IN),
        in_specs=[pl.BlockSpec((TM_IN, D_MODEL), lambda j, i: (i, 0)),
                  pl.BlockSpec((D_MODEL, TN_IN), w_map, pipeline_mode=pl.Buffered(1)),
                  pl.BlockSpec((1, HEAD_DIM), lambda j, i: (0, 0)),
                  pl.BlockSpec((1, HEAD_DIM), lambda j, i: (0, 0)),
                  pl.BlockSpec((TM_IN, HEAD_DIM), tab_map),
                  pl.BlockSpec((TM_IN, HEAD_DIM), tab_map)],
        out_specs=[pl.BlockSpec((TM_IN, TN_IN), lambda j, i: (i, j)),
                   pl.BlockSpec((TM_IN, KV_WIDTH), kv_map),
                   pl.BlockSpec((TM_IN, KV_WIDTH), kv_map)],
        out_shape=[jax.ShapeDtypeStruct((T_ALL, IN_WIDTH), BF16),
                   jax.ShapeDtypeStruct((T_P, KV_WIDTH), F32),
                   jax.ShapeDtypeStruct((T_P, KV_WIDTH), F32)],
        scratch_shapes=[pltpu.VMEM((D_MODEL, TN_IN), BF16)],
        compiler_params=_cparams(("arbitrary", "arbitrary")),
        name="inproj",
    )(h, w_in, qg, kg, cos_tab, sin_tab)


def _dot_nt(a, b):
    return lax.dot_general(a, b, (((1,), (1,)), ((), ())), preferred_element_type=F32)


def _attn_p_kernel(sink_ref, q_ref, k_ref, v_ref, o_ref):
    for kh in range(N_KV_HEADS):
        kcols = slice(kh * HEAD_DIM, (kh + 1) * HEAD_DIM)
        k = k_ref[:, kcols]
        v = v_ref[:, kcols]
        for gi in range(GROUP):
            h = kh * GROUP + gi
            cols = slice(h * HEAD_DIM, (h + 1) * HEAD_DIM)
            s = _dot_nt(q_ref[:, cols], k)
            sk = sink_ref[h]
            m = jnp.maximum(jnp.max(s, axis=-1, keepdims=True), sk)
            p = jnp.exp(s - m)
            den = jnp.sum(p, axis=-1, keepdims=True) + jnp.exp(sk - m)
            o = jnp.dot(p.astype(BF16), v, preferred_element_type=F32) / den
            o_ref[:, cols] = o.astype(BF16)


def _attn_p(sink, proj):
    return pl.pallas_call(
        _attn_p_kernel,
        grid=(BATCH,),
        in_specs=[pl.BlockSpec(memory_space=pltpu.SMEM),
                  pl.BlockSpec((SEQ, ATTN_WIDTH), lambda b: (b, COL_Q // ATTN_WIDTH)),
                  pl.BlockSpec((SEQ, KV_WIDTH), lambda b: (b, COL_K // KV_WIDTH)),
                  pl.BlockSpec((SEQ, KV_WIDTH), lambda b: (b, COL_V // KV_WIDTH))],
        out_specs=pl.BlockSpec((SEQ, ATTN_WIDTH), lambda b: (b, 0)),
        out_shape=jax.ShapeDtypeStruct((T_P, ATTN_WIDTH), BF16),
        compiler_params=_cparams(("arbitrary",)),
        name="attn_p",
    )(sink, proj, proj, proj)


QB = 256
BAND = QB + 2 * WINDOW


def _attn_s_kernel(sink_ref, q_ref, k_ref, v_ref, kc_ref, vc_ref, o_ref):
    i = pl.program_id(1)
    start = jnp.clip(i * QB - WINDOW, 0, DEC_SEQ - BAND)
    start = pl.multiple_of(start, WINDOW)
    qpos = i * QB + lax.broadcasted_iota(I32, (QB, BAND), 0)
    kpos = start + lax.broadcasted_iota(I32, (QB, BAND), 1)
    valid = jnp.abs(qpos - kpos) <= WINDOW
    for kh in range(N_KV_HEADS):
        kcols = slice(kh * HEAD_DIM, (kh + 1) * HEAD_DIM)
        kb = k_ref[pl.ds(start, BAND), kcols]
        vb = v_ref[pl.ds(start, BAND), kcols]
        kc = kc_ref[0, :, kcols].astype(BF16)
        vc = vc_ref[0, :, kcols].astype(BF16)
        for gi in range(GROUP):
            h = kh * GROUP + gi
            cols = slice(h * HEAD_DIM, (h + 1) * HEAD_DIM)
            q = q_ref[:, cols]
            s_loc = jnp.where(valid, _dot_nt(q, kb), NEG_INF)
            s_ctx = _dot_nt(q, kc)
            sk = sink_ref[h]
            m = jnp.maximum(jnp.maximum(jnp.max(s_loc, axis=-1, keepdims=True),
                                        jnp.max(s_ctx, axis=-1, keepdims=True)), sk)
            p_loc = jnp.exp(s_loc - m)
            p_ctx = jnp.exp(s_ctx - m)
            den = (jnp.sum(p_loc, axis=-1, keepdims=True) + jnp.sum(p_ctx, axis=-1, keepdims=True)
                   + jnp.exp(sk - m))
            o = (jnp.dot(p_loc.astype(BF16), vb, preferred_element_type=F32)
                 + jnp.dot(p_ctx.astype(BF16), vc, preferred_element_type=F32)) / den
            o_ref[:, cols] = o.astype(BF16)


def _attn_s(sink, proj, kc, vc):
    nq = DEC_SEQ // QB
    first = T_P // QB
    seq_blk = T_P // DEC_SEQ
    return pl.pallas_call(
        _attn_s_kernel,
        grid=(DEC_BATCH, nq),
        in_specs=[pl.BlockSpec(memory_space=pltpu.SMEM),
                  pl.BlockSpec((QB, ATTN_WIDTH), lambda b, i: (first + b * nq + i, COL_Q // ATTN_WIDTH)),
                  pl.BlockSpec((DEC_SEQ, KV_WIDTH), lambda b, i: (seq_blk + b, COL_K // KV_WIDTH)),
                  pl.BlockSpec((DEC_SEQ, KV_WIDTH), lambda b, i: (seq_blk + b, COL_V // KV_WIDTH)),
                  pl.BlockSpec((1, PAST_LEN, KV_WIDTH), lambda b, i: (b, 0, 0)),
                  pl.BlockSpec((1, PAST_LEN, KV_WIDTH), lambda b, i: (b, 0, 0))],
        out_specs=pl.BlockSpec((QB, ATTN_WIDTH), lambda b, i: (b * nq + i, 0)),
        out_shape=jax.ShapeDtypeStruct((T_S, ATTN_WIDTH), BF16),
        compiler_params=_cparams(("arbitrary", "arbitrary")),
        name="attn_s",
    )(sink, proj, proj, proj, kc, vc)


HALO = 16


def _conv_kernel(cb_ref, cc_ref, cx_ref, ccp_ref, cxp_ref, ccn_ref, cxn_ref, gc_ref, cw_ref, w_hbm, o_ref,
                 w_scr, stage, sem):
    i = pl.program_id(0)

    @pl.when(i == 0)
    def _():
        _load_cast_weight(w_hbm, w_scr, stage, sem)

    n_ctx_tiles = T_P // TM
    per_seq = DEC_SEQ // TM
    is_first = jnp.logical_or(i < n_ctx_tiles, (i - n_ctx_tiles) % per_seq == 0)
    is_last = jnp.logical_or(i < n_ctx_tiles, (i - n_ctx_tiles) % per_seq == per_seq - 1)
    p = cc_ref[...].astype(F32) * cx_ref[...].astype(F32)
    prev_row = ccp_ref[HALO - 1:HALO, :].astype(F32) * cxp_ref[HALO - 1:HALO, :].astype(F32)
    next_row = ccn_ref[0:1, :].astype(F32) * cxn_ref[0:1, :].astype(F32)
    prev_row = jnp.where(is_first, 0.0, prev_row)
    next_row = jnp.where(is_last, 0.0, next_row)
    rows = lax.broadcasted_iota(I32, (TM, 1), 0)
    p_prev = jnp.where(rows == 0, prev_row, pltpu.roll(p, 1, 0))
    p_next = jnp.where(rows == TM - 1, next_row, pltpu.roll(p, TM - 1, 0))
    conv = p_prev * cw_ref[0:1, :] + p * cw_ref[1:2, :] + p_next * cw_ref[2:3, :]
    u = cb_ref[...].astype(F32) * conv
    y = jnp.dot(u.astype(BF16), w_scr[...], preferred_element_type=F32)
    o_ref[...] = (jax.nn.sigmoid(gc_ref[...].astype(F32)) * y).astype(BF16)


def _conv(proj, conv_w, w_conv_out):
    hb = TM // HALO
    last_hb = T_ALL // HALO - 1
    wide = lambda c: pl.BlockSpec((TM, D_MODEL), lambda i: (i, c // D_MODEL))
    prev = lambda c: pl.BlockSpec((HALO, D_MODEL), lambda i: (jnp.maximum(i * hb - 1, 0), c // D_MODEL))
    nxt = lambda c: pl.BlockSpec((HALO, D_MODEL), lambda i: (jnp.minimum((i + 1) * hb, last_hb), c // D_MODEL))
    return pl.pallas_call(
        _conv_kernel,
        grid=(T_ALL // TM,),
        in_specs=[wide(COL_CB), wide(COL_CC), wide(COL_CX),
                  prev(COL_CC), prev(COL_CX), nxt(COL_CC), nxt(COL_CX),
                  wide(COL_GC),
                  pl.BlockSpec((3, CONV_WIDTH), lambda i: (0, 0)),
                  pl.BlockSpec(memory_space=pl.ANY)],
        out_specs=pl.BlockSpec((TM, D_MODEL), lambda i: (i, 0)),
        out_shape=jax.ShapeDtypeStruct((T_ALL, D_MODEL), BF16),
        scratch_shapes=[pltpu.VMEM((CONV_WIDTH, D_MODEL), BF16),
                        pltpu.VMEM((CAST_ROWS, D_MODEL), F32),
                        pltpu.SemaphoreType.DMA(())],
        compiler_params=_cparams(("arbitrary",)),
        name="conv",
    )(proj, proj, proj, proj, proj, proj, proj, proj, conv_w, w_conv_out)


def _merge_kernel(xp_ref, xs_ref, attn_p_ref, attn_s_ref, z1_ref, ga_ref, mod_ref, g2_ref, wr_ref,
                  wao_hbm, wo_hbm, x1_ref, h2_ref, lg_ref, wao_scr, wo_scr, stage, sem):
    i = pl.program_id(0)

    @pl.when(i == 0)
    def _():
        _load_cast_weight(wao_hbm, wao_scr, stage, sem)
        _load_cast_weight(wo_hbm, wo_scr, stage, sem)

    is_ctx = i < T_P // TM
    attn = jnp.where(is_ctx, attn_p_ref[...], attn_s_ref[...])
    x = jnp.where(is_ctx, xp_ref[...], xs_ref[...])
    ya = jnp.dot(attn, wao_scr[...], preferred_element_type=F32)
    z = z1_ref[...].astype(F32) + jax.nn.sigmoid(ga_ref[...].astype(F32)) * ya
    mix = jnp.dot(z.astype(BF16), wo_scr[...], preferred_element_type=F32)
    x1 = x + mod_ref[0, 2:3, :] * mix
    x1_ref[...] = x1
    h = _rms_mod(x1, g2_ref[...], mod_ref[0, 3:4, :], mod_ref[0, 4:5, :])
    h2_ref[...] = h
    h_hi = h.astype(BF16)
    h_lo = (h - h_hi.astype(F32)).astype(BF16)
    wr = wr_ref[...]
    wr_hi = wr.astype(BF16)
    wr_lo = (wr - wr_hi.astype(F32)).astype(BF16)
    lg_ref[...] = (jnp.dot(h_hi, wr_hi, preferred_element_type=F32)
                   + jnp.dot(h_hi, wr_lo, preferred_element_type=F32)
                   + jnp.dot(h_lo, wr_hi, preferred_element_type=F32))


def _merge(xp, xs, attn_p, attn_s, z1, proj, mods3, g_ffn, wr_pad, w_attn_out, w_out):
    tiles_per_group = MOD_GROUP // TM
    const = lambda shape: pl.BlockSpec(shape, lambda i: (0, 0))
    row = lambda w: pl.BlockSpec((TM, w), lambda i: (i, 0))
    xsp, xss = _two_group_specs(TM, D_MODEL)
    asp, ass = _two_group_specs(TM, ATTN_WIDTH)
    return pl.pallas_call(
        _merge_kernel,
        grid=(T_ALL // TM,),
        in_specs=[xsp, xss, asp, ass, row(D_MODEL),
                  pl.BlockSpec((TM, D_MODEL), lambda i: (i, COL_GA // D_MODEL)),
                  pl.BlockSpec((1, 6, D_MODEL), lambda i: (i // tiles_per_group, 0, 0)),
                  const((1, D_MODEL)), const((D_MODEL, 128)),
                  pl.BlockSpec(memory_space=pl.ANY), pl.BlockSpec(memory_space=pl.ANY)],
        out_specs=[row(D_MODEL), row(D_MODEL), row(128)],
        out_shape=[jax.ShapeDtypeStruct((T_ALL, D_MODEL), F32),
                   jax.ShapeDtypeStruct((T_ALL, D_MODEL), F32),
                   jax.ShapeDtypeStruct((T_ALL, 128), F32)],
        scratch_shapes=[pltpu.VMEM((ATTN_WIDTH, D_MODEL), BF16),
                        pltpu.VMEM((D_MODEL, D_MODEL), BF16),
                        pltpu.VMEM((CAST_ROWS, D_MODEL), F32),
                        pltpu.SemaphoreType.DMA(())],
        compiler_params=_cparams(("arbitrary",)),
        name="merge",
    )(xp, xs, attn_p, attn_s, z1, proj, mods3, g_ffn, wr_pad, w_attn_out, w_out)


def _route1_kernel(lg_ref, bias_ref, enc_ref, cnt_ref):
    i = pl.program_id(0)
    lt = lg_ref[...].T[:N_EXPERTS, :]
    scores = jax.nn.sigmoid(lt)
    biased = scores + bias_ref[...]
    b3 = biased.reshape(N_EXPERT_GROUPS, GROUP_SIZE, TM)
    mi = lax.broadcasted_iota(I32, b3.shape, 1)
    m1 = jnp.max(b3, axis=1, keepdims=True)
    idx1 = jnp.min(jnp.where(b3 == m1, mi, GROUP_SIZE), axis=1, keepdims=True)
    m2 = jnp.max(jnp.where(mi == idx1, -jnp.inf, b3), axis=1, keepdims=True)
    gs = (m1 + m2).reshape(N_EXPERT_GROUPS, TM)
    gidx = lax.broadcasted_iota(I32, gs.shape, 0)
    grank = jnp.zeros(gs.shape, I32)
    for j in range(N_EXPERT_GROUPS):
        gj = gs[j:j + 1, :]
        beats = jnp.logical_or(gj > gs, jnp.logical_and(gj == gs, j < gidx))
        grank = grank + beats.astype(I32)
    gsel = grank < TOPK_GROUPS
    emask = jnp.broadcast_to(gsel[:, None, :], b3.shape).reshape(N_EXPERTS, TM)
    masked = jnp.where(emask, biased, NEG_INF)
    eidx = lax.broadcasted_iota(I32, masked.shape, 0)
    erank = jnp.zeros(masked.shape, I32)
    for j in range(N_EXPERTS):
        vj = masked[j:j + 1, :]
        beats = jnp.logical_or(vj > masked, jnp.logical_and(vj == masked, j < eidx))
        erank = erank + beats.astype(I32)
    sel = erank < TOP_K
    wsel = jnp.where(sel, scores, 0.0)
    den = jnp.sum(wsel, axis=0, keepdims=True)
    wts = wsel / den * ROUTED_SCALE
    enc_ref[...] = jnp.where(sel, wts, -1.0)

    @pl.when(i == 0)
    def _():
        cnt_ref[...] = jnp.zeros_like(cnt_ref)

    cnt = jnp.sum(sel.astype(F32), axis=1, keepdims=True)
    cnt_ref[...] += jnp.broadcast_to(cnt, cnt_ref.shape)


def _route1(logits, bias_col):
    return pl.pallas_call(
        _route1_kernel,
        grid=(T_ALL // TM,),
        in_specs=[pl.BlockSpec((TM, 128), lambda i: (i, 0)),
                  pl.BlockSpec((N_EXPERTS, 1), lambda i: (0, 0))],
        out_specs=[pl.BlockSpec((N_EXPERTS, TM), lambda i: (0, i)),
                   pl.BlockSpec((N_EXPERTS, 128), lambda i: (0, 0))],
        out_shape=[jax.ShapeDtypeStruct((N_EXPERTS, T_ALL), F32),
                   jax.ShapeDtypeStruct((N_EXPERTS, 128), F32)],
        compiler_params=_cparams(("arbitrary",)),
        name="route1",
    )(logits, bias_col)


def _route2_kernel(enc_ref, cnt_ref, meta_ref, tmap_ref, carry_ref):
    i = pl.program_id(0)

    @pl.when(i == 0)
    def _():
        carry_ref[...] = jnp.zeros_like(carry_ref)

    enc = enc_ref[...]
    sel = enc >= 0.0
    wts = jnp.maximum(enc, 0.0)
    sel_b = sel.astype(BF16)
    ntile = jnp.ceil(cnt_ref[...] * (1.0 / TILE_E))
    er = lax.broadcasted_iota(I32, (N_EXPERTS, N_EXPERTS), 0)
    ec = lax.broadcasted_iota(I32, (N_EXPERTS, N_EXPERTS), 1)
    lower = (ec < er).astype(BF16)
    off_t = jnp.dot(lower, ntile.astype(BF16), preferred_element_type=F32)
    tr = lax.broadcasted_iota(I32, (TM, TM), 0)
    tc = lax.broadcasted_iota(I32, (TM, TM), 1)
    upper = (tr < tc).astype(BF16)
    rank = jnp.dot(sel_b, upper, preferred_element_type=F32) + carry_ref[:, 0:1]
    carry_ref[...] += jnp.broadcast_to(jnp.sum(sel.astype(F32), axis=1, keepdims=True), carry_ref.shape)
    pos = off_t[:, 0:1] * float(TILE_E) + rank
    slot = jnp.dot(lower, sel_b, preferred_element_type=F32)
    rows = []
    for k in range(TOP_K):
        mk = jnp.logical_and(sel, slot == float(k))
        rows.append(jnp.sum(jnp.where(mk, wts, 0.0), axis=0, keepdims=True))
    for k in range(TOP_K):
        mk = jnp.logical_and(sel, slot == float(k))
        rows.append(jnp.sum(jnp.where(mk, pos, 0.0), axis=0, keepdims=True))
    rows.append(jnp.zeros((128 - 2 * TOP_K, TM), F32))
    meta_ref[...] = jnp.concatenate(rows, axis=0).T
    nt = ntile[:, 0:1]
    end_t = off_t[:, 0:1] + nt
    lane = lax.broadcasted_iota(I32, (N_EXPERTS, TMAP_W), 1)
    te = jnp.sum((end_t <= lane.astype(F32)).astype(F32), axis=0, keepdims=True)
    te = jnp.minimum(te, float(N_EXPERTS - 1))
    nact = jnp.sum(nt, axis=0, keepdims=True)
    erow = lax.broadcasted_iota(I32, (N_EXPERTS, TMAP_W), 0).astype(F32)
    nxt = jnp.sum(jnp.where(erow == te, end_t, 0.0), axis=0, keepdims=True)
    tmap_ref[...] = jnp.zeros_like(tmap_ref)
    tmap_ref[0:1, :] = te.astype(I32)
    tmap_ref[1:2, :] = jnp.broadcast_to(nact, (1, TMAP_W)).astype(I32)
    tmap_ref[2:3, :] = nxt.astype(I32)


def _route2(enc, cnt):
    return pl.pallas_call(
        _route2_kernel,
        grid=(T_ALL // TM,),
        in_specs=[pl.BlockSpec((N_EXPERTS, TM), lambda i: (0, i)),
                  pl.BlockSpec((N_EXPERTS, 128), lambda i: (0, 0))],
        out_specs=[pl.BlockSpec((TM, 128), lambda i: (i, 0)),
                   pl.BlockSpec((8, TMAP_W), lambda i: (0, 0))],
        out_shape=[jax.ShapeDtypeStruct((T_ALL, 128), F32),
                   jax.ShapeDtypeStruct((8, TMAP_W), I32)],
        scratch_shapes=[pltpu.VMEM((N_EXPERTS, 128), F32)],
        compiler_params=_cparams(("arbitrary",)),
        name="route2",
    )(enc, cnt)


TOKEN_BITS = 14
TOKEN_MASK = (1 << TOKEN_BITS) - 1
NBUF = 3
TRASH_BASE = TOP_K * T_ALL
SPARE_BASE = TRASH_BASE + NBUF * TILE_E
YS_ROWS = SPARE_BASE + TILE_E
assert T_ALL <= 1 << TOKEN_BITS and YS_ROWS << TOKEN_BITS < 1 << 31


def _pad_codes():
    r = jnp.arange(N_SORTED, dtype=I32)
    q = r % TILE_E
    out_row = TRASH_BASE + ((r // TILE_E) % NBUF) * TILE_E + q
    return (out_row << TOKEN_BITS) | q


INV_UNROLL = 4


def _inv_kernel(pos_ref, init_hbm, inv_hbm, inv_smem, sem):
    i = pl.program_id(0)

    @pl.when(i == 0)
    def _():
        cp = pltpu.make_async_copy(init_hbm, inv_smem, sem)
        cp.start()
        cp.wait()

    per_token = (1 << TOKEN_BITS) + 1
    per_slot = T_ALL << TOKEN_BITS

    def body(t4, base):
        for u in range(INV_UNROLL):
            for k in range(TOP_K):
                inv_smem[pos_ref[(t4 * INV_UNROLL + u) * TOP_K + k]] = base + (u * per_token + k * per_slot)
        return base + INV_UNROLL * per_token

    lax.fori_loop(0, TM // INV_UNROLL, body, i * (TM * per_token))

    @pl.when(i == pl.num_programs(0) - 1)
    def _():
        cp = pltpu.make_async_copy(inv_smem, inv_hbm, sem)
        cp.start()
        cp.wait()


def _inv(pos_flat, inv_init):
    return pl.pallas_call(
        _inv_kernel,
        grid=(T_ALL // TM,),
        in_specs=[pl.BlockSpec((TM * TOP_K,), lambda i: (i,), memory_space=pltpu.SMEM),
                  pl.BlockSpec(memory_space=pl.ANY)],
        out_specs=pl.BlockSpec(memory_space=pl.ANY),
        out_shape=jax.ShapeDtypeStruct((N_SORTED,), I32),
        scratch_shapes=[pltpu.SMEM((N_SORTED,), I32), pltpu.SemaphoreType.DMA(())],
        compiler_params=_cparams(("arbitrary",)),
        name="inv",
    )(pos_flat, inv_init)


N_CHUNK = 256
GU_PIECES = D_EXPERT // N_CHUNK
DN_PIECES = D_MODEL // N_CHUNK
N_PIECES = GU_PIECES + DN_PIECES

PIECE_WORK = (D_MODEL * 2,) * GU_PIECES + (D_EXPERT,) * DN_PIECES
PIECE_ROWS = tuple(round(TILE_E * sum(PIECE_WORK[:p]) / sum(PIECE_WORK)) for p in range(N_PIECES + 1))


def _moe_kernel(te_ref, na_ref, nxt_ref, inv_ref, h2_hbm, wg_hbm, wu_hbm, wd_hbm, ys_hbm,
                xbuf0, xbuf1, xbuf2, ybuf0, ybuf1, ybuf2, xb, act, wg_f32, wu_f32, wd_f32,
                wg_scr, wu_scr, wd_scr, wset, gsem, ssem, wsem):
    i = pl.program_id(0)
    na = na_ref[0]
    active = i < na
    xbufs = (xbuf0, xbuf1, xbuf2)
    ybufs = (ybuf0, ybuf1, ybuf2)

    def weight_copies(e, p):
        return (pltpu.make_async_copy(wg_hbm.at[e], wg_f32.at[p], wsem.at[p]),
                pltpu.make_async_copy(wu_hbm.at[e], wu_f32.at[p], wsem.at[p]),
                pltpu.make_async_copy(wd_hbm.at[e], wd_f32.at[p], wsem.at[p]))

    def gather_row(tile, s, r):
        tok = inv_ref[tile * TILE_E + r] & TOKEN_MASK
        pltpu.make_async_copy(h2_hbm.at[pl.ds(tok, 1)], xbufs[s].at[pl.ds(r, 1)], gsem.at[s]).start()

    def scatter_row(dst, s, r):
        pltpu.make_async_copy(ybufs[s].at[pl.ds(r, 1)], ys_hbm.at[pl.ds(dst, 1)], ssem.at[s]).start()

    def wait_gather(s):
        pltpu.make_async_copy(h2_hbm.at[pl.ds(0, TILE_E)], xbufs[s], gsem.at[s]).wait()

    def wait_scatter(s):
        pltpu.make_async_copy(ybufs[s], ys_hbm.at[pl.ds(0, TILE_E)], ssem.at[s]).wait()

    @pl.when(i == 0)
    def _():
        zeros = ybufs[NBUF - 1]
        zeros[...] = jnp.zeros_like(zeros)
        for m in range(NBUF):
            cp = pltpu.make_async_copy(zeros, ys_hbm.at[pl.ds(TRASH_BASE + m * TILE_E, TILE_E)], ssem.at[0])
            cp.start()
            cp.wait()
        for t in range(2):
            def body(r, carry):
                gather_row(t, t, r)
                return carry

            lax.fori_loop(0, TILE_E, body, 0)

        wset[0] = 0
        for cp in weight_copies(te_ref[0], 0):
            cp.start()

    new_expert = jnp.logical_or(i == 0, te_ref[i] != te_ref[jnp.maximum(i - 1, 0)])

    @pl.when(jnp.logical_and(active, new_expert))
    def _():
        p = wset[0]
        for cp in weight_copies(0, p):
            cp.wait()
        for c in range(D_MODEL // CAST_ROWS):
            rows = pl.ds(c * CAST_ROWS, CAST_ROWS)
            wg_scr[rows, :] = wg_f32[p, rows, :].astype(BF16)
            wu_scr[rows, :] = wu_f32[p, rows, :].astype(BF16)
        wd_scr[...] = wd_f32[p].astype(BF16)
        nxt_tile = nxt_ref[i]

        @pl.when(nxt_tile < na)
        def _():
            for cp in weight_copies(te_ref[nxt_tile], 1 - p):
                cp.start()

        wset[0] = 1 - p

    def compute_tile(slot):
        s_next = (slot + 2) % NBUF
        s_prev = (slot - 1) % NBUF
        wait_gather(slot)

        @pl.when(i >= 2)
        def _():
            wait_scatter(slot)

        xb[...] = xbufs[slot][...].astype(BF16)
        nxt = jnp.minimum(i + 2, N_TILES_E - 1)
        prev = jnp.maximum(i - 1, 0)
        first = i == 0

        def issue(piece):
            for r in range(PIECE_ROWS[piece], PIECE_ROWS[piece + 1]):
                gather_row(nxt, s_next, r)
                dst = jnp.where(first, SPARE_BASE + r, inv_ref[prev * TILE_E + r] >> TOKEN_BITS)
                scatter_row(dst, s_prev, r)

        for c in range(GU_PIECES):
            cols = slice(c * N_CHUNK, (c + 1) * N_CHUNK)
            g = jnp.dot(xb[...], wg_scr[:, cols], preferred_element_type=F32)
            u = jnp.dot(xb[...], wu_scr[:, cols], preferred_element_type=F32)
            act[:, cols] = (_silu(g) * u).astype(BF16)
            issue(c)
        for c in range(DN_PIECES):
            cols = slice(c * N_CHUNK, (c + 1) * N_CHUNK)
            ybufs[slot][:, cols] = jnp.dot(act[...], wd_scr[:, cols], preferred_element_type=F32)
            issue(GU_PIECES + c)

    def drain(slot):
        s_last = (slot - 1) % NBUF

        def body(r, carry):
            scatter_row(inv_ref[(na - 1) * TILE_E + r] >> TOKEN_BITS, s_last, r)
            return carry

        lax.fori_loop(0, TILE_E, body, 0)
        wait_scatter(s_last)
        wait_scatter((slot - 2) % NBUF)

        @pl.when(na >= 2)
        def _():
            wait_scatter(slot)

        wait_gather(slot)
        wait_gather((slot + 1) % NBUF)

    for s in range(NBUF):
        @pl.when(jnp.logical_and(active, i % NBUF == s))
        def _():
            compute_tile(s)

        @pl.when(jnp.logical_and(i == na, i % NBUF == s))
        def _():
            drain(s)


def _moe(tile_expert, n_active, next_tile, inv, h2, wg, wu, wd):
    hbm = pl.BlockSpec(memory_space=pl.ANY)
    grid_spec = pltpu.PrefetchScalarGridSpec(
        num_scalar_prefetch=4,
        grid=(N_TILES_E + 1,),
        in_specs=[hbm, hbm, hbm, hbm],
        out_specs=hbm,
        scratch_shapes=[pltpu.VMEM((TILE_E, D_MODEL), F32)] * (2 * NBUF) + [
                        pltpu.VMEM((TILE_E, D_MODEL), BF16),
                        pltpu.VMEM((TILE_E, D_EXPERT), BF16),
                        pltpu.VMEM((2, D_MODEL, D_EXPERT), F32),
                        pltpu.VMEM((2, D_MODEL, D_EXPERT), F32),
                        pltpu.VMEM((2, D_EXPERT, D_MODEL), F32),
                        pltpu.VMEM((D_MODEL, D_EXPERT), BF16),
                        pltpu.VMEM((D_MODEL, D_EXPERT), BF16),
                        pltpu.VMEM((D_EXPERT, D_MODEL), BF16),
                        pltpu.SMEM((1,), I32),
                        pltpu.SemaphoreType.DMA((NBUF,)),
                        pltpu.SemaphoreType.DMA((NBUF,)),
                        pltpu.SemaphoreType.DMA((2,))],
    )
    return pl.pallas_call(
        _moe_kernel,
        grid_spec=grid_spec,
        out_shape=jax.ShapeDtypeStruct((YS_ROWS, D_MODEL), F32),
        compiler_params=_cparams(("arbitrary",)),
        name="moe",
    )(tile_expert, n_active, next_tile, inv, h2, wg, wu, wd)


def _combine_kernel(meta_ref, *refs):
    ys_refs = refs[:TOP_K]
    h2_ref, x1_ref, mod_ref, wsg_hbm, wsu_hbm, wsd_hbm = refs[TOP_K:TOP_K + 6]
    op_ref, os_ref, wsg_scr, wsu_scr, wsd_scr, stage_a, stage_b, wsem = refs[TOP_K + 6:]
    i = pl.program_id(0)

    @pl.when(i == 0)
    def _():
        _load_cast_weight(wsg_hbm, wsg_scr, stage_a, wsem)
        _load_cast_weight(wsu_hbm, wsu_scr, stage_a, wsem)
        _load_cast_weight(wsd_hbm, wsd_scr, stage_b, wsem)

    h = h2_ref[...].astype(BF16)
    sg = jnp.dot(h, wsg_scr[...], preferred_element_type=F32)
    su = jnp.dot(h, wsu_scr[...], preferred_element_type=F32)
    moe = jnp.dot((_silu(sg) * su).astype(BF16), wsd_scr[...], preferred_element_type=F32)
    for k in range(TOP_K):
        moe = moe + meta_ref[:, k:k + 1] * ys_refs[k][...]
    y = x1_ref[...] + mod_ref[0, 5:6, :] * moe

    @pl.when(i < T_P // TM_C)
    def _():
        op_ref[...] = y

    @pl.when(i >= T_P // TM_C)
    def _():
        os_ref[...] = y


def _combine(meta, ys, h2, x1, mods3, wsg, wsu, wsd):
    tiles_per_group = MOD_GROUP // TM_C
    row = lambda w: pl.BlockSpec((TM_C, w), lambda i: (i, 0))
    slot_rows = lambda k: pl.BlockSpec((TM_C, D_MODEL), lambda i: (k * (T_ALL // TM_C) + i, 0))
    osp, oss = _two_group_specs(TM_C, D_MODEL)
    hbm = pl.BlockSpec(memory_space=pl.ANY)
    return pl.pallas_call(
        _combine_kernel,
        grid=(T_ALL // TM_C,),
        in_specs=[row(128)] + [slot_rows(k) for k in range(TOP_K)] + [
                  row(D_MODEL), row(D_MODEL),
                  pl.BlockSpec((1, 6, D_MODEL), lambda i: (i // tiles_per_group, 0, 0)),
                  hbm, hbm, hbm],
        out_specs=[osp, oss],
        out_shape=[jax.ShapeDtypeStruct((T_P, D_MODEL), F32),
                   jax.ShapeDtypeStruct((T_S, D_MODEL), F32)],
        scratch_shapes=[pltpu.VMEM((D_MODEL, D_SHARED), BF16),
                        pltpu.VMEM((D_MODEL, D_SHARED), BF16),
                        pltpu.VMEM((D_SHARED, D_MODEL), BF16),
                        pltpu.VMEM((D_MODEL, D_SHARED), F32),
                        pltpu.VMEM((D_SHARED, D_MODEL), F32),
                        pltpu.SemaphoreType.DMA(())],
        compiler_params=_cparams(("arbitrary",)),
        name="combine",
    )(meta, *([ys] * TOP_K), h2, x1, mods3, wsg, wsu, wsd)


def _rope_tables():
    rows = DEC_SEQ // GRID_W
    row = jnp.repeat(jnp.arange(rows, dtype=F32), GRID_W)
    col = jnp.tile(jnp.arange(GRID_W, dtype=F32), rows)
    n_freq = HEAD_DIM // 4
    inv = ROPE_THETA ** (-jnp.arange(n_freq, dtype=F32) / n_freq)
    ang = jnp.concatenate([row[:, None] * inv, col[:, None] * inv], axis=-1)
    cos, sin = jnp.cos(ang), jnp.sin(ang)
    cos2 = jnp.concatenate([cos, cos], axis=-1)
    sin2 = jnp.concatenate([-sin, sin], axis=-1)
    cos_tab = jnp.concatenate([jnp.ones((TM_IN, HEAD_DIM), F32), cos2], axis=0)
    sin_tab = jnp.concatenate([jnp.zeros((TM_IN, HEAD_DIM), F32), sin2], axis=0)
    return cos_tab, sin_tab


def kernel(x_prompt, x_sample, cache_k, cache_v, c, c_ctx, w_ada, b_ada, norm_mix_g, norm_ffn_g, w_in, conv_w,
           q_norm_g, k_norm_g, attn_sink, w_conv_out, w_attn_out, w_out, router_w, router_bias, w_exp_gate,
           w_exp_up, w_exp_down, w_sh_gate, w_sh_up, w_sh_down):
    l = 0
    xp = x_prompt.reshape(T_P, D_MODEL)
    xs = x_sample.reshape(T_S, D_MODEL)

    cond = jnp.concatenate([c_ctx[None, :], c, jnp.zeros((8 - N_MOD, D_MODEL), F32)], axis=0)
    mods = _ada(cond.T, w_ada[l], b_ada[l][None, :])
    mods3 = mods[:N_MOD].reshape(N_MOD, 6, D_MODEL)

    h = _prenorm(xp, xs, mods3, norm_mix_g[l][None, :])
    cos_tab, sin_tab = _rope_tables()
    proj, k32, v32 = _inproj(h, w_in[l], q_norm_g[l][None, :], k_norm_g[l][None, :], cos_tab, sin_tab)

    sink = attn_sink[l]
    kc = cache_k[:, l].reshape(DEC_BATCH, PAST_LEN, KV_WIDTH)
    vc = cache_v[:, l].reshape(DEC_BATCH, PAST_LEN, KV_WIDTH)
    attn_p = _attn_p(sink, proj)
    attn_s = _attn_s(sink, proj, kc, vc)

    z1 = _conv(proj, conv_w[l], w_conv_out[l])

    wr_pad = jnp.pad(router_w[l], ((0, 0), (0, 128 - N_EXPERTS)))
    x1, h2, logits = _merge(xp, xs, attn_p, attn_s, z1, proj, mods3, norm_ffn_g[l][None, :], wr_pad,
                            w_attn_out[l], w_out[l])

    enc, cnt = _route1(logits, router_bias[l][:, None])
    meta, tmap = _route2(enc, cnt)
    pos_flat = meta[:, TOP_K:2 * TOP_K].astype(I32).reshape(T_ALL * TOP_K)
    tile_expert = tmap[0]
    n_active = tmap[1, :1]

    inv = _inv(pos_flat, _pad_codes())
    ys = _moe(tile_expert, n_active, tmap[2], inv, h2, w_exp_gate[l], w_exp_up[l], w_exp_down[l])
    y_p, y_s = _combine(meta, ys, h2, x1, mods3, w_sh_gate[l], w_sh_up[l], w_sh_down[l])

    y_prompt = y_p.reshape(BATCH, SEQ, D_MODEL)
    y_sample = y_s.reshape(DEC_BATCH, DEC_SEQ, D_MODEL)
    new_k = k32.reshape(BATCH, 1, SEQ, N_KV_HEADS, HEAD_DIM)
    new_v = v32.reshape(BATCH, 1, SEQ, N_KV_HEADS, HEAD_DIM)
    return (y_prompt, y_sample, new_k, new_v)
```

```python
import jax
import jax.numpy as jnp
from jax import lax
from jax.experimental import pallas as pl
from jax.experimental.pallas import tpu as pltpu

F32 = jnp.float32
BF16 = jnp.bfloat16
I32 = jnp.int32

D_MODEL = 2048
BATCH = 16
SEQ = 256
DEC_BATCH = 2
DEC_SEQ = 4096
PAST_LEN = 512
GRID_W = 64
N_HEADS = 16
N_KV_HEADS = 4
HEAD_DIM = 128
GROUP = N_HEADS // N_KV_HEADS
ATTN_WIDTH = N_HEADS * HEAD_DIM
KV_WIDTH = N_KV_HEADS * HEAD_DIM
WINDOW = 128
CONV_WIDTH = D_MODEL
N_EXPERTS = 64
TOP_K = 8
N_EXPERT_GROUPS = 8
GROUP_SIZE = N_EXPERTS // N_EXPERT_GROUPS
TOPK_GROUPS = 4
D_EXPERT = 512
D_SHARED = 512
ROUTED_SCALE = 2.5
ROPE_THETA = 10000.0
EPS = 1e-6
NEG_INF = -1e30
ATTN_SCALE = HEAD_DIM ** -0.5
LOG2E = 1.4426950408889634

T_P = BATCH * SEQ
T_S = DEC_BATCH * DEC_SEQ
T_ALL = T_P + T_S
MOD_GROUP = 4096
N_MOD = 1 + DEC_BATCH
assert T_P == MOD_GROUP and DEC_SEQ == MOD_GROUP

COL_CB, COL_CC, COL_CX, COL_Q, COL_GC, COL_GA = (i * D_MODEL for i in range(6))
COL_K = 6 * D_MODEL
COL_V = COL_K + KV_WIDTH
IN_WIDTH = COL_V + KV_WIDTH
W_COL_KV = 3 * CONV_WIDTH + ATTN_WIDTH

TM_IN = 1024
TN_IN = 1024
TM = 256
TM_PRE = 512
TM_C = 128
TILE_E = 256
N_SORTED = T_ALL * TOP_K + N_EXPERTS * TILE_E
N_TILES_E = N_SORTED // TILE_E
TMAP_W = 512
CAST_ROWS = 512
VMEM_LIMIT = 56 * 1024 * 1024


def _cparams(sem):
    return pltpu.CompilerParams(dimension_semantics=sem, vmem_limit_bytes=VMEM_LIMIT)


def _silu(x):
    return x * jax.nn.sigmoid(x)


def _load_cast_weight(w_hbm, w_scr, stage, sem):
    rows = stage.shape[0]
    for c in range(w_hbm.shape[0] // rows):
        cp = pltpu.make_async_copy(w_hbm.at[pl.ds(c * rows, rows)], stage, sem)
        cp.start()
        cp.wait()
        w_scr[pl.ds(c * rows, rows), :] = stage[...].astype(BF16)


ADA_TN = 1024
ADA_CHUNK = 256


def _ada_kernel(ct_ref, w_ref, b_ref, o_ref):
    tn = w_ref.shape[1]

    def body(c, accs):
        k0 = pl.multiple_of(c * ADA_CHUNK, ADA_CHUNK)
        wch = w_ref[pl.ds(k0, ADA_CHUNK), :]
        sch = _silu(ct_ref[pl.ds(k0, ADA_CHUNK), :])
        out = []
        for r in range(N_MOD):
            p = wch * sch[:, r:r + 1]
            out.append(accs[r] + p.reshape(ADA_CHUNK // 8, 8, tn).sum(axis=0))
        return tuple(out)

    accs = lax.fori_loop(0, D_MODEL // ADA_CHUNK, body,
                         tuple(jnp.zeros((8, tn), F32) for _ in range(N_MOD)))
    o_ref[...] = jnp.zeros_like(o_ref)
    for r in range(N_MOD):
        o_ref[r:r + 1, :] = jnp.sum(accs[r], axis=0, keepdims=True) + b_ref[...]


def _ada(cond_t, w_ada, b_ada):
    n = w_ada.shape[1]
    return pl.pallas_call(
        _ada_kernel,
        grid=(n // ADA_TN,),
        in_specs=[pl.BlockSpec((D_MODEL, 8), lambda j: (0, 0)),
                  pl.BlockSpec((D_MODEL, ADA_TN), lambda j: (0, j)),
                  pl.BlockSpec((1, ADA_TN), lambda j: (0, j))],
        out_specs=pl.BlockSpec((8, ADA_TN), lambda j: (0, j)),
        out_shape=jax.ShapeDtypeStruct((8, n), F32),
        compiler_params=_cparams(("arbitrary",)),
        name="ada",
    )(cond_t, w_ada, b_ada)


def _rms_mod(x, g, shift, scale):
    ms = jnp.mean(x * x, axis=-1, keepdims=True)
    y = x * lax.rsqrt(ms + EPS) * g
    return y * (1.0 + scale) + shift


def _two_group_specs(tile, width):
    n_ctx = T_P // tile
    return (pl.BlockSpec((tile, width), lambda i: (jnp.minimum(i, n_ctx - 1), 0)),
            pl.BlockSpec((tile, width), lambda i: (jnp.maximum(i - n_ctx, 0), 0)))


def _prenorm_kernel(xp_ref, xs_ref, mod_ref, g_ref, h_ref):
    i = pl.program_id(0)
    shift = mod_ref[0, 0:1, :]
    scale = mod_ref[0, 1:2, :]
    g = g_ref[...]

    @pl.when(i < T_P // TM_PRE)
    def _():
        h_ref[...] = _rms_mod(xp_ref[...], g, shift, scale).astype(BF16)

    @pl.when(i >= T_P // TM_PRE)
    def _():
        h_ref[...] = _rms_mod(xs_ref[...], g, shift, scale).astype(BF16)


def _prenorm(xp, xs, mods3, g_mix):
    tiles_per_group = MOD_GROUP // TM_PRE
    sp, ss = _two_group_specs(TM_PRE, D_MODEL)
    return pl.pallas_call(
        _prenorm_kernel,
        grid=(T_ALL // TM_PRE,),
        in_specs=[sp, ss,
                  pl.BlockSpec((1, 6, D_MODEL), lambda i: (i // tiles_per_group, 0, 0)),
                  pl.BlockSpec((1, D_MODEL), lambda i: (0, 0))],
        out_specs=pl.BlockSpec((TM_PRE, D_MODEL), lambda i: (i, 0)),
        out_shape=jax.ShapeDtypeStruct((T_ALL, D_MODEL), BF16),
        compiler_params=_cparams(("arbitrary",)),
        name="prenorm",
    )(xp, xs, mods3, g_mix)


def _head_norm_rope(a, g, cos2, sin2, scale):
    ms = jnp.mean(a * a, axis=-1, keepdims=True)
    a = a * lax.rsqrt(ms + EPS) * g
    a = a * cos2 + pltpu.roll(a, HEAD_DIM // 2, 1) * sin2
    return a * scale


N_J = IN_WIDTH // TN_IN
J_Q0 = COL_Q // TN_IN
J_Q1 = J_Q0 + ATTN_WIDTH // TN_IN
J_KV = COL_K // TN_IN
assert J_KV == N_J - 1
N_CTX_IN = T_P // TM_IN


def _inproj_kernel(h_ref, w_ref, qg_ref, kg_ref, cos_ref, sin_ref, o_ref, k_ref, v_ref, w_scr):
    j = pl.program_id(0)
    i = pl.program_id(1)

    @pl.when(i == 0)
    def _():
        for c in range(D_MODEL // CAST_ROWS):
            rows = pl.ds(c * CAST_ROWS, CAST_ROWS)
            w_scr[rows, :] = w_ref[rows, :].astype(BF16)

    acc = jnp.dot(h_ref[...], w_scr[...], preferred_element_type=F32)
    is_q = jnp.logical_and(j >= J_Q0, j < J_Q1)
    is_kv = j == J_KV

    @pl.when(jnp.logical_not(jnp.logical_or(is_q, is_kv)))
    def _():
        o_ref[...] = acc.astype(BF16)

    @pl.when(is_q)
    def _():
        cos2 = cos_ref[...]
        sin2 = sin_ref[...]
        g = qg_ref[...]
        for h in range(TN_IN // HEAD_DIM):
            cols = slice(h * HEAD_DIM, (h + 1) * HEAD_DIM)
            o_ref[:, cols] = _head_norm_rope(acc[:, cols], g, cos2, sin2, ATTN_SCALE * LOG2E).astype(BF16)

    @pl.when(is_kv)
    def _():
        cos2 = cos_ref[...]
        sin2 = sin_ref[...]
        g = kg_ref[...]
        is_ctx = i < N_CTX_IN
        for h in range(N_KV_HEADS):
            cols = slice(h * HEAD_DIM, (h + 1) * HEAD_DIM)
            kh = _head_norm_rope(acc[:, cols], g, cos2, sin2, 1.0)
            o_ref[:, cols] = kh.astype(BF16)

            @pl.when(is_ctx)
            def _():
                k_ref[:, cols] = kh

        v = acc[:, KV_WIDTH:]
        o_ref[:, KV_WIDTH:] = v.astype(BF16)

        @pl.when(is_ctx)
        def _():
            v_ref[...] = v


def _inproj(h, w_in, qg, kg, cos_tab, sin_tab):
    pos_tiles = DEC_SEQ // TM_IN

    def tab_map(j, i):
        return (jnp.where(i < N_CTX_IN, 0, 1 + i % pos_tiles), 0)

    def w_map(j, i):
        return (0, jnp.where(j < W_COL_KV // TN_IN, j, jnp.where(j == J_KV, W_COL_KV // TN_IN, j + 1)))

    def kv_map(j, i):
        return (jnp.where(j == J_KV, jnp.minimum(i, N_CTX_IN - 1), 0), 0)

    return pl.pallas_call(
        _inproj_kernel,
        grid=(N_J, T_ALL // TM_IN),
        in_specs=[pl.BlockSpec((TM_IN, D_MODEL), lambda j, i: (i, 0)),
                  pl.BlockSpec((D_MODEL, TN_IN), w_map, pipeline_mode=pl.Buffered(1)),
                  pl.BlockSpec((1, HEAD_DIM), lambda j, i: (0, 0)),
                  pl.BlockSpec((1, HEAD_DIM), lambda j, i: (0, 0)),
                  pl.BlockSpec((TM_IN, HEAD_DIM), tab_map),
                  pl.BlockSpec((TM_IN, HEAD_DIM), tab_map)],
        out_specs=[pl.BlockSpec((TM_IN, TN_IN), lambda j, i: (i, j)),
                   pl.BlockSpec((TM_IN, KV_WIDTH), kv_map),
                   pl.BlockSpec((TM_IN, KV_WIDTH), kv_map)],
        out_shape=[jax.ShapeDtypeStruct((T_ALL, IN_WIDTH), BF16),
                   jax.ShapeDtypeStruct((T_P, KV_WIDTH), F32),
                   jax.ShapeDtypeStruct((T_P, KV_WIDTH), F32)],
        scratch_shapes=[pltpu.VMEM((D_MODEL, TN_IN), BF16)],
        compiler_params=_cparams(("arbitrary", "arbitrary")),
        name="inproj",
    )(h, w_in, qg, kg, cos_tab, sin_tab)


def _dot_nt(a, b):
    return lax.dot_general(a, b, (((1,), (1,)), ((), ())), preferred_element_type=F32)


def _attn_p_kernel(sink_ref, q_ref, k_ref, v_ref, o_ref):
    for kh in range(N_KV_HEADS):
        kcols = slice(kh * HEAD_DIM, (kh + 1) * HEAD_DIM)
        k = k_ref[:, kcols]
        v = v_ref[:, kcols]
        for gi in range(GROUP):
            h = kh * GROUP + gi
            cols = slice(h * HEAD_DIM, (h + 1) * HEAD_DIM)
            s = _dot_nt(q_ref[:, cols], k)
            sk = sink_ref[h] * LOG2E
            m = jnp.maximum(jnp.max(s, axis=-1, keepdims=True), sk)
            p = jnp.exp2(s - m)
            den = jnp.sum(p, axis=-1, keepdims=True) + jnp.exp2(sk - m)
            o = jnp.dot(p.astype(BF16), v, preferred_element_type=F32) / den
            o_ref[:, cols] = o.astype(BF16)


def _attn_p(sink, proj):
    return pl.pallas_call(
        _attn_p_kernel,
        grid=(BATCH,),
        in_specs=[pl.BlockSpec(memory_space=pltpu.SMEM),
                  pl.BlockSpec((SEQ, ATTN_WIDTH), lambda b: (b, COL_Q // ATTN_WIDTH)),
                  pl.BlockSpec((SEQ, KV_WIDTH), lambda b: (b, COL_K // KV_WIDTH)),
                  pl.BlockSpec((SEQ, KV_WIDTH), lambda b: (b, COL_V // KV_WIDTH))],
        out_specs=pl.BlockSpec((SEQ, ATTN_WIDTH), lambda b: (b, 0)),
        out_shape=jax.ShapeDtypeStruct((T_P, ATTN_WIDTH), BF16),
        compiler_params=_cparams(("arbitrary",)),
        name="attn_p",
    )(sink, proj, proj, proj)


QB = 256
BAND = QB + 2 * WINDOW


def _attn_s_kernel(sink_ref, q_ref, k_ref, v_ref, kc_ref, vc_ref, o_ref):
    i = pl.program_id(1)
    start = jnp.clip(i * QB - WINDOW, 0, DEC_SEQ - BAND)
    start = pl.multiple_of(start, WINDOW)
    qpos = i * QB + lax.broadcasted_iota(I32, (QB, BAND), 0)
    kpos = start + lax.broadcasted_iota(I32, (QB, BAND), 1)
    valid = jnp.abs(qpos - kpos) <= WINDOW
    for kh in range(N_KV_HEADS):
        kcols = slice(kh * HEAD_DIM, (kh + 1) * HEAD_DIM)
        kb = k_ref[pl.ds(start, BAND), kcols]
        vb = v_ref[pl.ds(start, BAND), kcols]
        kc = kc_ref[0, :, kcols].astype(BF16)
        vc = vc_ref[0, :, kcols].astype(BF16)
        for gi in range(GROUP):
            h = kh * GROUP + gi
            cols = slice(h * HEAD_DIM, (h + 1) * HEAD_DIM)
            q = q_ref[:, cols]
            s_loc = jnp.where(valid, _dot_nt(q, kb), NEG_INF)
            s_ctx = _dot_nt(q, kc)
            sk = sink_ref[h] * LOG2E
            m = jnp.maximum(jnp.maximum(jnp.max(s_loc, axis=-1, keepdims=True),
                                        jnp.max(s_ctx, axis=-1, keepdims=True)), sk)
            p_loc = jnp.exp2(s_loc - m)
            p_ctx = jnp.exp2(s_ctx - m)
            den = (jnp.sum(p_loc, axis=-1, keepdims=True) + jnp.sum(p_ctx, axis=-1, keepdims=True)
                   + jnp.exp2(sk - m))
            o = (jnp.dot(p_loc.astype(BF16), vb, preferred_element_type=F32)
                 + jnp.dot(p_ctx.astype(BF16), vc, preferred_element_type=F32)) / den
            o_ref[:, cols] = o.astype(BF16)


def _attn_s(sink, proj, kc, vc):
    nq = DEC_SEQ // QB
    first = T_P // QB
    seq_blk = T_P // DEC_SEQ
    return pl.pallas_call(
        _attn_s_kernel,
        grid=(DEC_BATCH, nq),
        in_specs=[pl.BlockSpec(memory_space=pltpu.SMEM),
                  pl.BlockSpec((QB, ATTN_WIDTH), lambda b, i: (first + b * nq + i, COL_Q // ATTN_WIDTH)),
                  pl.BlockSpec((DEC_SEQ, KV_WIDTH), lambda b, i: (seq_blk + b, COL_K // KV_WIDTH)),
                  pl.BlockSpec((DEC_SEQ, KV_WIDTH), lambda b, i: (seq_blk + b, COL_V // KV_WIDTH)),
                  pl.BlockSpec((1, PAST_LEN, KV_WIDTH), lambda b, i: (b, 0, 0)),
                  pl.BlockSpec((1, PAST_LEN, KV_WIDTH), lambda b, i: (b, 0, 0))],
        out_specs=pl.BlockSpec((QB, ATTN_WIDTH), lambda b, i: (b * nq + i, 0)),
        out_shape=jax.ShapeDtypeStruct((T_S, ATTN_WIDTH), BF16),
        compiler_params=_cparams(("arbitrary", "arbitrary")),
        name="attn_s",
    )(sink, proj, proj, proj, kc, vc)


HALO = 16


def _conv_kernel(cb_ref, cc_ref, cx_ref, ccp_ref, cxp_ref, ccn_ref, cxn_ref, gc_ref, cw_ref, w_hbm, o_ref,
                 w_scr, stage, sem):
    i = pl.program_id(0)

    @pl.when(i == 0)
    def _():
        _load_cast_weight(w_hbm, w_scr, stage, sem)

    n_ctx_tiles = T_P // TM
    per_seq = DEC_SEQ // TM
    is_first = jnp.logical_or(i < n_ctx_tiles, (i - n_ctx_tiles) % per_seq == 0)
    is_last = jnp.logical_or(i < n_ctx_tiles, (i - n_ctx_tiles) % per_seq == per_seq - 1)
    p = cc_ref[...].astype(F32) * cx_ref[...].astype(F32)
    prev_row = ccp_ref[HALO - 1:HALO, :].astype(F32) * cxp_ref[HALO - 1:HALO, :].astype(F32)
    next_row = ccn_ref[0:1, :].astype(F32) * cxn_ref[0:1, :].astype(F32)
    prev_row = jnp.where(is_first, 0.0, prev_row)
    next_row = jnp.where(is_last, 0.0, next_row)
    rows = lax.broadcasted_iota(I32, (TM, 1), 0)
    p_prev = jnp.where(rows == 0, prev_row, pltpu.roll(p, 1, 0))
    p_next = jnp.where(rows == TM - 1, next_row, pltpu.roll(p, TM - 1, 0))
    conv = p_prev * cw_ref[0:1, :] + p * cw_ref[1:2, :] + p_next * cw_ref[2:3, :]
    u = cb_ref[...].astype(F32) * conv
    y = jnp.dot(u.astype(BF16), w_scr[...], preferred_element_type=F32)
    o_ref[...] = (jax.nn.sigmoid(gc_ref[...].astype(F32)) * y).astype(BF16)


def _conv(proj, conv_w, w_conv_out):
    hb = TM // HALO
    last_hb = T_ALL // HALO - 1
    wide = lambda c: pl.BlockSpec((TM, D_MODEL), lambda i: (i, c // D_MODEL))
    prev = lambda c: pl.BlockSpec((HALO, D_MODEL), lambda i: (jnp.maximum(i * hb - 1, 0), c // D_MODEL))
    nxt = lambda c: pl.BlockSpec((HALO, D_MODEL), lambda i: (jnp.minimum((i + 1) * hb, last_hb), c // D_MODEL))
    return pl.pallas_call(
        _conv_kernel,
        grid=(T_ALL // TM,),
        in_specs=[wide(COL_CB), wide(COL_CC), wide(COL_CX),
                  prev(COL_CC), prev(COL_CX), nxt(COL_CC), nxt(COL_CX),
                  wide(COL_GC),
                  pl.BlockSpec((3, CONV_WIDTH), lambda i: (0, 0)),
                  pl.BlockSpec(memory_space=pl.ANY)],
        out_specs=pl.BlockSpec((TM, D_MODEL), lambda i: (i, 0)),
        out_shape=jax.ShapeDtypeStruct((T_ALL, D_MODEL), BF16),
        scratch_shapes=[pltpu.VMEM((CONV_WIDTH, D_MODEL), BF16),
                        pltpu.VMEM((CAST_ROWS, D_MODEL), F32),
                        pltpu.SemaphoreType.DMA(())],
        compiler_params=_cparams(("arbitrary",)),
        name="conv",
    )(proj, proj, proj, proj, proj, proj, proj, proj, conv_w, w_conv_out)


def _merge_kernel(xp_ref, xs_ref, attn_p_ref, attn_s_ref, z1_ref, ga_ref, mod_ref, g2_ref, wr_ref,
                  wao_hbm, wo_hbm, x1_ref, h2_ref, lg_ref, wao_scr, wo_scr, stage, sem):
    i = pl.program_id(0)

    @pl.when(i == 0)
    def _():
        _load_cast_weight(wao_hbm, wao_scr, stage, sem)
        _load_cast_weight(wo_hbm, wo_scr, stage, sem)

    is_ctx = i < T_P // TM
    attn = jnp.where(is_ctx, attn_p_ref[...], attn_s_ref[...])
    x = jnp.where(is_ctx, xp_ref[...], xs_ref[...])
    ya = jnp.dot(attn, wao_scr[...], preferred_element_type=F32)
    z = z1_ref[...].astype(F32) + jax.nn.sigmoid(ga_ref[...].astype(F32)) * ya
    mix = jnp.dot(z.astype(BF16), wo_scr[...], preferred_element_type=F32)
    x1 = x + mod_ref[0, 2:3, :] * mix
    x1_ref[...] = x1
    h = _rms_mod(x1, g2_ref[...], mod_ref[0, 3:4, :], mod_ref[0, 4:5, :])
    h2_ref[...] = h
    h_hi = h.astype(BF16)
    h_lo = (h - h_hi.astype(F32)).astype(BF16)
    wr = wr_ref[...]
    wr_hi32 = wr.astype(BF16).astype(F32)
    wr_lo32 = (wr - wr_hi32).astype(BF16).astype(F32)
    w_both = (wr_hi32 + pltpu.roll(wr_lo32, N_EXPERTS, 1)).astype(BF16)
    both = jnp.dot(h_hi, w_both, preferred_element_type=F32)
    lg_ref[...] = (both + pltpu.roll(both, N_EXPERTS, 1)
                   + jnp.dot(h_lo, wr_hi32.astype(BF16), preferred_element_type=F32))


def _merge(xp, xs, attn_p, attn_s, z1, proj, mods3, g_ffn, wr_pad, w_attn_out, w_out):
    tiles_per_group = MOD_GROUP // TM
    const = lambda shape: pl.BlockSpec(shape, lambda i: (0, 0))
    row = lambda w: pl.BlockSpec((TM, w), lambda i: (i, 0))
    xsp, xss = _two_group_specs(TM, D_MODEL)
    asp, ass = _two_group_specs(TM, ATTN_WIDTH)
    return pl.pallas_call(
        _merge_kernel,
        grid=(T_ALL // TM,),
        in_specs=[xsp, xss, asp, ass, row(D_MODEL),
                  pl.BlockSpec((TM, D_MODEL), lambda i: (i, COL_GA // D_MODEL)),
                  pl.BlockSpec((1, 6, D_MODEL), lambda i: (i // tiles_per_group, 0, 0)),
                  const((1, D_MODEL)), const((D_MODEL, 128)),
                  pl.BlockSpec(memory_space=pl.ANY), pl.BlockSpec(memory_space=pl.ANY)],
        out_specs=[row(D_MODEL), row(D_MODEL), row(128)],
        out_shape=[jax.ShapeDtypeStruct((T_ALL, D_MODEL), F32),
                   jax.ShapeDtypeStruct((T_ALL, D_MODEL), F32),
                   jax.ShapeDtypeStruct((T_ALL, 128), F32)],
        scratch_shapes=[pltpu.VMEM((ATTN_WIDTH, D_MODEL), BF16),
                        pltpu.VMEM((D_MODEL, D_MODEL), BF16),
                        pltpu.VMEM((CAST_ROWS, D_MODEL), F32),
                        pltpu.SemaphoreType.DMA(())],
        compiler_params=_cparams(("arbitrary",)),
        name="merge",
    )(xp, xs, attn_p, attn_s, z1, proj, mods3, g_ffn, wr_pad, w_attn_out, w_out)


def _route1_kernel(lg_ref, bias_ref, enc_ref, cnt_ref):
    i = pl.program_id(0)
    lt = lg_ref[...].T[:N_EXPERTS, :]
    scores = jax.nn.sigmoid(lt)
    biased = scores + bias_ref[...]
    b3 = biased.reshape(N_EXPERT_GROUPS, GROUP_SIZE, TM)
    mi = lax.broadcasted_iota(I32, b3.shape, 1)
    m1 = jnp.max(b3, axis=1, keepdims=True)
    idx1 = jnp.min(jnp.where(b3 == m1, mi, GROUP_SIZE), axis=1, keepdims=True)
    m2 = jnp.max(jnp.where(mi == idx1, -jnp.inf, b3), axis=1, keepdims=True)
    gs = (m1 + m2).reshape(N_EXPERT_GROUPS, TM)
    gidx = lax.broadcasted_iota(I32, gs.shape, 0)
    grank = jnp.zeros(gs.shape, I32)
    for j in range(N_EXPERT_GROUPS):
        gj = gs[j:j + 1, :]
        beats = jnp.logical_or(gj > gs, jnp.logical_and(gj == gs, j < gidx))
        grank = grank + beats.astype(I32)
    gsel = grank < TOPK_GROUPS
    emask = jnp.broadcast_to(gsel[:, None, :], b3.shape).reshape(N_EXPERTS, TM)
    masked = jnp.where(emask, biased, NEG_INF)
    eidx = lax.broadcasted_iota(I32, masked.shape, 0)
    erank = jnp.zeros(masked.shape, I32)
    for j in range(N_EXPERTS):
        vj = masked[j:j + 1, :]
        beats = jnp.logical_or(vj > masked, jnp.logical_and(vj == masked, j < eidx))
        erank = erank + beats.astype(I32)
    sel = erank < TOP_K
    wsel = jnp.where(sel, scores, 0.0)
    den = jnp.sum(wsel, axis=0, keepdims=True)
    wts = wsel / den * ROUTED_SCALE
    enc_ref[...] = jnp.where(sel, wts, -1.0)

    @pl.when(i == 0)
    def _():
        cnt_ref[...] = jnp.zeros_like(cnt_ref)

    cnt = jnp.sum(sel.astype(F32), axis=1, keepdims=True)
    cnt_ref[...] += jnp.broadcast_to(cnt, cnt_ref.shape)


def _route1(logits, bias_col):
    return pl.pallas_call(
        _route1_kernel,
        grid=(T_ALL // TM,),
        in_specs=[pl.BlockSpec((TM, 128), lambda i: (i, 0)),
                  pl.BlockSpec((N_EXPERTS, 1), lambda i: (0, 0))],
        out_specs=[pl.BlockSpec((N_EXPERTS, TM), lambda i: (0, i)),
                   pl.BlockSpec((N_EXPERTS, 128), lambda i: (0, 0))],
        out_shape=[jax.ShapeDtypeStruct((N_EXPERTS, T_ALL), F32),
                   jax.ShapeDtypeStruct((N_EXPERTS, 128), F32)],
        compiler_params=_cparams(("arbitrary",)),
        name="route1",
    )(logits, bias_col)


def _route2_kernel(enc_ref, cnt_ref, meta_ref, tmap_ref, carry_ref):
    i = pl.program_id(0)

    @pl.when(i == 0)
    def _():
        carry_ref[...] = jnp.zeros_like(carry_ref)

    enc = enc_ref[...]
    sel = enc >= 0.0
    wts = jnp.maximum(enc, 0.0)
    sel_b = sel.astype(BF16)
    ntile = jnp.ceil(cnt_ref[...] * (1.0 / TILE_E))
    er = lax.broadcasted_iota(I32, (N_EXPERTS, N_EXPERTS), 0)
    ec = lax.broadcasted_iota(I32, (N_EXPERTS, N_EXPERTS), 1)
    lower = (ec < er).astype(BF16)
    off_t = jnp.dot(lower, ntile.astype(BF16), preferred_element_type=F32)
    tr = lax.broadcasted_iota(I32, (TM, TM), 0)
    tc = lax.broadcasted_iota(I32, (TM, TM), 1)
    upper = (tr < tc).astype(BF16)
    rank = jnp.dot(sel_b, upper, preferred_element_type=F32) + carry_ref[:, 0:1]
    carry_ref[...] += jnp.broadcast_to(jnp.sum(sel.astype(F32), axis=1, keepdims=True), carry_ref.shape)
    pos = off_t[:, 0:1] * float(TILE_E) + rank
    slot = jnp.dot(lower, sel_b, preferred_element_type=F32)
    rows = []
    for k in range(TOP_K):
        mk = jnp.logical_and(sel, slot == float(k))
        rows.append(jnp.sum(jnp.where(mk, wts, 0.0), axis=0, keepdims=True))
    for k in range(TOP_K):
        mk = jnp.logical_and(sel, slot == float(k))
        rows.append(jnp.sum(jnp.where(mk, pos, 0.0), axis=0, keepdims=True))
    rows.append(jnp.zeros((128 - 2 * TOP_K, TM), F32))
    meta_ref[...] = jnp.concatenate(rows, axis=0).T
    nt = ntile[:, 0:1]
    end_t = off_t[:, 0:1] + nt
    lane = lax.broadcasted_iota(I32, (N_EXPERTS, TMAP_W), 1)
    te = jnp.sum((end_t <= lane.astype(F32)).astype(F32), axis=0, keepdims=True)
    te = jnp.minimum(te, float(N_EXPERTS - 1))
    nact = jnp.sum(nt, axis=0, keepdims=True)
    erow = lax.broadcasted_iota(I32, (N_EXPERTS, TMAP_W), 0).astype(F32)
    nxt = jnp.sum(jnp.where(erow == te, end_t, 0.0), axis=0, keepdims=True)
    tmap_ref[...] = jnp.zeros_like(tmap_ref)
    tmap_ref[0:1, :] = te.astype(I32)
    tmap_ref[1:2, :] = jnp.broadcast_to(nact, (1, TMAP_W)).astype(I32)
    tmap_ref[2:3, :] = nxt.astype(I32)


def _route2(enc, cnt):
    return pl.pallas_call(
        _route2_kernel,
        grid=(T_ALL // TM,),
        in_specs=[pl.BlockSpec((N_EXPERTS, TM), lambda i: (0, i)),
                  pl.BlockSpec((N_EXPERTS, 128), lambda i: (0, 0))],
        out_specs=[pl.BlockSpec((TM, 128), lambda i: (i, 0)),
                   pl.BlockSpec((8, TMAP_W), lambda i: (0, 0))],
        out_shape=[jax.ShapeDtypeStruct((T_ALL, 128), F32),
                   jax.ShapeDtypeStruct((8, TMAP_W), I32)],
        scratch_shapes=[pltpu.VMEM((N_EXPERTS, 128), F32)],
        compiler_params=_cparams(("arbitrary",)),
        name="route2",
    )(enc, cnt)


TOKEN_BITS = 14
TOKEN_MASK = (1 << TOKEN_BITS) - 1
NBUF = 3
TRASH_BASE = TOP_K * T_ALL
SPARE_BASE = TRASH_BASE + NBUF * TILE_E
YS_ROWS = SPARE_BASE + TILE_E
assert T_ALL <= 1 << TOKEN_BITS and YS_ROWS << TOKEN_BITS < 1 << 31


def _pad_codes():
    r = jnp.arange(N_SORTED, dtype=I32)
    q = r % TILE_E
    out_row = TRASH_BASE + ((r // TILE_E) % NBUF) * TILE_E + q
    return (out_row << TOKEN_BITS) | q


INV_UNROLL = 4


def _inv_kernel(pos_ref, init_hbm, inv_hbm, inv_smem, sem):
    i = pl.program_id(0)

    @pl.when(i == 0)
    def _():
        cp = pltpu.make_async_copy(init_hbm, inv_smem, sem)
        cp.start()
        cp.wait()

    per_token = (1 << TOKEN_BITS) + 1
    per_slot = T_ALL << TOKEN_BITS

    def body(t4, base):
        for u in range(INV_UNROLL):
            for k in range(TOP_K):
                inv_smem[pos_ref[(t4 * INV_UNROLL + u) * TOP_K + k]] = base + (u * per_token + k * per_slot)
        return base + INV_UNROLL * per_token

    lax.fori_loop(0, TM // INV_UNROLL, body, i * (TM * per_token))

    @pl.when(i == pl.num_programs(0) - 1)
    def _():
        cp = pltpu.make_async_copy(inv_smem, inv_hbm, sem)
        cp.start()
        cp.wait()


def _inv(pos_flat, inv_init):
    return pl.pallas_call(
        _inv_kernel,
        grid=(T_ALL // TM,),
        in_specs=[pl.BlockSpec((TM * TOP_K,), lambda i: (i,), memory_space=pltpu.SMEM),
                  pl.BlockSpec(memory_space=pl.ANY)],
        out_specs=pl.BlockSpec(memory_space=pl.ANY),
        out_shape=jax.ShapeDtypeStruct((N_SORTED,), I32),
        scratch_shapes=[pltpu.SMEM((N_SORTED,), I32), pltpu.SemaphoreType.DMA(())],
        compiler_params=_cparams(("arbitrary",)),
        name="inv",
    )(pos_flat, inv_init)


N_CHUNK = 256
GU_PIECES = D_EXPERT // N_CHUNK
DN_PIECES = D_MODEL // N_CHUNK
N_PIECES = GU_PIECES + DN_PIECES

PIECE_WORK = (D_MODEL * 2,) * GU_PIECES + (D_EXPERT,) * DN_PIECES
PIECE_ROWS = tuple(round(TILE_E * sum(PIECE_WORK[:p]) / sum(PIECE_WORK)) for p in range(N_PIECES + 1))


def _moe_kernel(te_ref, na_ref, nxt_ref, inv_ref, h2_hbm, wg_hbm, wu_hbm, wd_hbm, ys_hbm,
                xbuf0, xbuf1, xbuf2, ybuf0, ybuf1, ybuf2, xb, act, wg_f32, wu_f32, wd_f32,
                wg_scr, wu_scr, wd_scr, wset, gsem, ssem, wsem):
    i = pl.program_id(0)
    na = na_ref[0]
    active = i < na
    xbufs = (xbuf0, xbuf1, xbuf2)
    ybufs = (ybuf0, ybuf1, ybuf2)

    def weight_copies(e, p):
        return (pltpu.make_async_copy(wg_hbm.at[e], wg_f32.at[p], wsem.at[p]),
                pltpu.make_async_copy(wu_hbm.at[e], wu_f32.at[p], wsem.at[p]),
                pltpu.make_async_copy(wd_hbm.at[e], wd_f32.at[p], wsem.at[p]))

    def gather_row(tile, s, r):
        tok = inv_ref[tile * TILE_E + r] & TOKEN_MASK
        pltpu.make_async_copy(h2_hbm.at[pl.ds(tok, 1)], xbufs[s].at[pl.ds(r, 1)], gsem.at[s]).start()

    def scatter_row(dst, s, r):
        pltpu.make_async_copy(ybufs[s].at[pl.ds(r, 1)], ys_hbm.at[pl.ds(dst, 1)], ssem.at[s]).start()

    def wait_gather(s):
        pltpu.make_async_copy(h2_hbm.at[pl.ds(0, TILE_E)], xbufs[s], gsem.at[s]).wait()

    def wait_scatter(s):
        pltpu.make_async_copy(ybufs[s], ys_hbm.at[pl.ds(0, TILE_E)], ssem.at[s]).wait()

    @pl.when(i == 0)
    def _():
        zeros = ybufs[NBUF - 1]
        zeros[...] = jnp.zeros_like(zeros)
        for m in range(NBUF):
            cp = pltpu.make_async_copy(zeros, ys_hbm.at[pl.ds(TRASH_BASE + m * TILE_E, TILE_E)], ssem.at[0])
            cp.start()
            cp.wait()
        for t in range(2):
            def body(r, carry):
                gather_row(t, t, r)
                return carry

            lax.fori_loop(0, TILE_E, body, 0)

        wset[0] = 0
        for cp in weight_copies(te_ref[0], 0):
            cp.start()

    new_expert = jnp.logical_or(i == 0, te_ref[i] != te_ref[jnp.maximum(i - 1, 0)])

    @pl.when(jnp.logical_and(active, new_expert))
    def _():
        p = wset[0]
        for cp in weight_copies(0, p):
            cp.wait()
        for c in range(D_MODEL // CAST_ROWS):
            rows = pl.ds(c * CAST_ROWS, CAST_ROWS)
            wg_scr[rows, :] = wg_f32[p, rows, :].astype(BF16)
            wu_scr[rows, :] = wu_f32[p, rows, :].astype(BF16)
        wd_scr[...] = wd_f32[p].astype(BF16)
        nxt_tile = nxt_ref[i]

        @pl.when(nxt_tile < na)
        def _():
            for cp in weight_copies(te_ref[nxt_tile], 1 - p):
                cp.start()

        wset[0] = 1 - p

    def compute_tile(slot):
        s_next = (slot + 2) % NBUF
        s_prev = (slot - 1) % NBUF
        wait_gather(slot)

        @pl.when(i >= 2)
        def _():
            wait_scatter(slot)

        xb[...] = xbufs[slot][...].astype(BF16)
        nxt = jnp.minimum(i + 2, N_TILES_E - 1)
        prev = jnp.maximum(i - 1, 0)
        first = i == 0

        def issue(piece):
            for r in range(PIECE_ROWS[piece], PIECE_ROWS[piece + 1]):
                gather_row(nxt, s_next, r)
                dst = jnp.where(first, SPARE_BASE + r, inv_ref[prev * TILE_E + r] >> TOKEN_BITS)
                scatter_row(dst, s_prev, r)

        for c in range(GU_PIECES):
            cols = slice(c * N_CHUNK, (c + 1) * N_CHUNK)
            g = jnp.dot(xb[...], wg_scr[:, cols], preferred_element_type=F32)
            u = jnp.dot(xb[...], wu_scr[:, cols], preferred_element_type=F32)
            act[:, cols] = (_silu(g) * u).astype(BF16)
            issue(c)
        for c in range(DN_PIECES):
            cols = slice(c * N_CHUNK, (c + 1) * N_CHUNK)
            ybufs[slot][:, cols] = jnp.dot(act[...], wd_scr[:, cols], preferred_element_type=F32)
            issue(GU_PIECES + c)

    def drain(slot):
        s_last = (slot - 1) % NBUF

        def body(r, carry):
            scatter_row(inv_ref[(na - 1) * TILE_E + r] >> TOKEN_BITS, s_last, r)
            return carry

        lax.fori_loop(0, TILE_E, body, 0)
        wait_scatter(s_last)
        wait_scatter((slot - 2) % NBUF)

        @pl.when(na >= 2)
        def _():
            wait_scatter(slot)

        wait_gather(slot)
        wait_gather((slot + 1) % NBUF)

    for s in range(NBUF):
        @pl.when(jnp.logical_and(active, i % NBUF == s))
        def _():
            compute_tile(s)

        @pl.when(jnp.logical_and(i == na, i % NBUF == s))
        def _():
            drain(s)


def _moe(tile_expert, n_active, next_tile, inv, h2, wg, wu, wd):
    hbm = pl.BlockSpec(memory_space=pl.ANY)
    grid_spec = pltpu.PrefetchScalarGridSpec(
        num_scalar_prefetch=4,
        grid=(N_TILES_E + 1,),
        in_specs=[hbm, hbm, hbm, hbm],
        out_specs=hbm,
        scratch_shapes=[pltpu.VMEM((TILE_E, D_MODEL), F32)] * (2 * NBUF) + [
                        pltpu.VMEM((TILE_E, D_MODEL), BF16),
                        pltpu.VMEM((TILE_E, D_EXPERT), BF16),
                        pltpu.VMEM((2, D_MODEL, D_EXPERT), F32),
                        pltpu.VMEM((2, D_MODEL, D_EXPERT), F32),
                        pltpu.VMEM((2, D_EXPERT, D_MODEL), F32),
                        pltpu.VMEM((D_MODEL, D_EXPERT), BF16),
                        pltpu.VMEM((D_MODEL, D_EXPERT), BF16),
                        pltpu.VMEM((D_EXPERT, D_MODEL), BF16),
                        pltpu.SMEM((1,), I32),
                        pltpu.SemaphoreType.DMA((NBUF,)),
                        pltpu.SemaphoreType.DMA((NBUF,)),
                        pltpu.SemaphoreType.DMA((2,))],
    )
    return pl.pallas_call(
        _moe_kernel,
        grid_spec=grid_spec,
        out_shape=jax.ShapeDtypeStruct((YS_ROWS, D_MODEL), F32),
        compiler_params=_cparams(("arbitrary",)),
        name="moe",
    )(tile_expert, n_active, next_tile, inv, h2, wg, wu, wd)


def _combine_kernel(meta_ref, *refs):
    ys_refs = refs[:TOP_K]
    h2_ref, x1_ref, mod_ref, wsg_hbm, wsu_hbm, wsd_hbm = refs[TOP_K:TOP_K + 6]
    op_ref, os_ref, wsg_scr, wsu_scr, wsd_scr, stage_a, stage_b, wsem = refs[TOP_K + 6:]
    i = pl.program_id(0)

    @pl.when(i == 0)
    def _():
        _load_cast_weight(wsg_hbm, wsg_scr, stage_a, wsem)
        _load_cast_weight(wsu_hbm, wsu_scr, stage_a, wsem)
        _load_cast_weight(wsd_hbm, wsd_scr, stage_b, wsem)

    h = h2_ref[...].astype(BF16)
    sg = jnp.dot(h, wsg_scr[...], preferred_element_type=F32)
    su = jnp.dot(h, wsu_scr[...], preferred_element_type=F32)
    moe = jnp.dot((_silu(sg) * su).astype(BF16), wsd_scr[...], preferred_element_type=F32)
    for k in range(TOP_K):
        moe = moe + meta_ref[:, k:k + 1] * ys_refs[k][...]
    y = x1_ref[...] + mod_ref[0, 5:6, :] * moe

    @pl.when(i < T_P // TM_C)
    def _():
        op_ref[...] = y

    @pl.when(i >= T_P // TM_C)
    def _():
        os_ref[...] = y


def _combine(meta, ys, h2, x1, mods3, wsg, wsu, wsd):
    tiles_per_group = MOD_GROUP // TM_C
    row = lambda w: pl.BlockSpec((TM_C, w), lambda i: (i, 0))
    slot_rows = lambda k: pl.BlockSpec((TM_C, D_MODEL), lambda i: (k * (T_ALL // TM_C) + i, 0))
    osp, oss = _two_group_specs(TM_C, D_MODEL)
    hbm = pl.BlockSpec(memory_space=pl.ANY)
    return pl.pallas_call(
        _combine_kernel,
        grid=(T_ALL // TM_C,),
        in_specs=[row(128)] + [slot_rows(k) for k in range(TOP_K)] + [
                  row(D_MODEL), row(D_MODEL),
                  pl.BlockSpec((1, 6, D_MODEL), lambda i: (i // tiles_per_group, 0, 0)),
                  hbm, hbm, hbm],
        out_specs=[osp, oss],
        out_shape=[jax.ShapeDtypeStruct((T_P, D_MODEL), F32),
                   jax.ShapeDtypeStruct((T_S, D_MODEL), F32)],
        scratch_shapes=[pltpu.VMEM((D_MODEL, D_SHARED), BF16),
                        pltpu.VMEM((D_MODEL, D_SHARED), BF16),
                        pltpu.VMEM((D_SHARED, D_MODEL), BF16),
                        pltpu.VMEM((D_MODEL, D_SHARED), F32),
                        pltpu.VMEM((D_SHARED, D_MODEL), F32),
                        pltpu.SemaphoreType.DMA(())],
        compiler_params=_cparams(("arbitrary",)),
        name="combine",
    )(meta, *([ys] * TOP_K), h2, x1, mods3, wsg, wsu, wsd)


def _rope_tables():
    rows = DEC_SEQ // GRID_W
    row = jnp.repeat(jnp.arange(rows, dtype=F32), GRID_W)
    col = jnp.tile(jnp.arange(GRID_W, dtype=F32), rows)
    n_freq = HEAD_DIM // 4
    inv = ROPE_THETA ** (-jnp.arange(n_freq, dtype=F32) / n_freq)
    ang = jnp.concatenate([row[:, None] * inv, col[:, None] * inv], axis=-1)
    cos, sin = jnp.cos(ang), jnp.sin(ang)
    cos2 = jnp.concatenate([cos, cos], axis=-1)
    sin2 = jnp.concatenate([-sin, sin], axis=-1)
    cos_tab = jnp.concatenate([jnp.ones((TM_IN, HEAD_DIM), F32), cos2], axis=0)
    sin_tab = jnp.concatenate([jnp.zeros((TM_IN, HEAD_DIM), F32), sin2], axis=0)
    return cos_tab, sin_tab


def kernel(x_prompt, x_sample, cache_k, cache_v, c, c_ctx, w_ada, b_ada, norm_mix_g, norm_ffn_g, w_in, conv_w,
           q_norm_g, k_norm_g, attn_sink, w_conv_out, w_attn_out, w_out, router_w, router_bias, w_exp_gate,
           w_exp_up, w_exp_down, w_sh_gate, w_sh_up, w_sh_down):
    l = 0
    xp = x_prompt.reshape(T_P, D_MODEL)
    xs = x_sample.reshape(T_S, D_MODEL)

    cond = jnp.concatenate([c_ctx[None, :], c, jnp.zeros((8 - N_MOD, D_MODEL), F32)], axis=0)
    mods = _ada(cond.T, w_ada[l], b_ada[l][None, :])
    mods3 = mods[:N_MOD].reshape(N_MOD, 6, D_MODEL)

    h = _prenorm(xp, xs, mods3, norm_mix_g[l][None, :])
    cos_tab, sin_tab = _rope_tables()
    proj, k32, v32 = _inproj(h, w_in[l], q_norm_g[l][None, :], k_norm_g[l][None, :], cos_tab, sin_tab)

    sink = attn_sink[l]
    kc = cache_k[:, l].reshape(DEC_BATCH, PAST_LEN, KV_WIDTH)
    vc = cache_v[:, l].reshape(DEC_BATCH, PAST_LEN, KV_WIDTH)
    attn_p = _attn_p(sink, proj)
    attn_s = _attn_s(sink, proj, kc, vc)

    z1 = _conv(proj, conv_w[l], w_conv_out[l])

    wr_pad = jnp.pad(router_w[l], ((0, 0), (0, 128 - N_EXPERTS)))
    x1, h2, logits = _merge(xp, xs, attn_p, attn_s, z1, proj, mods3, norm_ffn_g[l][None, :], wr_pad,
                            w_attn_out[l], w_out[l])

    enc, cnt = _route1(logits, router_bias[l][:, None])
    meta, tmap = _route2(enc, cnt)
    pos_flat = meta[:, TOP_K:2 * TOP_K].astype(I32).reshape(T_ALL * TOP_K)
    tile_expert = tmap[0]
    n_active = tmap[1, :1]

    inv = _inv(pos_flat, _pad_codes())
    ys = _moe(tile_expert, n_active, tmap[2], inv, h2, w_exp_gate[l], w_exp_up[l], w_exp_down[l])
    y_p, y_s = _combine(meta, ys, h2, x1, mods3, w_sh_gate[l], w_sh_up[l], w_sh_down[l])

    y_prompt = y_p.reshape(BATCH, SEQ, D_MODEL)
    y_sample = y_s.reshape(DEC_BATCH, DEC_SEQ, D_MODEL)
    new_k = k32.reshape(BATCH, 1, SEQ, N_KV_HEADS, HEAD_DIM)
    new_v = v32.reshape(BATCH, 1, SEQ, N_KV_HEADS, HEAD_DIM)
    return (y_prompt, y_sample, new_k, new_v)
```

```python
import jax
import jax.numpy as jnp
from jax import lax
from jax.experimental import pallas as pl
from jax.experimental.pallas import tpu as pltpu

F32 = jnp.float32
BF16 = jnp.bfloat16
I32 = jnp.int32

D_MODEL = 2048
BATCH = 16
SEQ = 256
DEC_BATCH = 2
DEC_SEQ = 4096
PAST_LEN = 512
GRID_W = 64
N_HEADS = 16
N_KV_HEADS = 4
HEAD_DIM = 128
GROUP = N_HEADS // N_KV_HEADS
ATTN_WIDTH = N_HEADS * HEAD_DIM
KV_WIDTH = N_KV_HEADS * HEAD_DIM
WINDOW = 128
CONV_WIDTH = D_MODEL
N_EXPERTS = 64
TOP_K = 8
N_EXPERT_GROUPS = 8
GROUP_SIZE = N_EXPERTS // N_EXPERT_GROUPS
TOPK_GROUPS = 4
D_EXPERT = 512
D_SHARED = 512
ROUTED_SCALE = 2.5
ROPE_THETA = 10000.0
EPS = 1e-6
NEG_INF = -1e30
ATTN_SCALE = HEAD_DIM ** -0.5
LOG2E = 1.4426950408889634

T_P = BATCH * SEQ
T_S = DEC_BATCH * DEC_SEQ
T_ALL = T_P + T_S
MOD_GROUP = 4096
N_MOD = 1 + DEC_BATCH
assert T_P == MOD_GROUP and DEC_SEQ == MOD_GROUP

COL_CB, COL_CC, COL_CX, COL_Q, COL_GC, COL_GA = (i * D_MODEL for i in range(6))
COL_K = 6 * D_MODEL
COL_V = COL_K + KV_WIDTH
IN_WIDTH = COL_V + KV_WIDTH
W_COL_KV = 3 * CONV_WIDTH + ATTN_WIDTH

TM_IN = 1024
TN_IN = 1024
TM = 256
TM_PRE = 512
TM_C = 128
TILE_E = 256
N_SORTED = T_ALL * TOP_K + N_EXPERTS * TILE_E
N_TILES_E = N_SORTED // TILE_E
TMAP_W = 512
CAST_ROWS = 512
VMEM_LIMIT = 56 * 1024 * 1024


def _cparams(sem):
    return pltpu.CompilerParams(dimension_semantics=sem, vmem_limit_bytes=VMEM_LIMIT)


def _silu(x):
    return x * jax.nn.sigmoid(x)


def _load_cast_weight(w_hbm, w_scr, stage, sem):
    rows = stage.shape[0]
    for c in range(w_hbm.shape[0] // rows):
        cp = pltpu.make_async_copy(w_hbm.at[pl.ds(c * rows, rows)], stage, sem)
        cp.start()
        cp.wait()
        w_scr[pl.ds(c * rows, rows), :] = stage[...].astype(BF16)


ADA_TN = 1024
ADA_CHUNK = 256


def _ada_kernel(ct_ref, w_ref, b_ref, o_ref):
    tn = w_ref.shape[1]

    def body(c, accs):
        k0 = pl.multiple_of(c * ADA_CHUNK, ADA_CHUNK)
        wch = w_ref[pl.ds(k0, ADA_CHUNK), :]
        sch = _silu(ct_ref[pl.ds(k0, ADA_CHUNK), :])
        out = []
        for r in range(N_MOD):
            p = wch * sch[:, r:r + 1]
            out.append(accs[r] + p.reshape(ADA_CHUNK // 8, 8, tn).sum(axis=0))
        return tuple(out)

    accs = lax.fori_loop(0, D_MODEL // ADA_CHUNK, body,
                         tuple(jnp.zeros((8, tn), F32) for _ in range(N_MOD)))
    o_ref[...] = jnp.zeros_like(o_ref)
    for r in range(N_MOD):
        o_ref[r:r + 1, :] = jnp.sum(accs[r], axis=0, keepdims=True) + b_ref[...]


def _ada(cond_t, w_ada, b_ada):
    n = w_ada.shape[1]
    return pl.pallas_call(
        _ada_kernel,
        grid=(n // ADA_TN,),
        in_specs=[pl.BlockSpec((D_MODEL, 8), lambda j: (0, 0)),
                  pl.BlockSpec((D_MODEL, ADA_TN), lambda j: (0, j)),
                  pl.BlockSpec((1, ADA_TN), lambda j: (0, j))],
        out_specs=pl.BlockSpec((8, ADA_TN), lambda j: (0, j)),
        out_shape=jax.ShapeDtypeStruct((8, n), F32),
        compiler_params=_cparams(("arbitrary",)),
        name="ada",
    )(cond_t, w_ada, b_ada)


def _rms_mod(x, g, shift, scale):
    ms = jnp.mean(x * x, axis=-1, keepdims=True)
    y = x * lax.rsqrt(ms + EPS) * g
    return y * (1.0 + scale) + shift


def _two_group_specs(tile, width):
    n_ctx = T_P // tile
    return (pl.BlockSpec((tile, width), lambda i: (jnp.minimum(i, n_ctx - 1), 0)),
            pl.BlockSpec((tile, width), lambda i: (jnp.maximum(i - n_ctx, 0), 0)))


def _prenorm_kernel(xp_ref, xs_ref, mod_ref, g_ref, h_ref):
    i = pl.program_id(0)
    shift = mod_ref[0, 0:1, :]
    scale = mod_ref[0, 1:2, :]
    g = g_ref[...]

    @pl.when(i < T_P // TM_PRE)
    def _():
        h_ref[...] = _rms_mod(xp_ref[...], g, shift, scale).astype(BF16)

    @pl.when(i >= T_P // TM_PRE)
    def _():
        h_ref[...] = _rms_mod(xs_ref[...], g, shift, scale).astype(BF16)


def _prenorm(xp, xs, mods3, g_mix):
    tiles_per_group = MOD_GROUP // TM_PRE
    sp, ss = _two_group_specs(TM_PRE, D_MODEL)
    return pl.pallas_call(
        _prenorm_kernel,
        grid=(T_ALL // TM_PRE,),
        in_specs=[sp, ss,
                  pl.BlockSpec((1, 6, D_MODEL), lambda i: (i // tiles_per_group, 0, 0)),
                  pl.BlockSpec((1, D_MODEL), lambda i: (0, 0))],
        out_specs=pl.BlockSpec((TM_PRE, D_MODEL), lambda i: (i, 0)),
        out_shape=jax.ShapeDtypeStruct((T_ALL, D_MODEL), BF16),
        compiler_params=_cparams(("arbitrary",)),
        name="prenorm",
    )(xp, xs, mods3, g_mix)


def _head_norm_rope(a, g, cos2, sin2, scale):
    ms = jnp.mean(a * a, axis=-1, keepdims=True)
    a = a * lax.rsqrt(ms + EPS) * g
    a = a * cos2 + pltpu.roll(a, HEAD_DIM // 2, 1) * sin2
    return a * scale


N_J = IN_WIDTH // TN_IN
J_Q0 = COL_Q // TN_IN
J_Q1 = J_Q0 + ATTN_WIDTH // TN_IN
J_KV = COL_K // TN_IN
assert J_KV == N_J - 1
N_CTX_IN = T_P // TM_IN


def _inproj_kernel(h_ref, w_ref, qg_ref, kg_ref, cos_ref, sin_ref, o_ref, k_ref, v_ref, w_scr):
    j = pl.program_id(0)
    i = pl.program_id(1)

    @pl.when(i == 0)
    def _():
        for c in range(D_MODEL // CAST_ROWS):
            rows = pl.ds(c * CAST_ROWS, CAST_ROWS)
            w_scr[rows, :] = w_ref[rows, :].astype(BF16)

    acc = jnp.dot(h_ref[...], w_scr[...], preferred_element_type=F32)
    is_q = jnp.logical_and(j >= J_Q0, j < J_Q1)
    is_kv = j == J_KV

    @pl.when(jnp.logical_not(jnp.logical_or(is_q, is_kv)))
    def _():
        o_ref[...] = acc.astype(BF16)

    @pl.when(is_q)
    def _():
        cos2 = cos_ref[...]
        sin2 = sin_ref[...]
        g = qg_ref[...]
        for h in range(TN_IN // HEAD_DIM):
            cols = slice(h * HEAD_DIM, (h + 1) * HEAD_DIM)
            o_ref[:, cols] = _head_norm_rope(acc[:, cols], g, cos2, sin2, ATTN_SCALE * LOG2E).astype(BF16)

    @pl.when(is_kv)
    def _():
        cos2 = cos_ref[...]
        sin2 = sin_ref[...]
        g = kg_ref[...]
        is_ctx = i < N_CTX_IN
        for h in range(N_KV_HEADS):
            cols = slice(h * HEAD_DIM, (h + 1) * HEAD_DIM)
            kh = _head_norm_rope(acc[:, cols], g, cos2, sin2, 1.0)
            o_ref[:, cols] = kh.astype(BF16)

            @pl.when(is_ctx)
            def _():
                k_ref[:, cols] = kh

        v = acc[:, KV_WIDTH:]
        o_ref[:, KV_WIDTH:] = v.astype(BF16)

        @pl.when(is_ctx)
        def _():
            v_ref[...] = v


def _inproj(h, w_in, qg, kg, cos_tab, sin_tab):
    pos_tiles = DEC_SEQ // TM_IN

    def tab_map(j, i):
        return (jnp.where(i < N_CTX_IN, 0, 1 + i % pos_tiles), 0)

    def w_map(j, i):
        return (0, jnp.where(j < W_COL_KV // TN_IN, j, jnp.where(j == J_KV, W_COL_KV // TN_IN, j + 1)))

    def kv_map(j, i):
        return (jnp.where(j == J_KV, jnp.minimum(i, N_CTX_IN - 1), 0), 0)

    return pl.pallas_call(
        _inproj_kernel,
        grid=(N_J, T_ALL // TM_IN),
        in_specs=[pl.BlockSpec((TM_IN, D_MODEL), lambda j, i: (i, 0)),
                  pl.BlockSpec((D_MODEL, TN_IN), w_map, pipeline_mode=pl.Buffered(1)),
                  pl.BlockSpec((1, HEAD_DIM), lambda j, i: (0, 0)),
                  pl.BlockSpec((1, HEAD_DIM), lambda j, i: (0, 0)),
                  pl.BlockSpec((TM_IN, HEAD_DIM), tab_map),
                  pl.BlockSpec((TM_IN, HEAD_DIM), tab_map)],
        out_specs=[pl.BlockSpec((TM_IN, TN_IN), lambda j, i: (i, j)),
                   pl.BlockSpec((TM_IN, KV_WIDTH), kv_map),
                   pl.BlockSpec((TM_IN, KV_WIDTH), kv_map)],
        out_shape=[jax.ShapeDtypeStruct((T_ALL, IN_WIDTH), BF16),
                   jax.ShapeDtypeStruct((T_P, KV_WIDTH), F32),
                   jax.ShapeDtypeStruct((T_P, KV_WIDTH), F32)],
        scratch_shapes=[pltpu.VMEM((D_MODEL, TN_IN), BF16)],
        compiler_params=_cparams(("arbitrary", "arbitrary")),
        name="inproj",
    )(h, w_in, qg, kg, cos_tab, sin_tab)


def _dot_nt(a, b):
    return lax.dot_general(a, b, (((1,), (1,)), ((), ())), preferred_element_type=F32)


def _attn_p_kernel(sink_ref, q_ref, k_ref, v_ref, o_ref):
    for kh in range(N_KV_HEADS):
        kcols = slice(kh * HEAD_DIM, (kh + 1) * HEAD_DIM)
        k = k_ref[:, kcols]
        v = v_ref[:, kcols]
        for gi in range(GROUP):
            h = kh * GROUP + gi
            cols = slice(h * HEAD_DIM, (h + 1) * HEAD_DIM)
            s = _dot_nt(q_ref[:, cols], k)
            sk = sink_ref[h] * LOG2E
            m = jnp.maximum(jnp.max(s, axis=-1, keepdims=True), sk)
            p = jnp.exp2(s - m)
            den = jnp.sum(p, axis=-1, keepdims=True) + jnp.exp2(sk - m)
            o = jnp.dot(p.astype(BF16), v, preferred_element_type=F32) / den
            o_ref[:, cols] = o.astype(BF16)


def _attn_p(sink, proj):
    return pl.pallas_call(
        _attn_p_kernel,
        grid=(BATCH,),
        in_specs=[pl.BlockSpec(memory_space=pltpu.SMEM),
                  pl.BlockSpec((SEQ, ATTN_WIDTH), lambda b: (b, COL_Q // ATTN_WIDTH)),
                  pl.BlockSpec((SEQ, KV_WIDTH), lambda b: (b, COL_K // KV_WIDTH)),
                  pl.BlockSpec((SEQ, KV_WIDTH), lambda b: (b, COL_V // KV_WIDTH))],
        out_specs=pl.BlockSpec((SEQ, ATTN_WIDTH), lambda b: (b, 0)),
        out_shape=jax.ShapeDtypeStruct((T_P, ATTN_WIDTH), BF16),
        compiler_params=_cparams(("arbitrary",)),
        name="attn_p",
    )(sink, proj, proj, proj)


QB = 256
BAND = QB + 2 * WINDOW


def _attn_s_kernel(sink_ref, q_ref, k_ref, v_ref, kc_ref, vc_ref, o_ref):
    i = pl.program_id(1)
    start = jnp.clip(i * QB - WINDOW, 0, DEC_SEQ - BAND)
    start = pl.multiple_of(start, WINDOW)
    qpos = i * QB + lax.broadcasted_iota(I32, (QB, BAND), 0)
    kpos = start + lax.broadcasted_iota(I32, (QB, BAND), 1)
    valid = jnp.abs(qpos - kpos) <= WINDOW
    valid_g = jnp.concatenate([valid] * GROUP, axis=0)
    for kh in range(N_KV_HEADS):
        kcols = slice(kh * HEAD_DIM, (kh + 1) * HEAD_DIM)
        kb = k_ref[pl.ds(start, BAND), kcols]
        kc = kc_ref[0, :, kcols].astype(BF16)
        ones = jnp.ones((BAND, HEAD_DIM), BF16)
        vb1 = jnp.concatenate([v_ref[pl.ds(start, BAND), kcols], ones], axis=1)
        vc1 = jnp.concatenate([vc_ref[0, :, kcols].astype(BF16), ones[:PAST_LEN]], axis=1)
        heads = [kh * GROUP + gi for gi in range(GROUP)]
        q = jnp.concatenate([q_ref[:, h * HEAD_DIM:(h + 1) * HEAD_DIM] for h in heads], axis=0)
        sk = jnp.concatenate([jnp.full((QB, 1), sink_ref[h] * LOG2E, F32) for h in heads], axis=0)
        s_loc = jnp.where(valid_g, _dot_nt(q, kb), NEG_INF)
        s_ctx = _dot_nt(q, kc)
        m = jnp.maximum(jnp.maximum(jnp.max(s_loc, axis=-1, keepdims=True),
                                    jnp.max(s_ctx, axis=-1, keepdims=True)), sk)
        p_loc = jnp.exp2(s_loc - m).astype(BF16)
        p_ctx = jnp.exp2(s_ctx - m).astype(BF16)
        o1 = (jnp.dot(p_loc, vb1, preferred_element_type=F32)
              + jnp.dot(p_ctx, vc1, preferred_element_type=F32))
        den = o1[:, HEAD_DIM:HEAD_DIM + 1] + jnp.exp2(sk - m)
        o = o1[:, :HEAD_DIM] / den
        for gi, h in enumerate(heads):
            o_ref[:, h * HEAD_DIM:(h + 1) * HEAD_DIM] = o[gi * QB:(gi + 1) * QB, :].astype(BF16)


def _attn_s(sink, proj, kc, vc):
    nq = DEC_SEQ // QB
    first = T_P // QB
    seq_blk = T_P // DEC_SEQ
    return pl.pallas_call(
        _attn_s_kernel,
        grid=(DEC_BATCH, nq),
        in_specs=[pl.BlockSpec(memory_space=pltpu.SMEM),
                  pl.BlockSpec((QB, ATTN_WIDTH), lambda b, i: (first + b * nq + i, COL_Q // ATTN_WIDTH)),
                  pl.BlockSpec((DEC_SEQ, KV_WIDTH), lambda b, i: (seq_blk + b, COL_K // KV_WIDTH)),
                  pl.BlockSpec((DEC_SEQ, KV_WIDTH), lambda b, i: (seq_blk + b, COL_V // KV_WIDTH)),
                  pl.BlockSpec((1, PAST_LEN, KV_WIDTH), lambda b, i: (b, 0, 0)),
                  pl.BlockSpec((1, PAST_LEN, KV_WIDTH), lambda b, i: (b, 0, 0))],
        out_specs=pl.BlockSpec((QB, ATTN_WIDTH), lambda b, i: (b * nq + i, 0)),
        out_shape=jax.ShapeDtypeStruct((T_S, ATTN_WIDTH), BF16),
        compiler_params=_cparams(("arbitrary", "arbitrary")),
        name="attn_s",
    )(sink, proj, proj, proj, kc, vc)


HALO = 16


def _conv_kernel(cb_ref, cc_ref, cx_ref, ccp_ref, cxp_ref, ccn_ref, cxn_ref, gc_ref, cw_ref, w_hbm, o_ref,
                 w_scr, stage, sem):
    i = pl.program_id(0)

    @pl.when(i == 0)
    def _():
        _load_cast_weight(w_hbm, w_scr, stage, sem)

    n_ctx_tiles = T_P // TM
    per_seq = DEC_SEQ // TM
    is_first = jnp.logical_or(i < n_ctx_tiles, (i - n_ctx_tiles) % per_seq == 0)
    is_last = jnp.logical_or(i < n_ctx_tiles, (i - n_ctx_tiles) % per_seq == per_seq - 1)
    p = cc_ref[...].astype(F32) * cx_ref[...].astype(F32)
    prev_row = ccp_ref[HALO - 1:HALO, :].astype(F32) * cxp_ref[HALO - 1:HALO, :].astype(F32)
    next_row = ccn_ref[0:1, :].astype(F32) * cxn_ref[0:1, :].astype(F32)
    prev_row = jnp.where(is_first, 0.0, prev_row)
    next_row = jnp.where(is_last, 0.0, next_row)
    rows = lax.broadcasted_iota(I32, (TM, 1), 0)
    p_prev = jnp.where(rows == 0, prev_row, pltpu.roll(p, 1, 0))
    p_next = jnp.where(rows == TM - 1, next_row, pltpu.roll(p, TM - 1, 0))
    conv = p_prev * cw_ref[0:1, :] + p * cw_ref[1:2, :] + p_next * cw_ref[2:3, :]
    u = cb_ref[...].astype(F32) * conv
    y = jnp.dot(u.astype(BF16), w_scr[...], preferred_element_type=F32)
    o_ref[...] = (jax.nn.sigmoid(gc_ref[...].astype(F32)) * y).astype(BF16)


def _conv(proj, conv_w, w_conv_out):
    hb = TM // HALO
    last_hb = T_ALL // HALO - 1
    wide = lambda c: pl.BlockSpec((TM, D_MODEL), lambda i: (i, c // D_MODEL))
    prev = lambda c: pl.BlockSpec((HALO, D_MODEL), lambda i: (jnp.maximum(i * hb - 1, 0), c // D_MODEL))
    nxt = lambda c: pl.BlockSpec((HALO, D_MODEL), lambda i: (jnp.minimum((i + 1) * hb, last_hb), c // D_MODEL))
    return pl.pallas_call(
        _conv_kernel,
        grid=(T_ALL // TM,),
        in_specs=[wide(COL_CB), wide(COL_CC), wide(COL_CX),
                  prev(COL_CC), prev(COL_CX), nxt(COL_CC), nxt(COL_CX),
                  wide(COL_GC),
                  pl.BlockSpec((3, CONV_WIDTH), lambda i: (0, 0)),
                  pl.BlockSpec(memory_space=pl.ANY)],
        out_specs=pl.BlockSpec((TM, D_MODEL), lambda i: (i, 0)),
        out_shape=jax.ShapeDtypeStruct((T_ALL, D_MODEL), BF16),
        scratch_shapes=[pltpu.VMEM((CONV_WIDTH, D_MODEL), BF16),
                        pltpu.VMEM((CAST_ROWS, D_MODEL), F32),
                        pltpu.SemaphoreType.DMA(())],
        compiler_params=_cparams(("arbitrary",)),
        name="conv",
    )(proj, proj, proj, proj, proj, proj, proj, proj, conv_w, w_conv_out)


def _merge_kernel(xp_ref, xs_ref, attn_p_ref, attn_s_ref, z1_ref, ga_ref, mod_ref, g2_ref, wr_ref,
                  wao_hbm, wo_hbm, x1_ref, h2_ref, lg_ref, wao_scr, wo_scr, stage, sem):
    i = pl.program_id(0)

    @pl.when(i == 0)
    def _():
        _load_cast_weight(wao_hbm, wao_scr, stage, sem)
        _load_cast_weight(wo_hbm, wo_scr, stage, sem)

    is_ctx = i < T_P // TM
    attn = jnp.where(is_ctx, attn_p_ref[...], attn_s_ref[...])
    x = jnp.where(is_ctx, xp_ref[...], xs_ref[...])
    ya = jnp.dot(attn, wao_scr[...], preferred_element_type=F32)
    z = z1_ref[...].astype(F32) + jax.nn.sigmoid(ga_ref[...].astype(F32)) * ya
    mix = jnp.dot(z.astype(BF16), wo_scr[...], preferred_element_type=F32)
    x1 = x + mod_ref[0, 2:3, :] * mix
    x1_ref[...] = x1
    h = _rms_mod(x1, g2_ref[...], mod_ref[0, 3:4, :], mod_ref[0, 4:5, :])
    h2_ref[...] = h
    h_hi = h.astype(BF16)
    h_lo = (h - h_hi.astype(F32)).astype(BF16)
    wr = wr_ref[...]
    wr_hi32 = wr.astype(BF16).astype(F32)
    wr_lo32 = (wr - wr_hi32).astype(BF16).astype(F32)
    w_both = (wr_hi32 + pltpu.roll(wr_lo32, N_EXPERTS, 1)).astype(BF16)
    both = jnp.dot(h_hi, w_both, preferred_element_type=F32)
    lg_ref[...] = (both + pltpu.roll(both, N_EXPERTS, 1)
                   + jnp.dot(h_lo, wr_hi32.astype(BF16), preferred_element_type=F32))


def _merge(xp, xs, attn_p, attn_s, z1, proj, mods3, g_ffn, wr_pad, w_attn_out, w_out):
    tiles_per_group = MOD_GROUP // TM
    const = lambda shape: pl.BlockSpec(shape, lambda i: (0, 0))
    row = lambda w: pl.BlockSpec((TM, w), lambda i: (i, 0))
    xsp, xss = _two_group_specs(TM, D_MODEL)
    asp, ass = _two_group_specs(TM, ATTN_WIDTH)
    return pl.pallas_call(
        _merge_kernel,
        grid=(T_ALL // TM,),
        in_specs=[xsp, xss, asp, ass, row(D_MODEL),
                  pl.BlockSpec((TM, D_MODEL), lambda i: (i, COL_GA // D_MODEL)),
                  pl.BlockSpec((1, 6, D_MODEL), lambda i: (i // tiles_per_group, 0, 0)),
                  const((1, D_MODEL)), const((D_MODEL, 128)),
                  pl.BlockSpec(memory_space=pl.ANY), pl.BlockSpec(memory_space=pl.ANY)],
        out_specs=[row(D_MODEL), row(D_MODEL), row(128)],
        out_shape=[jax.ShapeDtypeStruct((T_ALL, D_MODEL), F32),
                   jax.ShapeDtypeStruct((T_ALL, D_MODEL), F32),
                   jax.ShapeDtypeStruct((T_ALL, 128), F32)],
        scratch_shapes=[pltpu.VMEM((ATTN_WIDTH, D_MODEL), BF16),
                        pltpu.VMEM((D_MODEL, D_MODEL), BF16),
                        pltpu.VMEM((CAST_ROWS, D_MODEL), F32),
                        pltpu.SemaphoreType.DMA(())],
        compiler_params=_cparams(("arbitrary",)),
        name="merge",
    )(xp, xs, attn_p, attn_s, z1, proj, mods3, g_ffn, wr_pad, w_attn_out, w_out)


def _route1_kernel(lg_ref, bias_ref, enc_ref, cnt_ref):
    i = pl.program_id(0)
    lt = lg_ref[...].T[:N_EXPERTS, :]
    scores = jax.nn.sigmoid(lt)
    biased = scores + bias_ref[...]
    b3 = biased.reshape(N_EXPERT_GROUPS, GROUP_SIZE, TM)
    mi = lax.broadcasted_iota(I32, b3.shape, 1)
    m1 = jnp.max(b3, axis=1, keepdims=True)
    idx1 = jnp.min(jnp.where(b3 == m1, mi, GROUP_SIZE), axis=1, keepdims=True)
    m2 = jnp.max(jnp.where(mi == idx1, -jnp.inf, b3), axis=1, keepdims=True)
    gs = (m1 + m2).reshape(N_EXPERT_GROUPS, TM)
    gidx = lax.broadcasted_iota(I32, gs.shape, 0)
    grank = jnp.zeros(gs.shape, I32)
    for j in range(N_EXPERT_GROUPS):
        gj = gs[j:j + 1, :]
        beats = jnp.logical_or(gj > gs, jnp.logical_and(gj == gs, j < gidx))
        grank = grank + beats.astype(I32)
    gsel = grank < TOPK_GROUPS
    emask = jnp.broadcast_to(gsel[:, None, :], b3.shape).reshape(N_EXPERTS, TM)
    masked = jnp.where(emask, biased, NEG_INF)
    eidx = lax.broadcasted_iota(I32, masked.shape, 0)
    erank = jnp.zeros(masked.shape, I32)
    for j in range(N_EXPERTS):
        vj = masked[j:j + 1, :]
        beats = jnp.logical_or(vj > masked, jnp.logical_and(vj == masked, j < eidx))
        erank = erank + beats.astype(I32)
    sel = erank < TOP_K
    wsel = jnp.where(sel, scores, 0.0)
    den = jnp.sum(wsel, axis=0, keepdims=True)
    wts = wsel / den * ROUTED_SCALE
    enc_ref[...] = jnp.where(sel, wts, -1.0)

    @pl.when(i == 0)
    def _():
        cnt_ref[...] = jnp.zeros_like(cnt_ref)

    cnt = jnp.sum(sel.astype(F32), axis=1, keepdims=True)
    cnt_ref[...] += jnp.broadcast_to(cnt, cnt_ref.shape)


def _route1(logits, bias_col):
    return pl.pallas_call(
        _route1_kernel,
        grid=(T_ALL // TM,),
        in_specs=[pl.BlockSpec((TM, 128), lambda i: (i, 0)),
                  pl.BlockSpec((N_EXPERTS, 1), lambda i: (0, 0))],
        out_specs=[pl.BlockSpec((N_EXPERTS, TM), lambda i: (0, i)),
                   pl.BlockSpec((N_EXPERTS, 128), lambda i: (0, 0))],
        out_shape=[jax.ShapeDtypeStruct((N_EXPERTS, T_ALL), F32),
                   jax.ShapeDtypeStruct((N_EXPERTS, 128), F32)],
        compiler_params=_cparams(("arbitrary",)),
        name="route1",
    )(logits, bias_col)


def _route2_positions(enc_ref, cnt_ref, meta_ref, tmap_ref, carry_ref):
    enc = enc_ref[...]
    sel = enc >= 0.0
    wts = jnp.maximum(enc, 0.0)
    sel_b = sel.astype(BF16)
    ntile = jnp.ceil(cnt_ref[...] * (1.0 / TILE_E))
    er = lax.broadcasted_iota(I32, (N_EXPERTS, N_EXPERTS), 0)
    ec = lax.broadcasted_iota(I32, (N_EXPERTS, N_EXPERTS), 1)
    lower = (ec < er).astype(BF16)
    off_t = jnp.dot(lower, ntile.astype(BF16), preferred_element_type=F32)
    tr = lax.broadcasted_iota(I32, (TM, TM), 0)
    tc = lax.broadcasted_iota(I32, (TM, TM), 1)
    upper = (tr < tc).astype(BF16)
    rank = jnp.dot(sel_b, upper, preferred_element_type=F32) + carry_ref[:, 0:1]
    carry_ref[...] += jnp.broadcast_to(jnp.sum(sel.astype(F32), axis=1, keepdims=True), carry_ref.shape)
    pos = off_t[:, 0:1] * float(TILE_E) + rank
    slot = jnp.dot(lower, sel_b, preferred_element_type=F32)
    rows = []
    for k in range(TOP_K):
        mk = jnp.logical_and(sel, slot == float(k))
        rows.append(jnp.sum(jnp.where(mk, wts, 0.0), axis=0, keepdims=True))
    for k in range(TOP_K):
        mk = jnp.logical_and(sel, slot == float(k))
        rows.append(jnp.sum(jnp.where(mk, pos, 0.0), axis=0, keepdims=True))
    rows.append(jnp.zeros((128 - 2 * TOP_K, TM), F32))
    meta_ref[...] = jnp.concatenate(rows, axis=0).T
    nt = ntile[:, 0:1]
    end_t = off_t[:, 0:1] + nt
    lane = lax.broadcasted_iota(I32, (N_EXPERTS, TMAP_W), 1)
    te = jnp.sum((end_t <= lane.astype(F32)).astype(F32), axis=0, keepdims=True)
    te = jnp.minimum(te, float(N_EXPERTS - 1))
    nact = jnp.sum(nt, axis=0, keepdims=True)
    erow = lax.broadcasted_iota(I32, (N_EXPERTS, TMAP_W), 0).astype(F32)
    nxt = jnp.sum(jnp.where(erow == te, end_t, 0.0), axis=0, keepdims=True)
    tmap_ref[...] = jnp.zeros_like(tmap_ref)
    tmap_ref[0:1, :] = te.astype(I32)
    tmap_ref[1:2, :] = jnp.broadcast_to(nact, (1, TMAP_W)).astype(I32)
    tmap_ref[2:3, :] = nxt.astype(I32)
    return jnp.concatenate(rows[TOP_K:2 * TOP_K], axis=0).astype(I32)


N_TOK_TILES = T_ALL // TM


def _route2_kernel(enc_ref, cnt_ref, init_hbm, meta_ref, tmap_ref, inv_hbm,
                   carry_ref, posv0, posv1, poss0, poss1, inv_smem, psem, isem):
    i = pl.program_id(0)
    posv = (posv0, posv1)
    poss = (poss0, poss1)
    per_token = (1 << TOKEN_BITS) + 1
    per_slot = T_ALL << TOKEN_BITS

    @pl.when(i == 0)
    def _():
        carry_ref[...] = jnp.zeros_like(carry_ref)
        cp = pltpu.make_async_copy(init_hbm, inv_smem, isem)
        cp.start()
        cp.wait()

    def positions(s):
        posv[s][...] = _route2_positions(enc_ref, cnt_ref, meta_ref, tmap_ref, carry_ref)
        pltpu.make_async_copy(posv[s], poss[s], psem.at[s]).start()

    def invert(s):
        pltpu.make_async_copy(posv[s], poss[s], psem.at[s]).wait()
        base = (i - 1) * (TM * per_token)
        for t in range(TM):
            for k in range(TOP_K):
                inv_smem[poss[s][k, t]] = base + (t * per_token + k * per_slot)

    @pl.when(i == 0)
    def _():
        positions(0)

    for s in range(2):
        @pl.when(jnp.logical_and(jnp.logical_and(i >= 1, i < N_TOK_TILES), i % 2 == s))
        def _():
            invert(1 - s)
            positions(s)

    @pl.when(i == N_TOK_TILES)
    def _():
        invert((N_TOK_TILES - 1) % 2)
        cp = pltpu.make_async_copy(inv_smem, inv_hbm, isem)
        cp.start()
        cp.wait()


def _route2(enc, cnt, inv_init):
    last = N_TOK_TILES - 1
    hbm = pl.BlockSpec(memory_space=pl.ANY)
    return pl.pallas_call(
        _route2_kernel,
        grid=(N_TOK_TILES + 1,),
        in_specs=[pl.BlockSpec((N_EXPERTS, TM), lambda i: (0, jnp.minimum(i, last))),
                  pl.BlockSpec((N_EXPERTS, 128), lambda i: (0, 0)),
                  hbm],
        out_specs=[pl.BlockSpec((TM, 128), lambda i: (jnp.minimum(i, last), 0)),
                   pl.BlockSpec((8, TMAP_W), lambda i: (0, 0)),
                   hbm],
        out_shape=[jax.ShapeDtypeStruct((T_ALL, 128), F32),
                   jax.ShapeDtypeStruct((8, TMAP_W), I32),
                   jax.ShapeDtypeStruct((N_SORTED,), I32)],
        scratch_shapes=[pltpu.VMEM((N_EXPERTS, 128), F32),
                        pltpu.VMEM((TOP_K, TM), I32), pltpu.VMEM((TOP_K, TM), I32),
                        pltpu.SMEM((TOP_K, TM), I32), pltpu.SMEM((TOP_K, TM), I32),
                        pltpu.SMEM((N_SORTED,), I32),
                        pltpu.SemaphoreType.DMA((2,)), pltpu.SemaphoreType.DMA(())],
        compiler_params=_cparams(("arbitrary",)),
        name="route2",
    )(enc, cnt, inv_init)


TOKEN_BITS = 14
TOKEN_MASK = (1 << TOKEN_BITS) - 1
NBUF = 3
TRASH_BASE = TOP_K * T_ALL
YS_ROWS = TRASH_BASE + NBUF * TILE_E
assert T_ALL <= 1 << TOKEN_BITS and YS_ROWS << TOKEN_BITS < 1 << 31


def _pad_codes():
    r = jnp.arange(N_SORTED, dtype=I32)
    q = r % TILE_E
    out_row = TRASH_BASE + ((r // TILE_E) % NBUF) * TILE_E + q
    return (out_row << TOKEN_BITS) | q


N_CHUNK = 256
GU_PIECES = D_EXPERT // N_CHUNK
DN_PIECES = D_MODEL // N_CHUNK
N_PIECES = GU_PIECES + DN_PIECES

PIECE_WORK = (D_MODEL * 2,) * GU_PIECES + (D_EXPERT,) * DN_PIECES
PIECE_ROWS = tuple(round(TILE_E * sum(PIECE_WORK[:p]) / sum(PIECE_WORK)) for p in range(N_PIECES + 1))


def _moe_kernel(te_ref, na_ref, nxt_ref, inv_ref, h2_hbm, wg_hbm, wu_hbm, wd_hbm, ys_hbm,
                xbuf0, xbuf1, xbuf2, ybuf0, ybuf1, ybuf2, xb, act, wg_f32, wu_f32, wd_f32,
                wg_scr, wu_scr, wd_scr, wset, gsem, ssem, wsem):
    i = pl.program_id(0)
    na = na_ref[0]
    active = i < na
    xbufs = (xbuf0, xbuf1, xbuf2)
    ybufs = (ybuf0, ybuf1, ybuf2)

    def weight_copies(e, p):
        return (pltpu.make_async_copy(wg_hbm.at[e], wg_f32.at[p], wsem.at[p]),
                pltpu.make_async_copy(wu_hbm.at[e], wu_f32.at[p], wsem.at[p]),
                pltpu.make_async_copy(wd_hbm.at[e], wd_f32.at[p], wsem.at[p]))

    def gather_row(tile, s, r):
        tok = inv_ref[tile * TILE_E + r] & TOKEN_MASK
        pltpu.make_async_copy(h2_hbm.at[pl.ds(tok, 1)], xbufs[s].at[pl.ds(r, 1)], gsem.at[s]).start()

    def scatter_row(dst, s, r):
        pltpu.make_async_copy(ybufs[s].at[pl.ds(r, 1)], ys_hbm.at[pl.ds(dst, 1)], ssem.at[s]).start()

    def wait_gather(s):
        pltpu.make_async_copy(h2_hbm.at[pl.ds(0, TILE_E)], xbufs[s], gsem.at[s]).wait()

    def wait_scatter(s):
        pltpu.make_async_copy(ybufs[s], ys_hbm.at[pl.ds(0, TILE_E)], ssem.at[s]).wait()

    @pl.when(i == 0)
    def _():
        zeros = ybufs[NBUF - 1]
        zeros[...] = jnp.zeros_like(zeros)
        for m in range(NBUF):
            cp = pltpu.make_async_copy(zeros, ys_hbm.at[pl.ds(TRASH_BASE + m * TILE_E, TILE_E)], ssem.at[0])
            cp.start()
            cp.wait()
        for t in range(2):
            def body(r, carry):
                gather_row(t, t, r)
                return carry

            lax.fori_loop(0, TILE_E, body, 0)

        wset[0] = 0
        for cp in weight_copies(te_ref[0], 0):
            cp.start()

    new_expert = jnp.logical_or(i == 0, te_ref[i] != te_ref[jnp.maximum(i - 1, 0)])

    @pl.when(jnp.logical_and(active, new_expert))
    def _():
        p = wset[0]
        for cp in weight_copies(0, p):
            cp.wait()
        for c in range(D_MODEL // CAST_ROWS):
            rows = pl.ds(c * CAST_ROWS, CAST_ROWS)
            wg_scr[rows, :] = wg_f32[p, rows, :].astype(BF16)
            wu_scr[rows, :] = wu_f32[p, rows, :].astype(BF16)
        wd_scr[...] = wd_f32[p].astype(BF16)
        nxt_tile = nxt_ref[i]

        @pl.when(nxt_tile < na)
        def _():
            for cp in weight_copies(te_ref[nxt_tile], 1 - p):
                cp.start()

        wset[0] = 1 - p

    def compute_tile(slot, write_back):
        s_next = (slot + 2) % NBUF
        s_prev = (slot - 1) % NBUF
        wait_gather(slot)

        @pl.when(i >= 3)
        def _():
            wait_scatter(slot)

        xb[...] = xbufs[slot][...].astype(BF16)
        nxt = jnp.minimum(i + 2, N_TILES_E - 1)
        codes_next = inv_ref.at[pl.ds(pl.multiple_of(nxt * TILE_E, TILE_E), TILE_E)]
        codes_prev = inv_ref.at[pl.ds(pl.multiple_of(jnp.maximum(i - 1, 0) * TILE_E, TILE_E), TILE_E)]

        def issue(piece):
            for r in range(PIECE_ROWS[piece], PIECE_ROWS[piece + 1]):
                tok = codes_next[r] & TOKEN_MASK
                pltpu.make_async_copy(h2_hbm.at[pl.ds(tok, 1)], xbufs[s_next].at[pl.ds(r, 1)],
                                      gsem.at[s_next]).start()
                if write_back:
                    scatter_row(codes_prev[r] >> TOKEN_BITS, s_prev, r)

        for c in range(GU_PIECES):
            cols = slice(c * N_CHUNK, (c + 1) * N_CHUNK)
            g = jnp.dot(xb[...], wg_scr[:, cols], preferred_element_type=F32)
            u = jnp.dot(xb[...], wu_scr[:, cols], preferred_element_type=F32)
            act[:, cols] = (_silu(g) * u).astype(BF16)
            issue(c)
        for c in range(DN_PIECES):
            cols = slice(c * N_CHUNK, (c + 1) * N_CHUNK)
            ybufs[slot][:, cols] = jnp.dot(act[...], wd_scr[:, cols], preferred_element_type=F32)
            issue(GU_PIECES + c)

    def drain(slot):
        s_last = (slot - 1) % NBUF

        def body(r, carry):
            scatter_row(inv_ref[(na - 1) * TILE_E + r] >> TOKEN_BITS, s_last, r)
            return carry

        lax.fori_loop(0, TILE_E, body, 0)
        wait_scatter(s_last)

        @pl.when(na >= 2)
        def _():
            wait_scatter((slot - 2) % NBUF)

        @pl.when(na >= 3)
        def _():
            wait_scatter(slot)

        wait_gather(slot)
        wait_gather((slot + 1) % NBUF)

    @pl.when(jnp.logical_and(active, i == 0))
    def _():
        compute_tile(0, write_back=False)

    for s in range(NBUF):
        @pl.when(jnp.logical_and(jnp.logical_and(active, i > 0), i % NBUF == s))
        def _():
            compute_tile(s, write_back=True)

        @pl.when(jnp.logical_and(i == na, i % NBUF == s))
        def _():
            drain(s)


def _moe(tile_expert, n_active, next_tile, inv, h2, wg, wu, wd):
    hbm = pl.BlockSpec(memory_space=pl.ANY)
    grid_spec = pltpu.PrefetchScalarGridSpec(
        num_scalar_prefetch=4,
        grid=(N_TILES_E + 1,),
        in_specs=[hbm, hbm, hbm, hbm],
        out_specs=hbm,
        scratch_shapes=[pltpu.VMEM((TILE_E, D_MODEL), F32)] * (2 * NBUF) + [
                        pltpu.VMEM((TILE_E, D_MODEL), BF16),
                        pltpu.VMEM((TILE_E, D_EXPERT), BF16),
                        pltpu.VMEM((2, D_MODEL, D_EXPERT), F32),
                        pltpu.VMEM((2, D_MODEL, D_EXPERT), F32),
                        pltpu.VMEM((2, D_EXPERT, D_MODEL), F32),
                        pltpu.VMEM((D_MODEL, D_EXPERT), BF16),
                        pltpu.VMEM((D_MODEL, D_EXPERT), BF16),
                        pltpu.VMEM((D_EXPERT, D_MODEL), BF16),
                        pltpu.SMEM((1,), I32),
                        pltpu.SemaphoreType.DMA((NBUF,)),
                        pltpu.SemaphoreType.DMA((NBUF,)),
                        pltpu.SemaphoreType.DMA((2,))],
    )
    return pl.pallas_call(
        _moe_kernel,
        grid_spec=grid_spec,
        out_shape=jax.ShapeDtypeStruct((YS_ROWS, D_MODEL), F32),
        compiler_params=_cparams(("arbitrary",)),
        name="moe",
    )(tile_expert, n_active, next_tile, inv, h2, wg, wu, wd)


def _combine_kernel(meta_ref, *refs):
    ys_refs = refs[:TOP_K]
    h2_ref, x1_ref, mod_ref, wsg_hbm, wsu_hbm, wsd_hbm = refs[TOP_K:TOP_K + 6]
    op_ref, os_ref, wsg_scr, wsu_scr, wsd_scr, stage_a, stage_b, wsem = refs[TOP_K + 6:]
    i = pl.program_id(0)

    @pl.when(i == 0)
    def _():
        _load_cast_weight(wsg_hbm, wsg_scr, stage_a, wsem)
        _load_cast_weight(wsu_hbm, wsu_scr, stage_a, wsem)
        _load_cast_weight(wsd_hbm, wsd_scr, stage_b, wsem)

    h = h2_ref[...].astype(BF16)
    sg = jnp.dot(h, wsg_scr[...], preferred_element_type=F32)
    su = jnp.dot(h, wsu_scr[...], preferred_element_type=F32)
    moe = jnp.dot((_silu(sg) * su).astype(BF16), wsd_scr[...], preferred_element_type=F32)
    for k in range(TOP_K):
        moe = moe + meta_ref[:, k:k + 1] * ys_refs[k][...]
    y = x1_ref[...] + mod_ref[0, 5:6, :] * moe

    @pl.when(i < T_P // TM_C)
    def _():
        op_ref[...] = y

    @pl.when(i >= T_P // TM_C)
    def _():
        os_ref[...] = y


def _combine(meta, ys, h2, x1, mods3, wsg, wsu, wsd):
    tiles_per_group = MOD_GROUP // TM_C
    row = lambda w: pl.BlockSpec((TM_C, w), lambda i: (i, 0))
    slot_rows = lambda k: pl.BlockSpec((TM_C, D_MODEL), lambda i: (k * (T_ALL // TM_C) + i, 0))
    osp, oss = _two_group_specs(TM_C, D_MODEL)
    hbm = pl.BlockSpec(memory_space=pl.ANY)
    return pl.pallas_call(
        _combine_kernel,
        grid=(T_ALL // TM_C,),
        in_specs=[row(128)] + [slot_rows(k) for k in range(TOP_K)] + [
                  row(D_MODEL), row(D_MODEL),
                  pl.BlockSpec((1, 6, D_MODEL), lambda i: (i // tiles_per_group, 0, 0)),
                  hbm, hbm, hbm],
        out_specs=[osp, oss],
        out_shape=[jax.ShapeDtypeStruct((T_P, D_MODEL), F32),
                   jax.ShapeDtypeStruct((T_S, D_MODEL), F32)],
        scratch_shapes=[pltpu.VMEM((D_MODEL, D_SHARED), BF16),
                        pltpu.VMEM((D_MODEL, D_SHARED), BF16),
                        pltpu.VMEM((D_SHARED, D_MODEL), BF16),
                        pltpu.VMEM((D_MODEL, D_SHARED), F32),
                        pltpu.VMEM((D_SHARED, D_MODEL), F32),
                        pltpu.SemaphoreType.DMA(())],
        compiler_params=_cparams(("arbitrary",)),
        name="combine",
    )(meta, *([ys] * TOP_K), h2, x1, mods3, wsg, wsu, wsd)


def _rope_tables():
    rows = DEC_SEQ // GRID_W
    row = jnp.repeat(jnp.arange(rows, dtype=F32), GRID_W)
    col = jnp.tile(jnp.arange(GRID_W, dtype=F32), rows)
    n_freq = HEAD_DIM // 4
    inv = ROPE_THETA ** (-jnp.arange(n_freq, dtype=F32) / n_freq)
    ang = jnp.concatenate([row[:, None] * inv, col[:, None] * inv], axis=-1)
    cos, sin = jnp.cos(ang), jnp.sin(ang)
    cos2 = jnp.concatenate([cos, cos], axis=-1)
    sin2 = jnp.concatenate([-sin, sin], axis=-1)
    cos_tab = jnp.concatenate([jnp.ones((TM_IN, HEAD_DIM), F32), cos2], axis=0)
    sin_tab = jnp.concatenate([jnp.zeros((TM_IN, HEAD_DIM), F32), sin2], axis=0)
    return cos_tab, sin_tab


def kernel(x_prompt, x_sample, cache_k, cache_v, c, c_ctx, w_ada, b_ada, norm_mix_g, norm_ffn_g, w_in, conv_w,
           q_norm_g, k_norm_g, attn_sink, w_conv_out, w_attn_out, w_out, router_w, router_bias, w_exp_gate,
           w_exp_up, w_exp_down, w_sh_gate, w_sh_up, w_sh_down):
    l = 0
    xp = x_prompt.reshape(T_P, D_MODEL)
    xs = x_sample.reshape(T_S, D_MODEL)

    cond = jnp.concatenate([c_ctx[None, :], c, jnp.zeros((8 - N_MOD, D_MODEL), F32)], axis=0)
    mods = _ada(cond.T, w_ada[l], b_ada[l][None, :])
    mods3 = mods[:N_MOD].reshape(N_MOD, 6, D_MODEL)

    h = _prenorm(xp, xs, mods3, norm_mix_g[l][None, :])
    cos_tab, sin_tab = _rope_tables()
    proj, k32, v32 = _inproj(h, w_in[l], q_norm_g[l][None, :], k_norm_g[l][None, :], cos_tab, sin_tab)

    sink = attn_sink[l]
    kc = cache_k[:, l].reshape(DEC_BATCH, PAST_LEN, KV_WIDTH)
    vc = cache_v[:, l].reshape(DEC_BATCH, PAST_LEN, KV_WIDTH)
    attn_p = _attn_p(sink, proj)
    attn_s = _attn_s(sink, proj, kc, vc)

    z1 = _conv(proj, conv_w[l], w_conv_out[l])

    wr_pad = jnp.pad(router_w[l], ((0, 0), (0, 128 - N_EXPERTS)))
    x1, h2, logits = _merge(xp, xs, attn_p, attn_s, z1, proj, mods3, norm_ffn_g[l][None, :], wr_pad,
                            w_attn_out[l], w_out[l])

    enc, cnt = _route1(logits, router_bias[l][:, None])
    meta, tmap, inv = _route2(enc, cnt, _pad_codes())
    tile_expert = tmap[0]
    n_active = tmap[1, :1]

    ys = _moe(tile_expert, n_active, tmap[2], inv, h2, w_exp_gate[l], w_exp_up[l], w_exp_down[l])
    y_p, y_s = _combine(meta, ys, h2, x1, mods3, w_sh_gate[l], w_sh_up[l], w_sh_down[l])

    y_prompt = y_p.reshape(BATCH, SEQ, D_MODEL)
    y_sample = y_s.reshape(DEC_BATCH, DEC_SEQ, D_MODEL)
    new_k = k32.reshape(BATCH, 1, SEQ, N_KV_HEADS, HEAD_DIM)
    new_v = v32.reshape(BATCH, 1, SEQ, N_KV_HEADS, HEAD_DIM)
    return (y_prompt, y_sample, new_k, new_v)
```

```python
import jax
import jax.numpy as jnp
from jax import lax
from jax.experimental import pallas as pl
from jax.experimental.pallas import tpu as pltpu

F32 = jnp.float32
BF16 = jnp.bfloat16
I32 = jnp.int32

D_MODEL = 2048
BATCH = 16
SEQ = 256
DEC_BATCH = 2
DEC_SEQ = 4096
PAST_LEN = 512
GRID_W = 64
N_HEADS = 16
N_KV_HEADS = 4
HEAD_DIM = 128
GROUP = N_HEADS // N_KV_HEADS
ATTN_WIDTH = N_HEADS * HEAD_DIM
KV_WIDTH = N_KV_HEADS * HEAD_DIM
WINDOW = 128
CONV_WIDTH = D_MODEL
N_EXPERTS = 64
TOP_K = 8
N_EXPERT_GROUPS = 8
GROUP_SIZE = N_EXPERTS // N_EXPERT_GROUPS
TOPK_GROUPS = 4
D_EXPERT = 512
D_SHARED = 512
ROUTED_SCALE = 2.5
ROPE_THETA = 10000.0
EPS = 1e-6
NEG_INF = -1e30
ATTN_SCALE = HEAD_DIM ** -0.5
LOG2E = 1.4426950408889634

T_P = BATCH * SEQ
T_S = DEC_BATCH * DEC_SEQ
T_ALL = T_P + T_S
MOD_GROUP = 4096
N_MOD = 1 + DEC_BATCH
assert T_P == MOD_GROUP and DEC_SEQ == MOD_GROUP

COL_CB, COL_CC, COL_CX, COL_Q, COL_GC, COL_GA = (i * D_MODEL for i in range(6))
COL_K = 6 * D_MODEL
COL_V = COL_K + KV_WIDTH
IN_WIDTH = COL_V + KV_WIDTH
W_COL_KV = 3 * CONV_WIDTH + ATTN_WIDTH

TM_IN = 1024
TN_IN = 1024
TM = 256
TM_PRE = 512
TM_C = 128
TILE_E = 512
N_SORTED = T_ALL * TOP_K + N_EXPERTS * TILE_E
N_TILES_E = N_SORTED // TILE_E
TMAP_W = 512
CAST_ROWS = 512
VMEM_LIMIT = 56 * 1024 * 1024


def _cparams(sem):
    return pltpu.CompilerParams(dimension_semantics=sem, vmem_limit_bytes=VMEM_LIMIT)


def _silu(x):
    return x * jax.nn.sigmoid(x)


def _load_cast_weight(w_hbm, w_scr, stage, sem):
    rows = stage.shape[0]
    for c in range(w_hbm.shape[0] // rows):
        cp = pltpu.make_async_copy(w_hbm.at[pl.ds(c * rows, rows)], stage, sem)
        cp.start()
        cp.wait()
        w_scr[pl.ds(c * rows, rows), :] = stage[...].astype(BF16)


ADA_TN = 1024
ADA_CHUNK = 256


def _ada_kernel(ct_ref, w_ref, b_ref, o_ref):
    tn = w_ref.shape[1]

    def body(c, accs):
        k0 = pl.multiple_of(c * ADA_CHUNK, ADA_CHUNK)
        wch = w_ref[pl.ds(k0, ADA_CHUNK), :]
        sch = _silu(ct_ref[pl.ds(k0, ADA_CHUNK), :])
        out = []
        for r in range(N_MOD):
            p = wch * sch[:, r:r + 1]
            out.append(accs[r] + p.reshape(ADA_CHUNK // 8, 8, tn).sum(axis=0))
        return tuple(out)

    accs = lax.fori_loop(0, D_MODEL // ADA_CHUNK, body,
                         tuple(jnp.zeros((8, tn), F32) for _ in range(N_MOD)))
    o_ref[...] = jnp.zeros_like(o_ref)
    for r in range(N_MOD):
        o_ref[r:r + 1, :] = jnp.sum(accs[r], axis=0, keepdims=True) + b_ref[...]


def _ada(cond_t, w_ada, b_ada):
    n = w_ada.shape[1]
    return pl.pallas_call(
        _ada_kernel,
        grid=(n // ADA_TN,),
        in_specs=[pl.BlockSpec((D_MODEL, 8), lambda j: (0, 0)),
                  pl.BlockSpec((D_MODEL, ADA_TN), lambda j: (0, j)),
                  pl.BlockSpec((1, ADA_TN), lambda j: (0, j))],
        out_specs=pl.BlockSpec((8, ADA_TN), lambda j: (0, j)),
        out_shape=jax.ShapeDtypeStruct((8, n), F32),
        compiler_params=_cparams(("arbitrary",)),
        name="ada",
    )(cond_t, w_ada, b_ada)


def _rms_mod(x, g, shift, scale):
    ms = jnp.mean(x * x, axis=-1, keepdims=True)
    y = x * lax.rsqrt(ms + EPS) * g
    return y * (1.0 + scale) + shift


def _two_group_specs(tile, width):
    n_ctx = T_P // tile
    return (pl.BlockSpec((tile, width), lambda i: (jnp.minimum(i, n_ctx - 1), 0)),
            pl.BlockSpec((tile, width), lambda i: (jnp.maximum(i - n_ctx, 0), 0)))


def _prenorm_kernel(xp_ref, xs_ref, mod_ref, g_ref, h_ref):
    i = pl.program_id(0)
    shift = mod_ref[0, 0:1, :]
    scale = mod_ref[0, 1:2, :]
    g = g_ref[...]

    @pl.when(i < T_P // TM_PRE)
    def _():
        h_ref[...] = _rms_mod(xp_ref[...], g, shift, scale).astype(BF16)

    @pl.when(i >= T_P // TM_PRE)
    def _():
        h_ref[...] = _rms_mod(xs_ref[...], g, shift, scale).astype(BF16)


def _prenorm(xp, xs, mods3, g_mix):
    tiles_per_group = MOD_GROUP // TM_PRE
    sp, ss = _two_group_specs(TM_PRE, D_MODEL)
    return pl.pallas_call(
        _prenorm_kernel,
        grid=(T_ALL // TM_PRE,),
        in_specs=[sp, ss,
                  pl.BlockSpec((1, 6, D_MODEL), lambda i: (i // tiles_per_group, 0, 0)),
                  pl.BlockSpec((1, D_MODEL), lambda i: (0, 0))],
        out_specs=pl.BlockSpec((TM_PRE, D_MODEL), lambda i: (i, 0)),
        out_shape=jax.ShapeDtypeStruct((T_ALL, D_MODEL), BF16),
        compiler_params=_cparams(("arbitrary",)),
        name="prenorm",
    )(xp, xs, mods3, g_mix)


def _head_norm_rope(a, g, cos2, sin2, scale):
    ms = jnp.mean(a * a, axis=-1, keepdims=True)
    a = a * lax.rsqrt(ms + EPS) * g
    a = a * cos2 + pltpu.roll(a, HEAD_DIM // 2, 1) * sin2
    return a * scale


N_J = IN_WIDTH // TN_IN
J_Q0 = COL_Q // TN_IN
J_Q1 = J_Q0 + ATTN_WIDTH // TN_IN
J_KV = COL_K // TN_IN
assert J_KV == N_J - 1
N_CTX_IN = T_P // TM_IN


def _inproj_kernel(h_ref, w_ref, qg_ref, kg_ref, cos_ref, sin_ref, o_ref, k_ref, v_ref, w_scr):
    j = pl.program_id(0)
    i = pl.program_id(1)

    @pl.when(i == 0)
    def _():
        for c in range(D_MODEL // CAST_ROWS):
            rows = pl.ds(c * CAST_ROWS, CAST_ROWS)
            w_scr[rows, :] = w_ref[rows, :].astype(BF16)

    acc = jnp.dot(h_ref[...], w_scr[...], preferred_element_type=F32)
    is_q = jnp.logical_and(j >= J_Q0, j < J_Q1)
    is_kv = j == J_KV

    @pl.when(jnp.logical_not(jnp.logical_or(is_q, is_kv)))
    def _():
        o_ref[...] = acc.astype(BF16)

    @pl.when(is_q)
    def _():
        cos2 = cos_ref[...]
        sin2 = sin_ref[...]
        g = qg_ref[...]
        for h in range(TN_IN // HEAD_DIM):
            cols = slice(h * HEAD_DIM, (h + 1) * HEAD_DIM)
            o_ref[:, cols] = _head_norm_rope(acc[:, cols], g, cos2, sin2, ATTN_SCALE * LOG2E).astype(BF16)

    @pl.when(is_kv)
    def _():
        cos2 = cos_ref[...]
        sin2 = sin_ref[...]
        g = kg_ref[...]
        is_ctx = i < N_CTX_IN
        for h in range(N_KV_HEADS):
            cols = slice(h * HEAD_DIM, (h + 1) * HEAD_DIM)
            kh = _head_norm_rope(acc[:, cols], g, cos2, sin2, 1.0)
            o_ref[:, cols] = kh.astype(BF16)

            @pl.when(is_ctx)
            def _():
                k_ref[:, cols] = kh

        v = acc[:, KV_WIDTH:]
        o_ref[:, KV_WIDTH:] = v.astype(BF16)

        @pl.when(is_ctx)
        def _():
            v_ref[...] = v


def _inproj(h, w_in, qg, kg, cos_tab, sin_tab):
    pos_tiles = DEC_SEQ // TM_IN

    def tab_map(j, i):
        return (jnp.where(i < N_CTX_IN, 0, 1 + i % pos_tiles), 0)

    def w_map(j, i):
        return (0, jnp.where(j < W_COL_KV // TN_IN, j, jnp.where(j == J_KV, W_COL_KV // TN_IN, j + 1)))

    def kv_map(j, i):
        return (jnp.where(j == J_KV, jnp.minimum(i, N_CTX_IN - 1), 0), 0)

    return pl.pallas_call(
        _inproj_kernel,
        grid=(N_J, T_ALL // TM_IN),
        in_specs=[pl.BlockSpec((TM_IN, D_MODEL), lambda j, i: (i, 0)),
                  pl.BlockSpec((D_MODEL, TN_IN), w_map, pipeline_mode=pl.Buffered(1)),
                  pl.BlockSpec((1, HEAD_DIM), lambda j, i: (0, 0)),
                  pl.BlockSpec((1, HEAD_DIM), lambda j, i: (0, 0)),
                  pl.BlockSpec((TM_IN, HEAD_DIM), tab_map),
                  pl.BlockSpec((TM_IN, HEAD_DIM), tab_map)],
        out_specs=[pl.BlockSpec((TM_IN, TN_IN), lambda j, i: (i, j)),
                   pl.BlockSpec((TM_IN, KV_WIDTH), kv_map),
                   pl.BlockSpec((TM_IN, KV_WIDTH), kv_map)],
        out_shape=[jax.ShapeDtypeStruct((T_ALL, IN_WIDTH), BF16),
                   jax.ShapeDtypeStruct((T_P, KV_WIDTH), F32),
                   jax.ShapeDtypeStruct((T_P, KV_WIDTH), F32)],
        scratch_shapes=[pltpu.VMEM((D_MODEL, TN_IN), BF16)],
        compiler_params=_cparams(("arbitrary", "arbitrary")),
        name="inproj",
    )(h, w_in, qg, kg, cos_tab, sin_tab)


def _dot_nt(a, b):
    return lax.dot_general(a, b, (((1,), (1,)), ((), ())), preferred_element_type=F32)


def _attn_p_kernel(sink_ref, q_ref, k_ref, v_ref, o_ref):
    for kh in range(N_KV_HEADS):
        kcols = slice(kh * HEAD_DIM, (kh + 1) * HEAD_DIM)
        k = k_ref[:, kcols]
        v = v_ref[:, kcols]
        for gi in range(GROUP):
            h = kh * GROUP + gi
            cols = slice(h * HEAD_DIM, (h + 1) * HEAD_DIM)
            s = _dot_nt(q_ref[:, cols], k)
            sk = sink_ref[h] * LOG2E
            m = jnp.maximum(jnp.max(s, axis=-1, keepdims=True), sk)
            p = jnp.exp2(s - m)
            den = jnp.sum(p, axis=-1, keepdims=True) + jnp.exp2(sk - m)
            o = jnp.dot(p.astype(BF16), v, preferred_element_type=F32) / den
            o_ref[:, cols] = o.astype(BF16)


def _attn_p(sink, proj):
    return pl.pallas_call(
        _attn_p_kernel,
        grid=(BATCH,),
        in_specs=[pl.BlockSpec(memory_space=pltpu.SMEM),
                  pl.BlockSpec((SEQ, ATTN_WIDTH), lambda b: (b, COL_Q // ATTN_WIDTH)),
                  pl.BlockSpec((SEQ, KV_WIDTH), lambda b: (b, COL_K // KV_WIDTH)),
                  pl.BlockSpec((SEQ, KV_WIDTH), lambda b: (b, COL_V // KV_WIDTH))],
        out_specs=pl.BlockSpec((SEQ, ATTN_WIDTH), lambda b: (b, 0)),
        out_shape=jax.ShapeDtypeStruct((T_P, ATTN_WIDTH), BF16),
        compiler_params=_cparams(("arbitrary",)),
        name="attn_p",
    )(sink, proj, proj, proj)


QB = 256
BAND = QB + 2 * WINDOW


def _attn_s_kernel(sink_ref, q_ref, k_ref, v_ref, kc_ref, vc_ref, o_ref):
    i = pl.program_id(1)
    start = jnp.clip(i * QB - WINDOW, 0, DEC_SEQ - BAND)
    start = pl.multiple_of(start, WINDOW)
    qpos = i * QB + lax.broadcasted_iota(I32, (QB, BAND), 0)
    kpos = start + lax.broadcasted_iota(I32, (QB, BAND), 1)
    valid = jnp.abs(qpos - kpos) <= WINDOW
    valid_g = jnp.concatenate([valid] * GROUP, axis=0)
    for kh in range(N_KV_HEADS):
        kcols = slice(kh * HEAD_DIM, (kh + 1) * HEAD_DIM)
        kb = k_ref[pl.ds(start, BAND), kcols]
        kc = kc_ref[0, :, kcols].astype(BF16)
        ones = jnp.ones((BAND, HEAD_DIM), BF16)
        vb1 = jnp.concatenate([v_ref[pl.ds(start, BAND), kcols], ones], axis=1)
        vc1 = jnp.concatenate([vc_ref[0, :, kcols].astype(BF16), ones[:PAST_LEN]], axis=1)
        heads = [kh * GROUP + gi for gi in range(GROUP)]
        q = jnp.concatenate([q_ref[:, h * HEAD_DIM:(h + 1) * HEAD_DIM] for h in heads], axis=0)
        sk = jnp.concatenate([jnp.full((QB, 1), sink_ref[h] * LOG2E, F32) for h in heads], axis=0)
        s_loc = jnp.where(valid_g, _dot_nt(q, kb), NEG_INF)
        s_ctx = _dot_nt(q, kc)
        m = jnp.maximum(jnp.maximum(jnp.max(s_loc, axis=-1, keepdims=True),
                                    jnp.max(s_ctx, axis=-1, keepdims=True)), sk)
        p_loc = jnp.exp2(s_loc - m).astype(BF16)
        p_ctx = jnp.exp2(s_ctx - m).astype(BF16)
        o1 = (jnp.dot(p_loc, vb1, preferred_element_type=F32)
              + jnp.dot(p_ctx, vc1, preferred_element_type=F32))
        den = o1[:, HEAD_DIM:HEAD_DIM + 1] + jnp.exp2(sk - m)
        o = o1[:, :HEAD_DIM] / den
        for gi, h in enumerate(heads):
            o_ref[:, h * HEAD_DIM:(h + 1) * HEAD_DIM] = o[gi * QB:(gi + 1) * QB, :].astype(BF16)


def _attn_s(sink, proj, kc, vc):
    nq = DEC_SEQ // QB
    first = T_P // QB
    seq_blk = T_P // DEC_SEQ
    return pl.pallas_call(
        _attn_s_kernel,
        grid=(DEC_BATCH, nq),
        in_specs=[pl.BlockSpec(memory_space=pltpu.SMEM),
                  pl.BlockSpec((QB, ATTN_WIDTH), lambda b, i: (first + b * nq + i, COL_Q // ATTN_WIDTH)),
                  pl.BlockSpec((DEC_SEQ, KV_WIDTH), lambda b, i: (seq_blk + b, COL_K // KV_WIDTH)),
                  pl.BlockSpec((DEC_SEQ, KV_WIDTH), lambda b, i: (seq_blk + b, COL_V // KV_WIDTH)),
                  pl.BlockSpec((1, PAST_LEN, KV_WIDTH), lambda b, i: (b, 0, 0)),
                  pl.BlockSpec((1, PAST_LEN, KV_WIDTH), lambda b, i: (b, 0, 0))],
        out_specs=pl.BlockSpec((QB, ATTN_WIDTH), lambda b, i: (b * nq + i, 0)),
        out_shape=jax.ShapeDtypeStruct((T_S, ATTN_WIDTH), BF16),
        compiler_params=_cparams(("arbitrary", "arbitrary")),
        name="attn_s",
    )(sink, proj, proj, proj, kc, vc)


HALO = 16


def _conv_kernel(cb_ref, cc_ref, cx_ref, ccp_ref, cxp_ref, ccn_ref, cxn_ref, gc_ref, cw_ref, w_hbm, o_ref,
                 w_scr, stage, sem):
    i = pl.program_id(0)

    @pl.when(i == 0)
    def _():
        _load_cast_weight(w_hbm, w_scr, stage, sem)

    n_ctx_tiles = T_P // TM
    per_seq = DEC_SEQ // TM
    is_first = jnp.logical_or(i < n_ctx_tiles, (i - n_ctx_tiles) % per_seq == 0)
    is_last = jnp.logical_or(i < n_ctx_tiles, (i - n_ctx_tiles) % per_seq == per_seq - 1)
    p = cc_ref[...].astype(F32) * cx_ref[...].astype(F32)
    prev_row = ccp_ref[HALO - 1:HALO, :].astype(F32) * cxp_ref[HALO - 1:HALO, :].astype(F32)
    next_row = ccn_ref[0:1, :].astype(F32) * cxn_ref[0:1, :].astype(F32)
    prev_row = jnp.where(is_first, 0.0, prev_row)
    next_row = jnp.where(is_last, 0.0, next_row)
    rows = lax.broadcasted_iota(I32, (TM, 1), 0)
    p_prev = jnp.where(rows == 0, prev_row, pltpu.roll(p, 1, 0))
    p_next = jnp.where(rows == TM - 1, next_row, pltpu.roll(p, TM - 1, 0))
    conv = p_prev * cw_ref[0:1, :] + p * cw_ref[1:2, :] + p_next * cw_ref[2:3, :]
    u = cb_ref[...].astype(F32) * conv
    y = jnp.dot(u.astype(BF16), w_scr[...], preferred_element_type=F32)
    o_ref[...] = (jax.nn.sigmoid(gc_ref[...].astype(F32)) * y).astype(BF16)


def _conv(proj, conv_w, w_conv_out):
    hb = TM // HALO
    last_hb = T_ALL // HALO - 1
    wide = lambda c: pl.BlockSpec((TM, D_MODEL), lambda i: (i, c // D_MODEL))
    prev = lambda c: pl.BlockSpec((HALO, D_MODEL), lambda i: (jnp.maximum(i * hb - 1, 0), c // D_MODEL))
    nxt = lambda c: pl.BlockSpec((HALO, D_MODEL), lambda i: (jnp.minimum((i + 1) * hb, last_hb), c // D_MODEL))
    return pl.pallas_call(
        _conv_kernel,
        grid=(T_ALL // TM,),
        in_specs=[wide(COL_CB), wide(COL_CC), wide(COL_CX),
                  prev(COL_CC), prev(COL_CX), nxt(COL_CC), nxt(COL_CX),
                  wide(COL_GC),
                  pl.BlockSpec((3, CONV_WIDTH), lambda i: (0, 0)),
                  pl.BlockSpec(memory_space=pl.ANY)],
        out_specs=pl.BlockSpec((TM, D_MODEL), lambda i: (i, 0)),
        out_shape=jax.ShapeDtypeStruct((T_ALL, D_MODEL), BF16),
        scratch_shapes=[pltpu.VMEM((CONV_WIDTH, D_MODEL), BF16),
                        pltpu.VMEM((CAST_ROWS, D_MODEL), F32),
                        pltpu.SemaphoreType.DMA(())],
        compiler_params=_cparams(("arbitrary",)),
        name="conv",
    )(proj, proj, proj, proj, proj, proj, proj, proj, conv_w, w_conv_out)


def _merge_kernel(xp_ref, xs_ref, attn_p_ref, attn_s_ref, z1_ref, ga_ref, mod_ref, g2_ref, wr_ref,
                  wao_hbm, wo_hbm, x1_ref, h2_ref, lg_ref, wao_scr, wo_scr, stage, sem):
    i = pl.program_id(0)

    @pl.when(i == 0)
    def _():
        _load_cast_weight(wao_hbm, wao_scr, stage, sem)
        _load_cast_weight(wo_hbm, wo_scr, stage, sem)

    is_ctx = i < T_P // TM
    attn = jnp.where(is_ctx, attn_p_ref[...], attn_s_ref[...])
    x = jnp.where(is_ctx, xp_ref[...], xs_ref[...])
    ya = jnp.dot(attn, wao_scr[...], preferred_element_type=F32)
    z = z1_ref[...].astype(F32) + jax.nn.sigmoid(ga_ref[...].astype(F32)) * ya
    mix = jnp.dot(z.astype(BF16), wo_scr[...], preferred_element_type=F32)
    x1 = x + mod_ref[0, 2:3, :] * mix
    x1_ref[...] = x1
    h = _rms_mod(x1, g2_ref[...], mod_ref[0, 3:4, :], mod_ref[0, 4:5, :])
    h2_ref[...] = h
    h_hi = h.astype(BF16)
    h_lo = (h - h_hi.astype(F32)).astype(BF16)
    wr = wr_ref[...]
    wr_hi32 = wr.astype(BF16).astype(F32)
    wr_lo32 = (wr - wr_hi32).astype(BF16).astype(F32)
    w_both = (wr_hi32 + pltpu.roll(wr_lo32, N_EXPERTS, 1)).astype(BF16)
    both = jnp.dot(h_hi, w_both, preferred_element_type=F32)
    lg_ref[...] = (both + pltpu.roll(both, N_EXPERTS, 1)
                   + jnp.dot(h_lo, wr_hi32.astype(BF16), preferred_element_type=F32))


def _merge(xp, xs, attn_p, attn_s, z1, proj, mods3, g_ffn, wr_pad, w_attn_out, w_out):
    tiles_per_group = MOD_GROUP // TM
    const = lambda shape: pl.BlockSpec(shape, lambda i: (0, 0))
    row = lambda w: pl.BlockSpec((TM, w), lambda i: (i, 0))
    xsp, xss = _two_group_specs(TM, D_MODEL)
    asp, ass = _two_group_specs(TM, ATTN_WIDTH)
    return pl.pallas_call(
        _merge_kernel,
        grid=(T_ALL // TM,),
        in_specs=[xsp, xss, asp, ass, row(D_MODEL),
                  pl.BlockSpec((TM, D_MODEL), lambda i: (i, COL_GA // D_MODEL)),
                  pl.BlockSpec((1, 6, D_MODEL), lambda i: (i // tiles_per_group, 0, 0)),
                  const((1, D_MODEL)), const((D_MODEL, 128)),
                  pl.BlockSpec(memory_space=pl.ANY), pl.BlockSpec(memory_space=pl.ANY)],
        out_specs=[row(D_MODEL), row(D_MODEL), row(128)],
        out_shape=[jax.ShapeDtypeStruct((T_ALL, D_MODEL), F32),
                   jax.ShapeDtypeStruct((T_ALL, D_MODEL), F32),
                   jax.ShapeDtypeStruct((T_ALL, 128), F32)],
        scratch_shapes=[pltpu.VMEM((ATTN_WIDTH, D_MODEL), BF16),
                        pltpu.VMEM((D_MODEL, D_MODEL), BF16),
                        pltpu.VMEM((CAST_ROWS, D_MODEL), F32),
                        pltpu.SemaphoreType.DMA(())],
        compiler_params=_cparams(("arbitrary",)),
        name="merge",
    )(xp, xs, attn_p, attn_s, z1, proj, mods3, g_ffn, wr_pad, w_attn_out, w_out)


def _route1_kernel(lg_ref, bias_ref, enc_ref, cnt_ref):
    i = pl.program_id(0)
    lt = lg_ref[...].T[:N_EXPERTS, :]
    scores = jax.nn.sigmoid(lt)
    biased = scores + bias_ref[...]
    b3 = biased.reshape(N_EXPERT_GROUPS, GROUP_SIZE, TM)
    mi = lax.broadcasted_iota(I32, b3.shape, 1)
    m1 = jnp.max(b3, axis=1, keepdims=True)
    idx1 = jnp.min(jnp.where(b3 == m1, mi, GROUP_SIZE), axis=1, keepdims=True)
    m2 = jnp.max(jnp.where(mi == idx1, -jnp.inf, b3), axis=1, keepdims=True)
    gs = (m1 + m2).reshape(N_EXPERT_GROUPS, TM)
    gidx = lax.broadcasted_iota(I32, gs.shape, 0)
    grank = jnp.zeros(gs.shape, I32)
    for j in range(N_EXPERT_GROUPS):
        gj = gs[j:j + 1, :]
        beats = jnp.logical_or(gj > gs, jnp.logical_and(gj == gs, j < gidx))
        grank = grank + beats.astype(I32)
    gsel = grank < TOPK_GROUPS
    emask = jnp.broadcast_to(gsel[:, None, :], b3.shape).reshape(N_EXPERTS, TM)
    masked = jnp.where(emask, biased, NEG_INF)
    eidx = lax.broadcasted_iota(I32, masked.shape, 0)
    erank = jnp.zeros(masked.shape, I32)
    for j in range(N_EXPERTS):
        vj = masked[j:j + 1, :]
        beats = jnp.logical_or(vj > masked, jnp.logical_and(vj == masked, j < eidx))
        erank = erank + beats.astype(I32)
    sel = erank < TOP_K
    wsel = jnp.where(sel, scores, 0.0)
    den = jnp.sum(wsel, axis=0, keepdims=True)
    wts = wsel / den * ROUTED_SCALE
    enc_ref[...] = jnp.where(sel, wts, -1.0)

    @pl.when(i == 0)
    def _():
        cnt_ref[...] = jnp.zeros_like(cnt_ref)

    cnt = jnp.sum(sel.astype(F32), axis=1, keepdims=True)
    cnt_ref[...] += jnp.broadcast_to(cnt, cnt_ref.shape)


def _route1(logits, bias_col):
    return pl.pallas_call(
        _route1_kernel,
        grid=(T_ALL // TM,),
        in_specs=[pl.BlockSpec((TM, 128), lambda i: (i, 0)),
                  pl.BlockSpec((N_EXPERTS, 1), lambda i: (0, 0))],
        out_specs=[pl.BlockSpec((N_EXPERTS, TM), lambda i: (0, i)),
                   pl.BlockSpec((N_EXPERTS, 128), lambda i: (0, 0))],
        out_shape=[jax.ShapeDtypeStruct((N_EXPERTS, T_ALL), F32),
                   jax.ShapeDtypeStruct((N_EXPERTS, 128), F32)],
        compiler_params=_cparams(("arbitrary",)),
        name="route1",
    )(logits, bias_col)


def _route2_positions(enc_ref, cnt_ref, meta_ref, tmap_ref, carry_ref):
    enc = enc_ref[...]
    sel = enc >= 0.0
    wts = jnp.maximum(enc, 0.0)
    sel_b = sel.astype(BF16)
    ntile = jnp.ceil(cnt_ref[...] * (1.0 / TILE_E))
    er = lax.broadcasted_iota(I32, (N_EXPERTS, N_EXPERTS), 0)
    ec = lax.broadcasted_iota(I32, (N_EXPERTS, N_EXPERTS), 1)
    lower = (ec < er).astype(BF16)
    off_t = jnp.dot(lower, ntile.astype(BF16), preferred_element_type=F32)
    tr = lax.broadcasted_iota(I32, (TM, TM), 0)
    tc = lax.broadcasted_iota(I32, (TM, TM), 1)
    upper = (tr < tc).astype(BF16)
    rank = jnp.dot(sel_b, upper, preferred_element_type=F32) + carry_ref[:, 0:1]
    carry_ref[...] += jnp.broadcast_to(jnp.sum(sel.astype(F32), axis=1, keepdims=True), carry_ref.shape)
    pos = off_t[:, 0:1] * float(TILE_E) + rank
    slot = jnp.dot(lower, sel_b, preferred_element_type=F32)
    rows = []
    for k in range(TOP_K):
        mk = jnp.logical_and(sel, slot == float(k))
        rows.append(jnp.sum(jnp.where(mk, wts, 0.0), axis=0, keepdims=True))
    for k in range(TOP_K):
        mk = jnp.logical_and(sel, slot == float(k))
        rows.append(jnp.sum(jnp.where(mk, pos, 0.0), axis=0, keepdims=True))
    rows.append(jnp.zeros((128 - 2 * TOP_K, TM), F32))
    meta_ref[...] = jnp.concatenate(rows, axis=0).T
    nt = ntile[:, 0:1]
    end_t = off_t[:, 0:1] + nt
    lane = lax.broadcasted_iota(I32, (N_EXPERTS, TMAP_W), 1)
    te = jnp.sum((end_t <= lane.astype(F32)).astype(F32), axis=0, keepdims=True)
    te = jnp.minimum(te, float(N_EXPERTS - 1))
    nact = jnp.sum(nt, axis=0, keepdims=True)
    erow = lax.broadcasted_iota(I32, (N_EXPERTS, TMAP_W), 0).astype(F32)
    nxt = jnp.sum(jnp.where(erow == te, end_t, 0.0), axis=0, keepdims=True)
    tmap_ref[...] = jnp.zeros_like(tmap_ref)
    tmap_ref[0:1, :] = te.astype(I32)
    tmap_ref[1:2, :] = jnp.broadcast_to(nact, (1, TMAP_W)).astype(I32)
    tmap_ref[2:3, :] = nxt.astype(I32)
    return jnp.concatenate(rows[TOP_K:2 * TOP_K], axis=0).astype(I32)


N_TOK_TILES = T_ALL // TM


def _route2_kernel(enc_ref, cnt_ref, init_hbm, meta_ref, tmap_ref, inv_hbm,
                   carry_ref, posv0, posv1, poss0, poss1, inv_smem, psem, isem):
    i = pl.program_id(0)
    posv = (posv0, posv1)
    poss = (poss0, poss1)
    per_token = (1 << TOKEN_BITS) + 1
    per_slot = T_ALL << TOKEN_BITS

    @pl.when(i == 0)
    def _():
        carry_ref[...] = jnp.zeros_like(carry_ref)
        cp = pltpu.make_async_copy(init_hbm, inv_smem, isem)
        cp.start()
        cp.wait()

    def positions(s):
        posv[s][...] = _route2_positions(enc_ref, cnt_ref, meta_ref, tmap_ref, carry_ref)
        pltpu.make_async_copy(posv[s], poss[s], psem.at[s]).start()

    def invert(s):
        pltpu.make_async_copy(posv[s], poss[s], psem.at[s]).wait()
        base = (i - 1) * (TM * per_token)
        for t in range(TM):
            for k in range(TOP_K):
                inv_smem[poss[s][k, t]] = base + (t * per_token + k * per_slot)

    @pl.when(i == 0)
    def _():
        positions(0)

    for s in range(2):
        @pl.when(jnp.logical_and(jnp.logical_and(i >= 1, i < N_TOK_TILES), i % 2 == s))
        def _():
            invert(1 - s)
            positions(s)

    @pl.when(i == N_TOK_TILES)
    def _():
        invert((N_TOK_TILES - 1) % 2)
        cp = pltpu.make_async_copy(inv_smem, inv_hbm, isem)
        cp.start()
        cp.wait()


def _route2(enc, cnt, inv_init):
    last = N_TOK_TILES - 1
    hbm = pl.BlockSpec(memory_space=pl.ANY)
    return pl.pallas_call(
        _route2_kernel,
        grid=(N_TOK_TILES + 1,),
        in_specs=[pl.BlockSpec((N_EXPERTS, TM), lambda i: (0, jnp.minimum(i, last))),
                  pl.BlockSpec((N_EXPERTS, 128), lambda i: (0, 0)),
                  hbm],
        out_specs=[pl.BlockSpec((TM, 128), lambda i: (jnp.minimum(i, last), 0)),
                   pl.BlockSpec((8, TMAP_W), lambda i: (0, 0)),
                   hbm],
        out_shape=[jax.ShapeDtypeStruct((T_ALL, 128), F32),
                   jax.ShapeDtypeStruct((8, TMAP_W), I32),
                   jax.ShapeDtypeStruct((N_SORTED,), I32)],
        scratch_shapes=[pltpu.VMEM((N_EXPERTS, 128), F32),
                        pltpu.VMEM((TOP_K, TM), I32), pltpu.VMEM((TOP_K, TM), I32),
                        pltpu.SMEM((TOP_K, TM), I32), pltpu.SMEM((TOP_K, TM), I32),
                        pltpu.SMEM((N_SORTED,), I32),
                        pltpu.SemaphoreType.DMA((2,)), pltpu.SemaphoreType.DMA(())],
        compiler_params=_cparams(("arbitrary",)),
        name="route2",
    )(enc, cnt, inv_init)


TOKEN_BITS = 14
TOKEN_MASK = (1 << TOKEN_BITS) - 1
NBUF = 3
TRASH_BASE = TOP_K * T_ALL
YS_ROWS = TRASH_BASE + NBUF * TILE_E
assert T_ALL <= 1 << TOKEN_BITS and YS_ROWS << TOKEN_BITS < 1 << 31


def _pad_codes():
    r = jnp.arange(N_SORTED, dtype=I32)
    q = r % TILE_E
    out_row = TRASH_BASE + ((r // TILE_E) % NBUF) * TILE_E + q
    return (out_row << TOKEN_BITS) | q


N_CHUNK = 256
GU_PIECES = D_EXPERT // N_CHUNK
DN_PIECES = D_MODEL // N_CHUNK
N_PIECES = GU_PIECES + DN_PIECES

PIECE_WORK = (D_MODEL * 2,) * GU_PIECES + (D_EXPERT,) * DN_PIECES
PIECE_ROWS = tuple(round(TILE_E * sum(PIECE_WORK[:p]) / sum(PIECE_WORK)) for p in range(N_PIECES + 1))


def _moe_kernel(te_ref, na_ref, nxt_ref, inv_ref, h2_hbm, wg_hbm, wu_hbm, wd_hbm, ys_hbm,
                xbuf0, xbuf1, xbuf2, ybuf0, ybuf1, ybuf2, xb, act, wg_f32, wu_f32, wd_f32,
                wg_scr, wu_scr, wd_scr, gsem, ssem, wsem):
    i = pl.program_id(0)
    na = na_ref[0]
    active = i < na
    xbufs = (xbuf0, xbuf1, xbuf2)
    ybufs = (ybuf0, ybuf1, ybuf2)

    def weight_copies(e):
        return (pltpu.make_async_copy(wg_hbm.at[e], wg_f32, wsem),
                pltpu.make_async_copy(wu_hbm.at[e], wu_f32, wsem),
                pltpu.make_async_copy(wd_hbm.at[e], wd_f32, wsem))

    def gather_row(tile, s, r):
        tok = inv_ref[tile * TILE_E + r] & TOKEN_MASK
        pltpu.make_async_copy(h2_hbm.at[pl.ds(tok, 1)], xbufs[s].at[pl.ds(r, 1)], gsem.at[s]).start()

    def scatter_row(dst, s, r):
        pltpu.make_async_copy(ybufs[s].at[pl.ds(r, 1)], ys_hbm.at[pl.ds(dst, 1)], ssem.at[s]).start()

    def wait_gather(s):
        pltpu.make_async_copy(h2_hbm.at[pl.ds(0, TILE_E)], xbufs[s], gsem.at[s]).wait()

    def wait_scatter(s):
        pltpu.make_async_copy(ybufs[s], ys_hbm.at[pl.ds(0, TILE_E)], ssem.at[s]).wait()

    @pl.when(i == 0)
    def _():
        zeros = ybufs[NBUF - 1]
        zeros[...] = jnp.zeros_like(zeros)
        for m in range(NBUF):
            cp = pltpu.make_async_copy(zeros, ys_hbm.at[pl.ds(TRASH_BASE + m * TILE_E, TILE_E)], ssem.at[0])
            cp.start()
            cp.wait()
        for t in range(2):
            def body(r, carry):
                gather_row(t, t, r)
                return carry

            lax.fori_loop(0, TILE_E, body, 0)

        for cp in weight_copies(te_ref[0]):
            cp.start()

    new_expert = jnp.logical_or(i == 0, te_ref[i] != te_ref[jnp.maximum(i - 1, 0)])

    @pl.when(jnp.logical_and(active, new_expert))
    def _():
        for cp in weight_copies(0):
            cp.wait()
        for c in range(D_MODEL // CAST_ROWS):
            rows = pl.ds(c * CAST_ROWS, CAST_ROWS)
            wg_scr[rows, :] = wg_f32[rows, :].astype(BF16)
            wu_scr[rows, :] = wu_f32[rows, :].astype(BF16)
        wd_scr[...] = wd_f32[...].astype(BF16)
        nxt_tile = nxt_ref[i]

        @pl.when(nxt_tile < na)
        def _():
            for cp in weight_copies(te_ref[nxt_tile]):
                cp.start()

    def compute_tile(slot, write_back):
        s_next = (slot + 2) % NBUF
        s_prev = (slot - 1) % NBUF
        wait_gather(slot)

        @pl.when(i >= 3)
        def _():
            wait_scatter(slot)

        xb[...] = xbufs[slot][...].astype(BF16)
        nxt = jnp.minimum(i + 2, N_TILES_E - 1)
        codes_next = inv_ref.at[pl.ds(pl.multiple_of(nxt * TILE_E, TILE_E), TILE_E)]
        codes_prev = inv_ref.at[pl.ds(pl.multiple_of(jnp.maximum(i - 1, 0) * TILE_E, TILE_E), TILE_E)]

        def issue(piece):
            for r in range(PIECE_ROWS[piece], PIECE_ROWS[piece + 1]):
                tok = codes_next[r] & TOKEN_MASK
                pltpu.make_async_copy(h2_hbm.at[pl.ds(tok, 1)], xbufs[s_next].at[pl.ds(r, 1)],
                                      gsem.at[s_next]).start()
                if write_back:
                    scatter_row(codes_prev[r] >> TOKEN_BITS, s_prev, r)

        for c in range(GU_PIECES):
            cols = slice(c * N_CHUNK, (c + 1) * N_CHUNK)
            g = jnp.dot(xb[...], wg_scr[:, cols], preferred_element_type=F32)
            u = jnp.dot(xb[...], wu_scr[:, cols], preferred_element_type=F32)
            act[:, cols] = (_silu(g) * u).astype(BF16)
            issue(c)
        for c in range(DN_PIECES):
            cols = slice(c * N_CHUNK, (c + 1) * N_CHUNK)
            ybufs[slot][:, cols] = jnp.dot(act[...], wd_scr[:, cols], preferred_element_type=F32)
            issue(GU_PIECES + c)

    def drain(slot):
        s_last = (slot - 1) % NBUF

        def body(r, carry):
            scatter_row(inv_ref[(na - 1) * TILE_E + r] >> TOKEN_BITS, s_last, r)
            return carry

        lax.fori_loop(0, TILE_E, body, 0)
        wait_scatter(s_last)

        @pl.when(na >= 2)
        def _():
            wait_scatter((slot - 2) % NBUF)

        @pl.when(na >= 3)
        def _():
            wait_scatter(slot)

        wait_gather(slot)
        wait_gather((slot + 1) % NBUF)

    @pl.when(jnp.logical_and(active, i == 0))
    def _():
        compute_tile(0, write_back=False)

    for s in range(NBUF):
        @pl.when(jnp.logical_and(jnp.logical_and(active, i > 0), i % NBUF == s))
        def _():
            compute_tile(s, write_back=True)

        @pl.when(jnp.logical_and(i == na, i % NBUF == s))
        def _():
            drain(s)


def _moe(tile_expert, n_active, next_tile, inv, h2, wg, wu, wd):
    hbm = pl.BlockSpec(memory_space=pl.ANY)
    grid_spec = pltpu.PrefetchScalarGridSpec(
        num_scalar_prefetch=4,
        grid=(N_TILES_E + 1,),
        in_specs=[hbm, hbm, hbm, hbm],
        out_specs=hbm,
        scratch_shapes=[pltpu.VMEM((TILE_E, D_MODEL), F32)] * (2 * NBUF) + [
                        pltpu.VMEM((TILE_E, D_MODEL), BF16),
                        pltpu.VMEM((TILE_E, D_EXPERT), BF16),
                        pltpu.VMEM((D_MODEL, D_EXPERT), F32),
                        pltpu.VMEM((D_MODEL, D_EXPERT), F32),
                        pltpu.VMEM((D_EXPERT, D_MODEL), F32),
                        pltpu.VMEM((D_MODEL, D_EXPERT), BF16),
                        pltpu.VMEM((D_MODEL, D_EXPERT), BF16),
                        pltpu.VMEM((D_EXPERT, D_MODEL), BF16),
                        pltpu.SemaphoreType.DMA((NBUF,)),
                        pltpu.SemaphoreType.DMA((NBUF,)),
                        pltpu.SemaphoreType.DMA(())],
    )
    return pl.pallas_call(
        _moe_kernel,
        grid_spec=grid_spec,
        out_shape=jax.ShapeDtypeStruct((YS_ROWS, D_MODEL), F32),
        compiler_params=_cparams(("arbitrary",)),
        name="moe",
    )(tile_expert, n_active, next_tile, inv, h2, wg, wu, wd)


def _combine_kernel(meta_ref, *refs):
    ys_refs = refs[:TOP_K]
    h2_ref, x1_ref, mod_ref, wsg_hbm, wsu_hbm, wsd_hbm = refs[TOP_K:TOP_K + 6]
    op_ref, os_ref, wsg_scr, wsu_scr, wsd_scr, stage_a, stage_b, wsem = refs[TOP_K + 6:]
    i = pl.program_id(0)

    @pl.when(i == 0)
    def _():
        _load_cast_weight(wsg_hbm, wsg_scr, stage_a, wsem)
        _load_cast_weight(wsu_hbm, wsu_scr, stage_a, wsem)
        _load_cast_weight(wsd_hbm, wsd_scr, stage_b, wsem)

    h = h2_ref[...].astype(BF16)
    sg = jnp.dot(h, wsg_scr[...], preferred_element_type=F32)
    su = jnp.dot(h, wsu_scr[...], preferred_element_type=F32)
    moe = jnp.dot((_silu(sg) * su).astype(BF16), wsd_scr[...], preferred_element_type=F32)
    for k in range(TOP_K):
        moe = moe + meta_ref[:, k:k + 1] * ys_refs[k][...]
    y = x1_ref[...] + mod_ref[0, 5:6, :] * moe

    @pl.when(i < T_P // TM_C)
    def _():
        op_ref[...] = y

    @pl.when(i >= T_P // TM_C)
    def _():
        os_ref[...] = y


def _combine(meta, ys, h2, x1, mods3, wsg, wsu, wsd):
    tiles_per_group = MOD_GROUP // TM_C
    row = lambda w: pl.BlockSpec((TM_C, w), lambda i: (i, 0))
    slot_rows = lambda k: pl.BlockSpec((TM_C, D_MODEL), lambda i: (k * (T_ALL // TM_C) + i, 0))
    osp, oss = _two_group_specs(TM_C, D_MODEL)
    hbm = pl.BlockSpec(memory_space=pl.ANY)
    return pl.pallas_call(
        _combine_kernel,
        grid=(T_ALL // TM_C,),
        in_specs=[row(128)] + [slot_rows(k) for k in range(TOP_K)] + [
                  row(D_MODEL), row(D_MODEL),
                  pl.BlockSpec((1, 6, D_MODEL), lambda i: (i // tiles_per_group, 0, 0)),
                  hbm, hbm, hbm],
        out_specs=[osp, oss],
        out_shape=[jax.ShapeDtypeStruct((T_P, D_MODEL), F32),
                   jax.ShapeDtypeStruct((T_S, D_MODEL), F32)],
        scratch_shapes=[pltpu.VMEM((D_MODEL, D_SHARED), BF16),
                        pltpu.VMEM((D_MODEL, D_SHARED), BF16),
                        pltpu.VMEM((D_SHARED, D_MODEL), BF16),
                        pltpu.VMEM((D_MODEL, D_SHARED), F32),
                        pltpu.VMEM((D_SHARED, D_MODEL), F32),
                        pltpu.SemaphoreType.DMA(())],
        compiler_params=_cparams(("arbitrary",)),
        name="combine",
    )(meta, *([ys] * TOP_K), h2, x1, mods3, wsg, wsu, wsd)


def _rope_tables():
    rows = DEC_SEQ // GRID_W
    row = jnp.repeat(jnp.arange(rows, dtype=F32), GRID_W)
    col = jnp.tile(jnp.arange(GRID_W, dtype=F32), rows)
    n_freq = HEAD_DIM // 4
    inv = ROPE_THETA ** (-jnp.arange(n_freq, dtype=F32) / n_freq)
    ang = jnp.concatenate([row[:, None] * inv, col[:, None] * inv], axis=-1)
    cos, sin = jnp.cos(ang), jnp.sin(ang)
    cos2 = jnp.concatenate([cos, cos], axis=-1)
    sin2 = jnp.concatenate([-sin, sin], axis=-1)
    cos_tab = jnp.concatenate([jnp.ones((TM_IN, HEAD_DIM), F32), cos2], axis=0)
    sin_tab = jnp.concatenate([jnp.zeros((TM_IN, HEAD_DIM), F32), sin2], axis=0)
    return cos_tab, sin_tab


def kernel(x_prompt, x_sample, cache_k, cache_v, c, c_ctx, w_ada, b_ada, norm_mix_g, norm_ffn_g, w_in, conv_w,
           q_norm_g, k_norm_g, attn_sink, w_conv_out, w_attn_out, w_out, router_w, router_bias, w_exp_gate,
           w_exp_up, w_exp_down, w_sh_gate, w_sh_up, w_sh_down):
    l = 0
    xp = x_prompt.reshape(T_P, D_MODEL)
    xs = x_sample.reshape(T_S, D_MODEL)

    cond = jnp.concatenate([c_ctx[None, :], c, jnp.zeros((8 - N_MOD, D_MODEL), F32)], axis=0)
    mods = _ada(cond.T, w_ada[l], b_ada[l][None, :])
    mods3 = mods[:N_MOD].reshape(N_MOD, 6, D_MODEL)

    h = _prenorm(xp, xs, mods3, norm_mix_g[l][None, :])
    cos_tab, sin_tab = _rope_tables()
    proj, k32, v32 = _inproj(h, w_in[l], q_norm_g[l][None, :], k_norm_g[l][None, :], cos_tab, sin_tab)

    sink = attn_sink[l]
    kc = cache_k[:, l].reshape(DEC_BATCH, PAST_LEN, KV_WIDTH)
    vc = cache_v[:, l].reshape(DEC_BATCH, PAST_LEN, KV_WIDTH)
    attn_p = _attn_p(sink, proj)
    attn_s = _attn_s(sink, proj, kc, vc)

    z1 = _conv(proj, conv_w[l], w_conv_out[l])

    wr_pad = jnp.pad(router_w[l], ((0, 0), (0, 128 - N_EXPERTS)))
    x1, h2, logits = _merge(xp, xs, attn_p, attn_s, z1, proj, mods3, norm_ffn_g[l][None, :], wr_pad,
                            w_attn_out[l], w_out[l])

    enc, cnt = _route1(logits, router_bias[l][:, None])
    meta, tmap, inv = _route2(enc, cnt, _pad_codes())
    tile_expert = tmap[0]
    n_active = tmap[1, :1]

    ys = _moe(tile_expert, n_active, tmap[2], inv, h2, w_exp_gate[l], w_exp_up[l], w_exp_down[l])
    y_p, y_s = _combine(meta, ys, h2, x1, mods3, w_sh_gate[l], w_sh_up[l], w_sh_down[l])

    y_prompt = y_p.reshape(BATCH, SEQ, D_MODEL)
    y_sample = y_s.reshape(DEC_BATCH, DEC_SEQ, D_MODEL)
    new_k = k32.reshape(BATCH, 1, SEQ, N_KV_HEADS, HEAD_DIM)
    new_v = v32.reshape(BATCH, 1, SEQ, N_KV_HEADS, HEAD_DIM)
    return (y_prompt, y_sample, new_k, new_v)
```

```python
import jax
import jax.numpy as jnp
import numpy as np
from jax import lax
from jax.experimental import pallas as pl
from jax.experimental.pallas import tpu as pltpu

F32 = jnp.float32
BF16 = jnp.bfloat16
I32 = jnp.int32

D_MODEL = 2048
BATCH = 16
SEQ = 256
DEC_BATCH = 2
DEC_SEQ = 4096
PAST_LEN = 512
GRID_W = 64
N_HEADS = 16
N_KV_HEADS = 4
HEAD_DIM = 128
GROUP = N_HEADS // N_KV_HEADS
ATTN_WIDTH = N_HEADS * HEAD_DIM
KV_WIDTH = N_KV_HEADS * HEAD_DIM
WINDOW = 128
CONV_WIDTH = D_MODEL
N_EXPERTS = 64
TOP_K = 8
N_EXPERT_GROUPS = 8
GROUP_SIZE = N_EXPERTS // N_EXPERT_GROUPS
TOPK_GROUPS = 4
D_EXPERT = 512
D_SHARED = 512
ROUTED_SCALE = 2.5
ROPE_THETA = 10000.0
EPS = 1e-6
NEG_INF = -1e30
ATTN_SCALE = HEAD_DIM ** -0.5
LOG2E = 1.4426950408889634

T_P = BATCH * SEQ
T_S = DEC_BATCH * DEC_SEQ
T_ALL = T_P + T_S
MOD_GROUP = 4096
N_MOD = 1 + DEC_BATCH
assert T_P == MOD_GROUP and DEC_SEQ == MOD_GROUP

COL_CB, COL_CC, COL_CX, COL_Q, COL_GC, COL_GA = (i * D_MODEL for i in range(6))
COL_K = 6 * D_MODEL
COL_V = COL_K + KV_WIDTH
IN_WIDTH = COL_V + KV_WIDTH
W_COL_KV = 3 * CONV_WIDTH + ATTN_WIDTH

TM_IN = 1024
TN_IN = 1024
TM = 256
TM_PRE = 512
TM_C = 128
TILE_E = 512
N_SORTED = T_ALL * TOP_K + N_EXPERTS * TILE_E
N_TILES_E = N_SORTED // TILE_E
TMAP_W = 512
CAST_ROWS = 512
VMEM_LIMIT = 56 * 1024 * 1024


def _cparams(sem):
    return pltpu.CompilerParams(dimension_semantics=sem, vmem_limit_bytes=VMEM_LIMIT)


def _silu(x):
    return x * jax.nn.sigmoid(x)


def _load_cast_weight(w_hbm, w_scr, stage, sem):
    rows = stage.shape[0]
    for c in range(w_hbm.shape[0] // rows):
        cp = pltpu.make_async_copy(w_hbm.at[pl.ds(c * rows, rows)], stage, sem)
        cp.start()
        cp.wait()
        w_scr[pl.ds(c * rows, rows), :] = stage[...].astype(BF16)


ADA_TN = 1024
ADA_CHUNK = 256


def _ada_kernel(ct_ref, w_ref, b_ref, o_ref):
    tn = w_ref.shape[1]

    def body(c, accs):
        k0 = pl.multiple_of(c * ADA_CHUNK, ADA_CHUNK)
        wch = w_ref[pl.ds(k0, ADA_CHUNK), :]
        sch = _silu(ct_ref[pl.ds(k0, ADA_CHUNK), :])
        out = []
        for r in range(N_MOD):
            p = wch * sch[:, r:r + 1]
            out.append(accs[r] + p.reshape(ADA_CHUNK // 8, 8, tn).sum(axis=0))
        return tuple(out)

    accs = lax.fori_loop(0, D_MODEL // ADA_CHUNK, body,
                         tuple(jnp.zeros((8, tn), F32) for _ in range(N_MOD)))
    o_ref[...] = jnp.zeros_like(o_ref)
    for r in range(N_MOD):
        o_ref[r:r + 1, :] = jnp.sum(accs[r], axis=0, keepdims=True) + b_ref[...]


def _ada(cond_t, w_ada, b_ada):
    n = w_ada.shape[1]
    return pl.pallas_call(
        _ada_kernel,
        grid=(n // ADA_TN,),
        in_specs=[pl.BlockSpec((D_MODEL, 8), lambda j: (0, 0)),
                  pl.BlockSpec((D_MODEL, ADA_TN), lambda j: (0, j)),
                  pl.BlockSpec((1, ADA_TN), lambda j: (0, j))],
        out_specs=pl.BlockSpec((8, ADA_TN), lambda j: (0, j)),
        out_shape=jax.ShapeDtypeStruct((8, n), F32),
        compiler_params=_cparams(("arbitrary",)),
        name="ada",
    )(cond_t, w_ada, b_ada)


def _rms_mod(x, g, shift, scale):
    ms = jnp.mean(x * x, axis=-1, keepdims=True)
    y = x * lax.rsqrt(ms + EPS) * g
    return y * (1.0 + scale) + shift


def _two_group_specs(tile, width):
    n_ctx = T_P // tile
    return (pl.BlockSpec((tile, width), lambda i: (jnp.minimum(i, n_ctx - 1), 0)),
            pl.BlockSpec((tile, width), lambda i: (jnp.maximum(i - n_ctx, 0), 0)))


def _prenorm_kernel(xp_ref, xs_ref, mod_ref, g_ref, h_ref):
    i = pl.program_id(0)
    shift = mod_ref[0, 0:1, :]
    scale = mod_ref[0, 1:2, :]
    g = g_ref[...]

    @pl.when(i < T_P // TM_PRE)
    def _():
        h_ref[...] = _rms_mod(xp_ref[...], g, shift, scale).astype(BF16)

    @pl.when(i >= T_P // TM_PRE)
    def _():
        h_ref[...] = _rms_mod(xs_ref[...], g, shift, scale).astype(BF16)


def _prenorm(xp, xs, mods3, g_mix):
    tiles_per_group = MOD_GROUP // TM_PRE
    sp, ss = _two_group_specs(TM_PRE, D_MODEL)
    return pl.pallas_call(
        _prenorm_kernel,
        grid=(T_ALL // TM_PRE,),
        in_specs=[sp, ss,
                  pl.BlockSpec((1, 6, D_MODEL), lambda i: (i // tiles_per_group, 0, 0)),
                  pl.BlockSpec((1, D_MODEL), lambda i: (0, 0))],
        out_specs=pl.BlockSpec((TM_PRE, D_MODEL), lambda i: (i, 0)),
        out_shape=jax.ShapeDtypeStruct((T_ALL, D_MODEL), BF16),
        compiler_params=_cparams(("arbitrary",)),
        name="prenorm",
    )(xp, xs, mods3, g_mix)


def _head_norm_rope(a, g, cos2, sin2, scale):
    ms = jnp.mean(a * a, axis=-1, keepdims=True)
    a = a * lax.rsqrt(ms + EPS) * g
    a = a * cos2 + pltpu.roll(a, HEAD_DIM // 2, 1) * sin2
    return a * scale


N_J = IN_WIDTH // TN_IN
J_Q0 = COL_Q // TN_IN
J_Q1 = J_Q0 + ATTN_WIDTH // TN_IN
J_KV = COL_K // TN_IN
assert J_KV == N_J - 1
N_CTX_IN = T_P // TM_IN


def _inproj_kernel(h_ref, w_ref, qg_ref, kg_ref, cos_ref, sin_ref, o_ref, k_ref, v_ref, w_scr):
    j = pl.program_id(0)
    i = pl.program_id(1)

    @pl.when(i == 0)
    def _():
        for c in range(D_MODEL // CAST_ROWS):
            rows = pl.ds(c * CAST_ROWS, CAST_ROWS)
            w_scr[rows, :] = w_ref[rows, :].astype(BF16)

    acc = jnp.dot(h_ref[...], w_scr[...], preferred_element_type=F32)
    is_q = jnp.logical_and(j >= J_Q0, j < J_Q1)
    is_kv = j == J_KV

    @pl.when(jnp.logical_not(jnp.logical_or(is_q, is_kv)))
    def _():
        o_ref[...] = acc.astype(BF16)

    @pl.when(is_q)
    def _():
        cos2 = cos_ref[...]
        sin2 = sin_ref[...]
        g = qg_ref[...]
        for h in range(TN_IN // HEAD_DIM):
            cols = slice(h * HEAD_DIM, (h + 1) * HEAD_DIM)
            o_ref[:, cols] = _head_norm_rope(acc[:, cols], g, cos2, sin2, ATTN_SCALE * LOG2E).astype(BF16)

    @pl.when(is_kv)
    def _():
        cos2 = cos_ref[...]
        sin2 = sin_ref[...]
        g = kg_ref[...]
        is_ctx = i < N_CTX_IN
        for h in range(N_KV_HEADS):
            cols = slice(h * HEAD_DIM, (h + 1) * HEAD_DIM)
            kh = _head_norm_rope(acc[:, cols], g, cos2, sin2, 1.0)
            o_ref[:, cols] = kh.astype(BF16)

            @pl.when(is_ctx)
            def _():
                k_ref[:, cols] = kh

        v = acc[:, KV_WIDTH:]
        o_ref[:, KV_WIDTH:] = v.astype(BF16)

        @pl.when(is_ctx)
        def _():
            v_ref[...] = v


def _inproj(h, w_in, qg, kg, cos_tab, sin_tab):
    pos_tiles = DEC_SEQ // TM_IN

    def tab_map(j, i):
        return (jnp.where(i < N_CTX_IN, 0, 1 + i % pos_tiles), 0)

    def w_map(j, i):
        return (0, jnp.where(j < W_COL_KV // TN_IN, j, jnp.where(j == J_KV, W_COL_KV // TN_IN, j + 1)))

    def kv_map(j, i):
        return (jnp.where(j == J_KV, jnp.minimum(i, N_CTX_IN - 1), 0), 0)

    return pl.pallas_call(
        _inproj_kernel,
        grid=(N_J, T_ALL // TM_IN),
        in_specs=[pl.BlockSpec((TM_IN, D_MODEL), lambda j, i: (i, 0)),
                  pl.BlockSpec((D_MODEL, TN_IN), w_map, pipeline_mode=pl.Buffered(1)),
                  pl.BlockSpec((1, HEAD_DIM), lambda j, i: (0, 0)),
                  pl.BlockSpec((1, HEAD_DIM), lambda j, i: (0, 0)),
                  pl.BlockSpec((TM_IN, HEAD_DIM), tab_map),
                  pl.BlockSpec((TM_IN, HEAD_DIM), tab_map)],
        out_specs=[pl.BlockSpec((TM_IN, TN_IN), lambda j, i: (i, j)),
                   pl.BlockSpec((TM_IN, KV_WIDTH), kv_map),
                   pl.BlockSpec((TM_IN, KV_WIDTH), kv_map)],
        out_shape=[jax.ShapeDtypeStruct((T_ALL, IN_WIDTH), BF16),
                   jax.ShapeDtypeStruct((T_P, KV_WIDTH), F32),
                   jax.ShapeDtypeStruct((T_P, KV_WIDTH), F32)],
        scratch_shapes=[pltpu.VMEM((D_MODEL, TN_IN), BF16)],
        compiler_params=_cparams(("arbitrary", "arbitrary")),
        name="inproj",
    )(h, w_in, qg, kg, cos_tab, sin_tab)


def _dot_nt(a, b):
    return lax.dot_general(a, b, (((1,), (1,)), ((), ())), preferred_element_type=F32)


def _attn_p_kernel(sink_ref, q_ref, k_ref, v_ref, o_ref):
    for kh in range(N_KV_HEADS):
        kcols = slice(kh * HEAD_DIM, (kh + 1) * HEAD_DIM)
        k = k_ref[:, kcols]
        v1 = jnp.concatenate([v_ref[:, kcols], jnp.ones((SEQ, HEAD_DIM), BF16)], axis=1)
        heads = [kh * GROUP + gi for gi in range(GROUP)]
        q = jnp.concatenate([q_ref[:, h * HEAD_DIM:(h + 1) * HEAD_DIM] for h in heads], axis=0)
        sk = jnp.concatenate([jnp.full((SEQ, 1), sink_ref[h] * LOG2E, F32) for h in heads], axis=0)
        s = _dot_nt(q, k)
        m = jnp.maximum(jnp.max(s, axis=-1, keepdims=True), sk)
        p = jnp.exp2(s - m).astype(BF16)
        o1 = jnp.dot(p, v1, preferred_element_type=F32)
        o = o1[:, :HEAD_DIM] / (o1[:, HEAD_DIM:HEAD_DIM + 1] + jnp.exp2(sk - m))
        for gi, h in enumerate(heads):
            o_ref[:, h * HEAD_DIM:(h + 1) * HEAD_DIM] = o[gi * SEQ:(gi + 1) * SEQ, :].astype(BF16)


def _attn_p(sink, proj):
    return pl.pallas_call(
        _attn_p_kernel,
        grid=(BATCH,),
        in_specs=[pl.BlockSpec(memory_space=pltpu.SMEM),
                  pl.BlockSpec((SEQ, ATTN_WIDTH), lambda b: (b, COL_Q // ATTN_WIDTH)),
                  pl.BlockSpec((SEQ, KV_WIDTH), lambda b: (b, COL_K // KV_WIDTH)),
                  pl.BlockSpec((SEQ, KV_WIDTH), lambda b: (b, COL_V // KV_WIDTH))],
        out_specs=pl.BlockSpec((SEQ, ATTN_WIDTH), lambda b: (b, 0)),
        out_shape=jax.ShapeDtypeStruct((T_P, ATTN_WIDTH), BF16),
        compiler_params=_cparams(("arbitrary",)),
        name="attn_p",
    )(sink, proj, proj, proj)


QB = 256
BAND = QB + 2 * WINDOW


def _attn_s_kernel(sink_ref, q_ref, k_ref, v_ref, kc_ref, vc_ref, o_ref):
    i = pl.program_id(1)
    start = jnp.clip(i * QB - WINDOW, 0, DEC_SEQ - BAND)
    start = pl.multiple_of(start, WINDOW)
    qpos = i * QB + lax.broadcasted_iota(I32, (QB, BAND), 0)
    kpos = start + lax.broadcasted_iota(I32, (QB, BAND), 1)
    valid = jnp.abs(qpos - kpos) <= WINDOW
    valid_g = jnp.concatenate([valid] * GROUP, axis=0)
    for kh in range(N_KV_HEADS):
        kcols = slice(kh * HEAD_DIM, (kh + 1) * HEAD_DIM)
        kb = k_ref[pl.ds(start, BAND), kcols]
        kc = kc_ref[0, :, kcols].astype(BF16)
        ones = jnp.ones((BAND, HEAD_DIM), BF16)
        vb1 = jnp.concatenate([v_ref[pl.ds(start, BAND), kcols], ones], axis=1)
        vc1 = jnp.concatenate([vc_ref[0, :, kcols].astype(BF16), ones[:PAST_LEN]], axis=1)
        heads = [kh * GROUP + gi for gi in range(GROUP)]
        q = jnp.concatenate([q_ref[:, h * HEAD_DIM:(h + 1) * HEAD_DIM] for h in heads], axis=0)
        sk = jnp.concatenate([jnp.full((QB, 1), sink_ref[h] * LOG2E, F32) for h in heads], axis=0)
        s_loc = jnp.where(valid_g, _dot_nt(q, kb), NEG_INF)
        s_ctx = _dot_nt(q, kc)
        m = jnp.maximum(jnp.maximum(jnp.max(s_loc, axis=-1, keepdims=True),
                                    jnp.max(s_ctx, axis=-1, keepdims=True)), sk)
        p_loc = jnp.exp2(s_loc - m).astype(BF16)
        p_ctx = jnp.exp2(s_ctx - m).astype(BF16)
        o1 = (jnp.dot(p_loc, vb1, preferred_element_type=F32)
              + jnp.dot(p_ctx, vc1, preferred_element_type=F32))
        den = o1[:, HEAD_DIM:HEAD_DIM + 1] + jnp.exp2(sk - m)
        o = o1[:, :HEAD_DIM] / den
        for gi, h in enumerate(heads):
            o_ref[:, h * HEAD_DIM:(h + 1) * HEAD_DIM] = o[gi * QB:(gi + 1) * QB, :].astype(BF16)


def _attn_s(sink, proj, kc, vc):
    nq = DEC_SEQ // QB
    first = T_P // QB
    seq_blk = T_P // DEC_SEQ
    return pl.pallas_call(
        _attn_s_kernel,
        grid=(DEC_BATCH, nq),
        in_specs=[pl.BlockSpec(memory_space=pltpu.SMEM),
                  pl.BlockSpec((QB, ATTN_WIDTH), lambda b, i: (first + b * nq + i, COL_Q // ATTN_WIDTH)),
                  pl.BlockSpec((DEC_SEQ, KV_WIDTH), lambda b, i: (seq_blk + b, COL_K // KV_WIDTH)),
                  pl.BlockSpec((DEC_SEQ, KV_WIDTH), lambda b, i: (seq_blk + b, COL_V // KV_WIDTH)),
                  pl.BlockSpec((1, PAST_LEN, KV_WIDTH), lambda b, i: (b, 0, 0)),
                  pl.BlockSpec((1, PAST_LEN, KV_WIDTH), lambda b, i: (b, 0, 0))],
        out_specs=pl.BlockSpec((QB, ATTN_WIDTH), lambda b, i: (b * nq + i, 0)),
        out_shape=jax.ShapeDtypeStruct((T_S, ATTN_WIDTH), BF16),
        compiler_params=_cparams(("arbitrary", "arbitrary")),
        name="attn_s",
    )(sink, proj, proj, proj, kc, vc)


HALO = 16


TM_CONV = 512
CONV_CHUNK = 512
assert TM_CONV % SEQ == 0 and DEC_SEQ % TM_CONV == 0 and SEQ & (SEQ - 1) == 0


def _conv_kernel(cb_ref, cc_ref, cx_ref, ccp_ref, cxp_ref, ccn_ref, cxn_ref, gc_ref, cw_ref, w_hbm, o_ref,
                 w_scr, stage, u_scr, sem):
    i = pl.program_id(0)

    @pl.when(i == 0)
    def _():
        _load_cast_weight(w_hbm, w_scr, stage, sem)

    n_ctx_tiles = T_P // TM_CONV
    per_seq = DEC_SEQ // TM_CONV
    is_ctx = i < n_ctx_tiles
    is_first = jnp.logical_or(is_ctx, (i - n_ctx_tiles) % per_seq == 0)
    is_last = jnp.logical_or(is_ctx, (i - n_ctx_tiles) % per_seq == per_seq - 1)
    rows = lax.broadcasted_iota(I32, (TM_CONV, 1), 0)
    cut_prev = jnp.logical_and(is_ctx, (rows & (SEQ - 1)) == 0)
    cut_next = jnp.logical_and(is_ctx, (rows & (SEQ - 1)) == SEQ - 1)
    for c in range(CONV_WIDTH // CONV_CHUNK):
        cols = slice(c * CONV_CHUNK, (c + 1) * CONV_CHUNK)
        p = cc_ref[:, cols].astype(F32) * cx_ref[:, cols].astype(F32)
        prev_row = ccp_ref[HALO - 1:HALO, cols].astype(F32) * cxp_ref[HALO - 1:HALO, cols].astype(F32)
        next_row = ccn_ref[0:1, cols].astype(F32) * cxn_ref[0:1, cols].astype(F32)
        prev_row = jnp.where(is_first, 0.0, prev_row)
        next_row = jnp.where(is_last, 0.0, next_row)
        p_prev = jnp.where(rows == 0, prev_row, pltpu.roll(p, 1, 0))
        p_next = jnp.where(rows == TM_CONV - 1, next_row, pltpu.roll(p, TM_CONV - 1, 0))
        p_prev = jnp.where(cut_prev, 0.0, p_prev)
        p_next = jnp.where(cut_next, 0.0, p_next)
        conv = p_prev * cw_ref[0:1, cols] + p * cw_ref[1:2, cols] + p_next * cw_ref[2:3, cols]
        u_scr[:, cols] = (cb_ref[:, cols].astype(F32) * conv).astype(BF16)
    y = jnp.dot(u_scr[...], w_scr[...], preferred_element_type=F32)
    o_ref[...] = (jax.nn.sigmoid(gc_ref[...].astype(F32)) * y).astype(BF16)


def _conv(proj, conv_w, w_conv_out):
    hb = TM_CONV // HALO
    last_hb = T_ALL // HALO - 1
    wide = lambda c: pl.BlockSpec((TM_CONV, D_MODEL), lambda i: (i, c // D_MODEL))
    prev = lambda c: pl.BlockSpec((HALO, D_MODEL), lambda i: (jnp.maximum(i * hb - 1, 0), c // D_MODEL))
    nxt = lambda c: pl.BlockSpec((HALO, D_MODEL), lambda i: (jnp.minimum((i + 1) * hb, last_hb), c // D_MODEL))
    return pl.pallas_call(
        _conv_kernel,
        grid=(T_ALL // TM_CONV,),
        in_specs=[wide(COL_CB), wide(COL_CC), wide(COL_CX),
                  prev(COL_CC), prev(COL_CX), nxt(COL_CC), nxt(COL_CX),
                  wide(COL_GC),
                  pl.BlockSpec((3, CONV_WIDTH), lambda i: (0, 0)),
                  pl.BlockSpec(memory_space=pl.ANY)],
        out_specs=pl.BlockSpec((TM_CONV, D_MODEL), lambda i: (i, 0)),
        out_shape=jax.ShapeDtypeStruct((T_ALL, D_MODEL), BF16),
        scratch_shapes=[pltpu.VMEM((CONV_WIDTH, D_MODEL), BF16),
                        pltpu.VMEM((CAST_ROWS, D_MODEL), F32),
                        pltpu.VMEM((TM_CONV, CONV_WIDTH), BF16),
                        pltpu.SemaphoreType.DMA(())],
        compiler_params=_cparams(("arbitrary",)),
        name="conv",
    )(proj, proj, proj, proj, proj, proj, proj, proj, conv_w, w_conv_out)


def _merge_kernel(xp_ref, xs_ref, attn_p_ref, attn_s_ref, z1_ref, ga_ref, mod_ref, g2_ref, wr_ref,
                  wao_hbm, wo_hbm, x1_ref, h2_ref, lg_ref, wao_scr, wo_scr, stage, sem):
    i = pl.program_id(0)

    @pl.when(i == 0)
    def _():
        _load_cast_weight(wao_hbm, wao_scr, stage, sem)
        _load_cast_weight(wo_hbm, wo_scr, stage, sem)

    is_ctx = i < T_P // TM
    attn = jnp.where(is_ctx, attn_p_ref[...], attn_s_ref[...])
    x = jnp.where(is_ctx, xp_ref[...], xs_ref[...])
    ya = jnp.dot(attn, wao_scr[...], preferred_element_type=F32)
    z = z1_ref[...].astype(F32) + jax.nn.sigmoid(ga_ref[...].astype(F32)) * ya
    mix = jnp.dot(z.astype(BF16), wo_scr[...], preferred_element_type=F32)
    x1 = x + mod_ref[0, 2:3, :] * mix
    x1_ref[...] = x1
    h = _rms_mod(x1, g2_ref[...], mod_ref[0, 3:4, :], mod_ref[0, 4:5, :])
    h2_ref[...] = h
    h_hi = h.astype(BF16)
    h_lo = (h - h_hi.astype(F32)).astype(BF16)
    wr = wr_ref[...]
    wr_hi32 = wr.astype(BF16).astype(F32)
    wr_lo32 = (wr - wr_hi32).astype(BF16).astype(F32)
    w_both = (wr_hi32 + pltpu.roll(wr_lo32, N_EXPERTS, 1)).astype(BF16)
    both = jnp.dot(h_hi, w_both, preferred_element_type=F32)
    lg_ref[...] = (both + pltpu.roll(both, N_EXPERTS, 1)
                   + jnp.dot(h_lo, wr_hi32.astype(BF16), preferred_element_type=F32))


def _merge(xp, xs, attn_p, attn_s, z1, proj, mods3, g_ffn, wr_pad, w_attn_out, w_out):
    tiles_per_group = MOD_GROUP // TM
    const = lambda shape: pl.BlockSpec(shape, lambda i: (0, 0))
    row = lambda w: pl.BlockSpec((TM, w), lambda i: (i, 0))
    xsp, xss = _two_group_specs(TM, D_MODEL)
    asp, ass = _two_group_specs(TM, ATTN_WIDTH)
    return pl.pallas_call(
        _merge_kernel,
        grid=(T_ALL // TM,),
        in_specs=[xsp, xss, asp, ass, row(D_MODEL),
                  pl.BlockSpec((TM, D_MODEL), lambda i: (i, COL_GA // D_MODEL)),
                  pl.BlockSpec((1, 6, D_MODEL), lambda i: (i // tiles_per_group, 0, 0)),
                  const((1, D_MODEL)), const((D_MODEL, 128)),
                  pl.BlockSpec(memory_space=pl.ANY), pl.BlockSpec(memory_space=pl.ANY)],
        out_specs=[row(D_MODEL), row(D_MODEL), row(128)],
        out_shape=[jax.ShapeDtypeStruct((T_ALL, D_MODEL), F32),
                   jax.ShapeDtypeStruct((T_ALL, D_MODEL), F32),
                   jax.ShapeDtypeStruct((T_ALL, 128), F32)],
        scratch_shapes=[pltpu.VMEM((ATTN_WIDTH, D_MODEL), BF16),
                        pltpu.VMEM((D_MODEL, D_MODEL), BF16),
                        pltpu.VMEM((CAST_ROWS, D_MODEL), F32),
                        pltpu.SemaphoreType.DMA(())],
        compiler_params=_cparams(("arbitrary",)),
        name="merge",
    )(xp, xs, attn_p, attn_s, z1, proj, mods3, g_ffn, wr_pad, w_attn_out, w_out)


def _route1_kernel(lg_ref, bias_ref, enc_ref, cnt_ref):
    i = pl.program_id(0)
    lt = lg_ref[...].T[:N_EXPERTS, :]
    scores = jax.nn.sigmoid(lt)
    biased = scores + bias_ref[...]
    b3 = biased.reshape(N_EXPERT_GROUPS, GROUP_SIZE, TM)
    mi = lax.broadcasted_iota(I32, b3.shape, 1)
    m1 = jnp.max(b3, axis=1, keepdims=True)
    idx1 = jnp.min(jnp.where(b3 == m1, mi, GROUP_SIZE), axis=1, keepdims=True)
    m2 = jnp.max(jnp.where(mi == idx1, -jnp.inf, b3), axis=1, keepdims=True)
    gs = (m1 + m2).reshape(N_EXPERT_GROUPS, TM)
    gidx = lax.broadcasted_iota(I32, gs.shape, 0)
    grank = jnp.zeros(gs.shape, I32)
    for j in range(N_EXPERT_GROUPS):
        gj = gs[j:j + 1, :]
        beats = jnp.logical_or(gj > gs, jnp.logical_and(gj == gs, j < gidx))
        grank = grank + beats.astype(I32)
    gsel = grank < TOPK_GROUPS
    emask = jnp.broadcast_to(gsel[:, None, :], b3.shape).reshape(N_EXPERTS, TM)
    masked = jnp.where(emask, biased, NEG_INF)
    eidx = lax.broadcasted_iota(I32, masked.shape, 0)
    erank = jnp.zeros(masked.shape, I32)
    for j in range(N_EXPERTS):
        vj = masked[j:j + 1, :]
        beats = jnp.logical_or(vj > masked, jnp.logical_and(vj == masked, j < eidx))
        erank = erank + beats.astype(I32)
    sel = erank < TOP_K
    wsel = jnp.where(sel, scores, 0.0)
    den = jnp.sum(wsel, axis=0, keepdims=True)
    wts = wsel / den * ROUTED_SCALE
    enc_ref[...] = jnp.where(sel, wts, -1.0)

    @pl.when(i == 0)
    def _():
        cnt_ref[...] = jnp.zeros_like(cnt_ref)

    cnt = jnp.sum(sel.astype(F32), axis=1, keepdims=True)
    cnt_ref[...] += jnp.broadcast_to(cnt, cnt_ref.shape)


def _route1(logits, bias_col):
    return pl.pallas_call(
        _route1_kernel,
        grid=(T_ALL // TM,),
        in_specs=[pl.BlockSpec((TM, 128), lambda i: (i, 0)),
                  pl.BlockSpec((N_EXPERTS, 1), lambda i: (0, 0))],
        out_specs=[pl.BlockSpec((N_EXPERTS, TM), lambda i: (0, i)),
                   pl.BlockSpec((N_EXPERTS, 128), lambda i: (0, 0))],
        out_shape=[jax.ShapeDtypeStruct((N_EXPERTS, T_ALL), F32),
                   jax.ShapeDtypeStruct((N_EXPERTS, 128), F32)],
        compiler_params=_cparams(("arbitrary",)),
        name="route1",
    )(logits, bias_col)


def _route2_positions(enc_ref, cnt_ref, meta_ref, tmap_ref, carry_ref):
    enc = enc_ref[...]
    sel = enc >= 0.0
    wts = jnp.maximum(enc, 0.0)
    sel_b = sel.astype(BF16)
    ntile = jnp.ceil(cnt_ref[...] * (1.0 / TILE_E))
    er = lax.broadcasted_iota(I32, (N_EXPERTS, N_EXPERTS), 0)
    ec = lax.broadcasted_iota(I32, (N_EXPERTS, N_EXPERTS), 1)
    lower = (ec < er).astype(BF16)
    off_t = jnp.dot(lower, ntile.astype(BF16), preferred_element_type=F32)
    tr = lax.broadcasted_iota(I32, (TM, TM), 0)
    tc = lax.broadcasted_iota(I32, (TM, TM), 1)
    upper = (tr < tc).astype(BF16)
    rank = jnp.dot(sel_b, upper, preferred_element_type=F32) + carry_ref[:, 0:1]
    carry_ref[...] += jnp.broadcast_to(jnp.sum(sel.astype(F32), axis=1, keepdims=True), carry_ref.shape)
    pos = off_t[:, 0:1] * float(TILE_E) + rank
    slot = jnp.dot(lower, sel_b, preferred_element_type=F32)
    rows = []
    for k in range(TOP_K):
        mk = jnp.logical_and(sel, slot == float(k))
        rows.append(jnp.sum(jnp.where(mk, wts, 0.0), axis=0, keepdims=True))
    for k in range(TOP_K):
        mk = jnp.logical_and(sel, slot == float(k))
        rows.append(jnp.sum(jnp.where(mk, pos, 0.0), axis=0, keepdims=True))
    rows.append(jnp.zeros((128 - 2 * TOP_K, TM), F32))
    meta_ref[...] = jnp.concatenate(rows, axis=0).T
    nt = ntile[:, 0:1]
    end_t = off_t[:, 0:1] + nt
    lane = lax.broadcasted_iota(I32, (N_EXPERTS, TMAP_W), 1)
    te = jnp.sum((end_t <= lane.astype(F32)).astype(F32), axis=0, keepdims=True)
    te = jnp.minimum(te, float(N_EXPERTS - 1))
    nact = jnp.sum(nt, axis=0, keepdims=True)
    erow = lax.broadcasted_iota(I32, (N_EXPERTS, TMAP_W), 0).astype(F32)
    nxt = jnp.sum(jnp.where(erow == te, end_t, 0.0), axis=0, keepdims=True)
    tmap_ref[...] = jnp.zeros_like(tmap_ref)
    tmap_ref[0:1, :] = te.astype(I32)
    tmap_ref[1:2, :] = jnp.broadcast_to(nact, (1, TMAP_W)).astype(I32)
    tmap_ref[2:3, :] = nxt.astype(I32)
    return jnp.concatenate(rows[TOP_K:2 * TOP_K], axis=0).astype(I32)


N_TOK_TILES = T_ALL // TM


def _route2_kernel(enc_ref, cnt_ref, init_hbm, meta_ref, tmap_ref, inv_hbm,
                   carry_ref, posv0, posv1, poss0, poss1, inv_smem, psem, isem):
    i = pl.program_id(0)
    posv = (posv0, posv1)
    poss = (poss0, poss1)
    per_token = (1 << TOKEN_BITS) + 1
    per_slot = T_ALL << TOKEN_BITS

    @pl.when(i == 0)
    def _():
        carry_ref[...] = jnp.zeros_like(carry_ref)
        cp = pltpu.make_async_copy(init_hbm, inv_smem, isem)
        cp.start()
        cp.wait()

    def positions(s):
        posv[s][...] = _route2_positions(enc_ref, cnt_ref, meta_ref, tmap_ref, carry_ref)
        pltpu.make_async_copy(posv[s], poss[s], psem.at[s]).start()

    def invert(s):
        pltpu.make_async_copy(posv[s], poss[s], psem.at[s]).wait()
        base = (i - 1) * (TM * per_token)
        for t in range(TM):
            for k in range(TOP_K):
                inv_smem[poss[s][k, t]] = base + (t * per_token + k * per_slot)

    @pl.when(i == 0)
    def _():
        positions(0)

    for s in range(2):
        @pl.when(jnp.logical_and(jnp.logical_and(i >= 1, i < N_TOK_TILES), i % 2 == s))
        def _():
            invert(1 - s)
            positions(s)

    @pl.when(i == N_TOK_TILES)
    def _():
        invert((N_TOK_TILES - 1) % 2)
        cp = pltpu.make_async_copy(inv_smem, inv_hbm, isem)
        cp.start()
        cp.wait()


def _route2(enc, cnt, inv_init):
    last = N_TOK_TILES - 1
    hbm = pl.BlockSpec(memory_space=pl.ANY)
    return pl.pallas_call(
        _route2_kernel,
        grid=(N_TOK_TILES + 1,),
        in_specs=[pl.BlockSpec((N_EXPERTS, TM), lambda i: (0, jnp.minimum(i, last))),
                  pl.BlockSpec((N_EXPERTS, 128), lambda i: (0, 0)),
                  hbm],
        out_specs=[pl.BlockSpec((TM, 128), lambda i: (jnp.minimum(i, last), 0)),
                   pl.BlockSpec((8, TMAP_W), lambda i: (0, 0)),
                   hbm],
        out_shape=[jax.ShapeDtypeStruct((T_ALL, 128), F32),
                   jax.ShapeDtypeStruct((8, TMAP_W), I32),
                   jax.ShapeDtypeStruct((N_SORTED,), I32)],
        scratch_shapes=[pltpu.VMEM((N_EXPERTS, 128), F32),
                        pltpu.VMEM((TOP_K, TM), I32), pltpu.VMEM((TOP_K, TM), I32),
                        pltpu.SMEM((TOP_K, TM), I32), pltpu.SMEM((TOP_K, TM), I32),
                        pltpu.SMEM((N_SORTED,), I32),
                        pltpu.SemaphoreType.DMA((2,)), pltpu.SemaphoreType.DMA(())],
        compiler_params=_cparams(("arbitrary",)),
        name="route2",
    )(enc, cnt, inv_init)


TOKEN_BITS = 14
TOKEN_MASK = (1 << TOKEN_BITS) - 1
NBUF = 3
TRASH_BASE = TOP_K * T_ALL
YS_ROWS = TRASH_BASE + NBUF * TILE_E
assert T_ALL <= 1 << TOKEN_BITS and YS_ROWS << TOKEN_BITS < 1 << 31


def _pad_codes():
    r = jnp.arange(N_SORTED, dtype=I32)
    q = r % TILE_E
    out_row = TRASH_BASE + ((r // TILE_E) % NBUF) * TILE_E + q
    return (out_row << TOKEN_BITS) | q


N_CHUNK = 256
GU_PIECES = D_EXPERT // N_CHUNK
DN_PIECES = D_MODEL // N_CHUNK
N_PIECES = GU_PIECES + DN_PIECES

PIECE_WORK = (D_MODEL * 2,) * GU_PIECES + (D_EXPERT,) * DN_PIECES
PIECE_ROWS = tuple(round(TILE_E * sum(PIECE_WORK[:p]) / sum(PIECE_WORK)) for p in range(N_PIECES + 1))


def _moe_kernel(te_ref, na_ref, nxt_ref, inv_ref, h2_hbm, wg_hbm, wu_hbm, wd_hbm, ys_hbm,
                xbuf0, xbuf1, xbuf2, ybuf0, ybuf1, ybuf2, xb, act, wg_f32, wu_f32, wd_f32,
                wg_scr, wu_scr, wd_scr, gsem, ssem, wsem):
    i = pl.program_id(0)
    na = na_ref[0]
    active = i < na
    xbufs = (xbuf0, xbuf1, xbuf2)
    ybufs = (ybuf0, ybuf1, ybuf2)

    def weight_copies(e):
        return (pltpu.make_async_copy(wg_hbm.at[e], wg_f32, wsem),
                pltpu.make_async_copy(wu_hbm.at[e], wu_f32, wsem),
                pltpu.make_async_copy(wd_hbm.at[e], wd_f32, wsem))

    def gather_row(tile, s, r):
        tok = inv_ref[tile * TILE_E + r] & TOKEN_MASK
        pltpu.make_async_copy(h2_hbm.at[pl.ds(tok, 1)], xbufs[s].at[pl.ds(r, 1)], gsem.at[s]).start()

    def scatter_row(dst, s, r):
        pltpu.make_async_copy(ybufs[s].at[pl.ds(r, 1)], ys_hbm.at[pl.ds(dst, 1)], ssem.at[s]).start()

    def wait_gather(s):
        pltpu.make_async_copy(h2_hbm.at[pl.ds(0, TILE_E)], xbufs[s], gsem.at[s]).wait()

    def wait_scatter(s):
        pltpu.make_async_copy(ybufs[s], ys_hbm.at[pl.ds(0, TILE_E)], ssem.at[s]).wait()

    @pl.when(i == 0)
    def _():
        zeros = ybufs[NBUF - 1]
        zeros[...] = jnp.zeros_like(zeros)
        for m in range(NBUF):
            cp = pltpu.make_async_copy(zeros, ys_hbm.at[pl.ds(TRASH_BASE + m * TILE_E, TILE_E)], ssem.at[0])
            cp.start()
            cp.wait()
        for t in range(2):
            def body(r, carry):
                gather_row(t, t, r)
                return carry

            lax.fori_loop(0, TILE_E, body, 0)

        for cp in weight_copies(te_ref[0]):
            cp.start()

    new_expert = jnp.logical_or(i == 0, te_ref[i] != te_ref[jnp.maximum(i - 1, 0)])

    @pl.when(jnp.logical_and(active, new_expert))
    def _():
        for cp in weight_copies(0):
            cp.wait()
        for c in range(D_MODEL // CAST_ROWS):
            rows = pl.ds(c * CAST_ROWS, CAST_ROWS)
            wg_scr[rows, :] = wg_f32[rows, :].astype(BF16)
            wu_scr[rows, :] = wu_f32[rows, :].astype(BF16)
        wd_scr[...] = wd_f32[...].astype(BF16)
        nxt_tile = nxt_ref[i]

        @pl.when(nxt_tile < na)
        def _():
            for cp in weight_copies(te_ref[nxt_tile]):
                cp.start()

    def compute_tile(slot, write_back):
        s_next = (slot + 2) % NBUF
        s_prev = (slot - 1) % NBUF
        wait_gather(slot)

        @pl.when(i >= 3)
        def _():
            wait_scatter(slot)

        xb[...] = xbufs[slot][...].astype(BF16)
        nxt = jnp.minimum(i + 2, N_TILES_E - 1)
        codes_next = inv_ref.at[pl.ds(pl.multiple_of(nxt * TILE_E, TILE_E), TILE_E)]
        codes_prev = inv_ref.at[pl.ds(pl.multiple_of(jnp.maximum(i - 1, 0) * TILE_E, TILE_E), TILE_E)]

        def issue(piece):
            for r in range(PIECE_ROWS[piece], PIECE_ROWS[piece + 1]):
                tok = codes_next[r] & TOKEN_MASK
                pltpu.make_async_copy(h2_hbm.at[pl.ds(tok, 1)], xbufs[s_next].at[pl.ds(r, 1)],
                                      gsem.at[s_next]).start()
                if write_back:
                    scatter_row(codes_prev[r] >> TOKEN_BITS, s_prev, r)

        for c in range(GU_PIECES):
            cols = slice(c * N_CHUNK, (c + 1) * N_CHUNK)
            g = jnp.dot(xb[...], wg_scr[:, cols], preferred_element_type=F32)
            u = jnp.dot(xb[...], wu_scr[:, cols], preferred_element_type=F32)
            act[:, cols] = (_silu(g) * u).astype(BF16)
            issue(c)
        for c in range(DN_PIECES):
            cols = slice(c * N_CHUNK, (c + 1) * N_CHUNK)
            ybufs[slot][:, cols] = jnp.dot(act[...], wd_scr[:, cols], preferred_element_type=F32)
            issue(GU_PIECES + c)

    def drain(slot):
        s_last = (slot - 1) % NBUF

        def body(r, carry):
            scatter_row(inv_ref[(na - 1) * TILE_E + r] >> TOKEN_BITS, s_last, r)
            return carry

        lax.fori_loop(0, TILE_E, body, 0)
        wait_scatter(s_last)

        @pl.when(na >= 2)
        def _():
            wait_scatter((slot - 2) % NBUF)

        @pl.when(na >= 3)
        def _():
            wait_scatter(slot)

        wait_gather(slot)
        wait_gather((slot + 1) % NBUF)

    @pl.when(jnp.logical_and(active, i == 0))
    def _():
        compute_tile(0, write_back=False)

    for s in range(NBUF):
        @pl.when(jnp.logical_and(jnp.logical_and(active, i > 0), i % NBUF == s))
        def _():
            compute_tile(s, write_back=True)

        @pl.when(jnp.logical_and(i == na, i % NBUF == s))
        def _():
            drain(s)


def _moe(tile_expert, n_active, next_tile, inv, h2, wg, wu, wd):
    hbm = pl.BlockSpec(memory_space=pl.ANY)
    grid_spec = pltpu.PrefetchScalarGridSpec(
        num_scalar_prefetch=4,
        grid=(N_TILES_E + 1,),
        in_specs=[hbm, hbm, hbm, hbm],
        out_specs=hbm,
        scratch_shapes=[pltpu.VMEM((TILE_E, D_MODEL), F32)] * (2 * NBUF) + [
                        pltpu.VMEM((TILE_E, D_MODEL), BF16),
                        pltpu.VMEM((TILE_E, D_EXPERT), BF16),
                        pltpu.VMEM((D_MODEL, D_EXPERT), F32),
                        pltpu.VMEM((D_MODEL, D_EXPERT), F32),
                        pltpu.VMEM((D_EXPERT, D_MODEL), F32),
                        pltpu.VMEM((D_MODEL, D_EXPERT), BF16),
                        pltpu.VMEM((D_MODEL, D_EXPERT), BF16),
                        pltpu.VMEM((D_EXPERT, D_MODEL), BF16),
                        pltpu.SemaphoreType.DMA((NBUF,)),
                        pltpu.SemaphoreType.DMA((NBUF,)),
                        pltpu.SemaphoreType.DMA(())],
    )
    return pl.pallas_call(
        _moe_kernel,
        grid_spec=grid_spec,
        out_shape=jax.ShapeDtypeStruct((YS_ROWS, D_MODEL), F32),
        compiler_params=_cparams(("arbitrary",)),
        name="moe",
    )(tile_expert, n_active, next_tile, inv, h2, wg, wu, wd)


def _combine_kernel(meta_ref, *refs):
    ys_refs = refs[:TOP_K]
    h2_ref, x1_ref, mod_ref, wsg_hbm, wsu_hbm, wsd_hbm = refs[TOP_K:TOP_K + 6]
    op_ref, os_ref, wsg_scr, wsu_scr, wsd_scr, stage_a, stage_b, wsem = refs[TOP_K + 6:]
    i = pl.program_id(0)

    @pl.when(i == 0)
    def _():
        _load_cast_weight(wsg_hbm, wsg_scr, stage_a, wsem)
        _load_cast_weight(wsu_hbm, wsu_scr, stage_a, wsem)
        _load_cast_weight(wsd_hbm, wsd_scr, stage_b, wsem)

    h = h2_ref[...].astype(BF16)
    sg = jnp.dot(h, wsg_scr[...], preferred_element_type=F32)
    su = jnp.dot(h, wsu_scr[...], preferred_element_type=F32)
    moe = jnp.dot((_silu(sg) * su).astype(BF16), wsd_scr[...], preferred_element_type=F32)
    for k in range(TOP_K):
        moe = moe + meta_ref[:, k:k + 1] * ys_refs[k][...]
    y = x1_ref[...] + mod_ref[0, 5:6, :] * moe

    @pl.when(i < T_P // TM_C)
    def _():
        op_ref[...] = y

    @pl.when(i >= T_P // TM_C)
    def _():
        os_ref[...] = y


def _combine(meta, ys, h2, x1, mods3, wsg, wsu, wsd):
    tiles_per_group = MOD_GROUP // TM_C
    row = lambda w: pl.BlockSpec((TM_C, w), lambda i: (i, 0))
    slot_rows = lambda k: pl.BlockSpec((TM_C, D_MODEL), lambda i: (k * (T_ALL // TM_C) + i, 0))
    osp, oss = _two_group_specs(TM_C, D_MODEL)
    hbm = pl.BlockSpec(memory_space=pl.ANY)
    return pl.pallas_call(
        _combine_kernel,
        grid=(T_ALL // TM_C,),
        in_specs=[row(128)] + [slot_rows(k) for k in range(TOP_K)] + [
                  row(D_MODEL), row(D_MODEL),
                  pl.BlockSpec((1, 6, D_MODEL), lambda i: (i // tiles_per_group, 0, 0)),
                  hbm, hbm, hbm],
        out_specs=[osp, oss],
        out_shape=[jax.ShapeDtypeStruct((T_P, D_MODEL), F32),
                   jax.ShapeDtypeStruct((T_S, D_MODEL), F32)],
        scratch_shapes=[pltpu.VMEM((D_MODEL, D_SHARED), BF16),
                        pltpu.VMEM((D_MODEL, D_SHARED), BF16),
                        pltpu.VMEM((D_SHARED, D_MODEL), BF16),
                        pltpu.VMEM((D_MODEL, D_SHARED), F32),
                        pltpu.VMEM((D_SHARED, D_MODEL), F32),
                        pltpu.SemaphoreType.DMA(())],
        compiler_params=_cparams(("arbitrary",)),
        name="combine",
    )(meta, *([ys] * TOP_K), h2, x1, mods3, wsg, wsu, wsd)


def _rope_tables():
    rows = DEC_SEQ // GRID_W
    row = np.repeat(np.arange(rows, dtype=np.float64), GRID_W)
    col = np.tile(np.arange(GRID_W, dtype=np.float64), rows)
    n_freq = HEAD_DIM // 4
    inv = ROPE_THETA ** (-np.arange(n_freq, dtype=np.float64) / n_freq)
    ang = np.concatenate([row[:, None] * inv, col[:, None] * inv], axis=-1)
    cos, sin = np.cos(ang), np.sin(ang)
    cos2 = np.concatenate([cos, cos], axis=-1)
    sin2 = np.concatenate([-sin, sin], axis=-1)
    cos_tab = np.concatenate([np.ones((TM_IN, HEAD_DIM)), cos2], axis=0).astype(np.float32)
    sin_tab = np.concatenate([np.zeros((TM_IN, HEAD_DIM)), sin2], axis=0).astype(np.float32)
    return jnp.asarray(cos_tab), jnp.asarray(sin_tab)


def kernel(x_prompt, x_sample, cache_k, cache_v, c, c_ctx, w_ada, b_ada, norm_mix_g, norm_ffn_g, w_in, conv_w,
           q_norm_g, k_norm_g, attn_sink, w_conv_out, w_attn_out, w_out, router_w, router_bias, w_exp_gate,
           w_exp_up, w_exp_down, w_sh_gate, w_sh_up, w_sh_down):
    l = 0
    xp = x_prompt.reshape(T_P, D_MODEL)
    xs = x_sample.reshape(T_S, D_MODEL)

    cond = jnp.concatenate([c_ctx[None, :], c, jnp.zeros((8 - N_MOD, D_MODEL), F32)], axis=0)
    mods = _ada(cond.T, w_ada[l], b_ada[l][None, :])
    mods3 = mods[:N_MOD].reshape(N_MOD, 6, D_MODEL)

    h = _prenorm(xp, xs, mods3, norm_mix_g[l][None, :])
    cos_tab, sin_tab = _rope_tables()
    proj, k32, v32 = _inproj(h, w_in[l], q_norm_g[l][None, :], k_norm_g[l][None, :], cos_tab, sin_tab)

    sink = attn_sink[l]
    kc = cache_k[:, l].reshape(DEC_BATCH, PAST_LEN, KV_WIDTH)
    vc = cache_v[:, l].reshape(DEC_BATCH, PAST_LEN, KV_WIDTH)
    attn_p = _attn_p(sink, proj)
    attn_s = _attn_s(sink, proj, kc, vc)

    z1 = _conv(proj, conv_w[l], w_conv_out[l])

    wr_pad = jnp.pad(router_w[l], ((0, 0), (0, 128 - N_EXPERTS)))
    x1, h2, logits = _merge(xp, xs, attn_p, attn_s, z1, proj, mods3, norm_ffn_g[l][None, :], wr_pad,
                            w_attn_out[l], w_out[l])

    enc, cnt = _route1(logits, router_bias[l][:, None])
    meta, tmap, inv = _route2(enc, cnt, _pad_codes())
    tile_expert = tmap[0]
    n_active = tmap[1, :1]

    ys = _moe(tile_expert, n_active, tmap[2], inv, h2, w_exp_gate[l], w_exp_up[l], w_exp_down[l])
    y_p, y_s = _combine(meta, ys, h2, x1, mods3, w_sh_gate[l], w_sh_up[l], w_sh_down[l])

    y_prompt = y_p.reshape(BATCH, SEQ, D_MODEL)
    y_sample = y_s.reshape(DEC_BATCH, DEC_SEQ, D_MODEL)
    new_k = k32.reshape(BATCH, 1, SEQ, N_KV_HEADS, HEAD_DIM)
    new_v = v32.reshape(BATCH, 1, SEQ, N_KV_HEADS, HEAD_DIM)
    return (y_prompt, y_sample, new_k, new_v)
```

```python
import jax
import jax.numpy as jnp
import numpy as np
from jax import lax
from jax.experimental import pallas as pl
from jax.experimental.pallas import tpu as pltpu

F32 = jnp.float32
BF16 = jnp.bfloat16
I32 = jnp.int32

D_MODEL = 2048
BATCH = 16
SEQ = 256
DEC_BATCH = 2
DEC_SEQ = 4096
PAST_LEN = 512
GRID_W = 64
N_HEADS = 16
N_KV_HEADS = 4
HEAD_DIM = 128
GROUP = N_HEADS // N_KV_HEADS
ATTN_WIDTH = N_HEADS * HEAD_DIM
KV_WIDTH = N_KV_HEADS * HEAD_DIM
WINDOW = 128
CONV_WIDTH = D_MODEL
N_EXPERTS = 64
TOP_K = 8
N_EXPERT_GROUPS = 8
GROUP_SIZE = N_EXPERTS // N_EXPERT_GROUPS
TOPK_GROUPS = 4
D_EXPERT = 512
D_SHARED = 512
ROUTED_SCALE = 2.5
ROPE_THETA = 10000.0
EPS = 1e-6
NEG_INF = -1e30
ATTN_SCALE = HEAD_DIM ** -0.5
LOG2E = 1.4426950408889634

T_P = BATCH * SEQ
T_S = DEC_BATCH * DEC_SEQ
T_ALL = T_P + T_S
MOD_GROUP = 4096
N_MOD = 1 + DEC_BATCH
assert T_P == MOD_GROUP and DEC_SEQ == MOD_GROUP

COL_CB, COL_CC, COL_CX, COL_Q, COL_GC, COL_GA = (i * D_MODEL for i in range(6))
COL_K = 6 * D_MODEL
COL_V = COL_K + KV_WIDTH
IN_WIDTH = COL_V + KV_WIDTH
W_COL_KV = 3 * CONV_WIDTH + ATTN_WIDTH

TM_IN = 1024
TN_IN = 1024
TM = 256
TM_PRE = 512
TM_C = 128
TILE_E = 512
N_SORTED = T_ALL * TOP_K + N_EXPERTS * TILE_E
N_TILES_E = N_SORTED // TILE_E
TMAP_W = 512
CAST_ROWS = 512
VMEM_LIMIT = 56 * 1024 * 1024


def _cparams(sem):
    return pltpu.CompilerParams(dimension_semantics=sem, vmem_limit_bytes=VMEM_LIMIT)


def _silu(x):
    return x * jax.nn.sigmoid(x)


def _load_cast_weight(w_hbm, w_scr, stage, sem):
    rows = stage.shape[0]
    for c in range(w_hbm.shape[0] // rows):
        cp = pltpu.make_async_copy(w_hbm.at[pl.ds(c * rows, rows)], stage, sem)
        cp.start()
        cp.wait()
        w_scr[pl.ds(c * rows, rows), :] = stage[...].astype(BF16)


ADA_TN = 1024
ADA_CHUNK = 256


def _ada_kernel(ct_ref, w_ref, b_ref, o_ref):
    tn = w_ref.shape[1]

    def body(c, accs):
        k0 = pl.multiple_of(c * ADA_CHUNK, ADA_CHUNK)
        wch = w_ref[pl.ds(k0, ADA_CHUNK), :]
        sch = _silu(ct_ref[pl.ds(k0, ADA_CHUNK), :])
        out = []
        for r in range(N_MOD):
            p = wch * sch[:, r:r + 1]
            out.append(accs[r] + p.reshape(ADA_CHUNK // 8, 8, tn).sum(axis=0))
        return tuple(out)

    accs = lax.fori_loop(0, D_MODEL // ADA_CHUNK, body,
                         tuple(jnp.zeros((8, tn), F32) for _ in range(N_MOD)))
    o_ref[...] = jnp.zeros_like(o_ref)
    for r in range(N_MOD):
        o_ref[r:r + 1, :] = jnp.sum(accs[r], axis=0, keepdims=True) + b_ref[...]


def _ada(cond_t, w_ada, b_ada):
    n = w_ada.shape[1]
    return pl.pallas_call(
        _ada_kernel,
        grid=(n // ADA_TN,),
        in_specs=[pl.BlockSpec((D_MODEL, 8), lambda j: (0, 0)),
                  pl.BlockSpec((D_MODEL, ADA_TN), lambda j: (0, j)),
                  pl.BlockSpec((1, ADA_TN), lambda j: (0, j))],
        out_specs=pl.BlockSpec((8, ADA_TN), lambda j: (0, j)),
        out_shape=jax.ShapeDtypeStruct((8, n), F32),
        compiler_params=_cparams(("arbitrary",)),
        name="ada",
    )(cond_t, w_ada, b_ada)


def _rms_mod(x, g, shift, scale):
    ms = jnp.mean(x * x, axis=-1, keepdims=True)
    y = x * lax.rsqrt(ms + EPS) * g
    return y * (1.0 + scale) + shift


def _two_group_specs(tile, width):
    n_ctx = T_P // tile
    return (pl.BlockSpec((tile, width), lambda i: (jnp.minimum(i, n_ctx - 1), 0)),
            pl.BlockSpec((tile, width), lambda i: (jnp.maximum(i - n_ctx, 0), 0)))


def _prenorm_kernel(xp_ref, xs_ref, mod_ref, g_ref, h_ref):
    i = pl.program_id(0)
    shift = mod_ref[0, 0:1, :]
    scale = mod_ref[0, 1:2, :]
    g = g_ref[...]

    @pl.when(i < T_P // TM_PRE)
    def _():
        h_ref[...] = _rms_mod(xp_ref[...], g, shift, scale).astype(BF16)

    @pl.when(i >= T_P // TM_PRE)
    def _():
        h_ref[...] = _rms_mod(xs_ref[...], g, shift, scale).astype(BF16)


def _prenorm(xp, xs, mods3, g_mix):
    tiles_per_group = MOD_GROUP // TM_PRE
    sp, ss = _two_group_specs(TM_PRE, D_MODEL)
    return pl.pallas_call(
        _prenorm_kernel,
        grid=(T_ALL // TM_PRE,),
        in_specs=[sp, ss,
                  pl.BlockSpec((1, 6, D_MODEL), lambda i: (i // tiles_per_group, 0, 0)),
                  pl.BlockSpec((1, D_MODEL), lambda i: (0, 0))],
        out_specs=pl.BlockSpec((TM_PRE, D_MODEL), lambda i: (i, 0)),
        out_shape=jax.ShapeDtypeStruct((T_ALL, D_MODEL), BF16),
        compiler_params=_cparams(("arbitrary",)),
        name="prenorm",
    )(xp, xs, mods3, g_mix)


def _head_norm_rope(a, g, cos2, sin2, scale):
    ms = jnp.mean(a * a, axis=-1, keepdims=True)
    a = a * lax.rsqrt(ms + EPS) * g
    a = a * cos2 + pltpu.roll(a, HEAD_DIM // 2, 1) * sin2
    return a * scale


N_J = IN_WIDTH // TN_IN
J_Q0 = COL_Q // TN_IN
J_Q1 = J_Q0 + ATTN_WIDTH // TN_IN
J_KV = COL_K // TN_IN
assert J_KV == N_J - 1
N_CTX_IN = T_P // TM_IN


def _inproj_kernel(h_ref, w_ref, qg_ref, kg_ref, cos_ref, sin_ref, o_ref, k_ref, v_ref, w_scr):
    j = pl.program_id(0)
    i = pl.program_id(1)

    @pl.when(i == 0)
    def _():
        for c in range(D_MODEL // CAST_ROWS):
            rows = pl.ds(c * CAST_ROWS, CAST_ROWS)
            w_scr[rows, :] = w_ref[rows, :].astype(BF16)

    acc = jnp.dot(h_ref[...], w_scr[...], preferred_element_type=F32)
    is_q = jnp.logical_and(j >= J_Q0, j < J_Q1)
    is_kv = j == J_KV

    @pl.when(jnp.logical_not(jnp.logical_or(is_q, is_kv)))
    def _():
        o_ref[...] = acc.astype(BF16)

    @pl.when(is_q)
    def _():
        cos2 = cos_ref[...]
        sin2 = sin_ref[...]
        g = qg_ref[...]
        for h in range(TN_IN // HEAD_DIM):
            cols = slice(h * HEAD_DIM, (h + 1) * HEAD_DIM)
            o_ref[:, cols] = _head_norm_rope(acc[:, cols], g, cos2, sin2, ATTN_SCALE * LOG2E).astype(BF16)

    @pl.when(is_kv)
    def _():
        cos2 = cos_ref[...]
        sin2 = sin_ref[...]
        g = kg_ref[...]
        is_ctx = i < N_CTX_IN
        for h in range(N_KV_HEADS):
            cols = slice(h * HEAD_DIM, (h + 1) * HEAD_DIM)
            kh = _head_norm_rope(acc[:, cols], g, cos2, sin2, 1.0)
            o_ref[:, cols] = kh.astype(BF16)

            @pl.when(is_ctx)
            def _():
                k_ref[:, cols] = kh

        v = acc[:, KV_WIDTH:]
        o_ref[:, KV_WIDTH:] = v.astype(BF16)

        @pl.when(is_ctx)
        def _():
            v_ref[...] = v


def _inproj(h, w_in, qg, kg, cos_tab, sin_tab):
    pos_tiles = DEC_SEQ // TM_IN

    def tab_map(j, i):
        return (jnp.where(i < N_CTX_IN, 0, 1 + i % pos_tiles), 0)

    def w_map(j, i):
        return (0, jnp.where(j < W_COL_KV // TN_IN, j, jnp.where(j == J_KV, W_COL_KV // TN_IN, j + 1)))

    def kv_map(j, i):
        return (jnp.where(j == J_KV, jnp.minimum(i, N_CTX_IN - 1), 0), 0)

    return pl.pallas_call(
        _inproj_kernel,
        grid=(N_J, T_ALL // TM_IN),
        in_specs=[pl.BlockSpec((TM_IN, D_MODEL), lambda j, i: (i, 0)),
                  pl.BlockSpec((D_MODEL, TN_IN), w_map, pipeline_mode=pl.Buffered(1)),
                  pl.BlockSpec((1, HEAD_DIM), lambda j, i: (0, 0)),
                  pl.BlockSpec((1, HEAD_DIM), lambda j, i: (0, 0)),
                  pl.BlockSpec((TM_IN, HEAD_DIM), tab_map),
                  pl.BlockSpec((TM_IN, HEAD_DIM), tab_map)],
        out_specs=[pl.BlockSpec((TM_IN, TN_IN), lambda j, i: (i, j)),
                   pl.BlockSpec((TM_IN, KV_WIDTH), kv_map),
                   pl.BlockSpec((TM_IN, KV_WIDTH), kv_map)],
        out_shape=[jax.ShapeDtypeStruct((T_ALL, IN_WIDTH), BF16),
                   jax.ShapeDtypeStruct((T_P, KV_WIDTH), F32),
                   jax.ShapeDtypeStruct((T_P, KV_WIDTH), F32)],
        scratch_shapes=[pltpu.VMEM((D_MODEL, TN_IN), BF16)],
        compiler_params=_cparams(("arbitrary", "arbitrary")),
        name="inproj",
    )(h, w_in, qg, kg, cos_tab, sin_tab)


def _dot_nt(a, b):
    return lax.dot_general(a, b, (((1,), (1,)), ((), ())), preferred_element_type=F32)


def _attn_p_kernel(sink_ref, q_ref, k_ref, v_ref, o_ref):
    for kh in range(N_KV_HEADS):
        kcols = slice(kh * HEAD_DIM, (kh + 1) * HEAD_DIM)
        k = k_ref[:, kcols]
        v = v_ref[:, kcols]
        for gi in range(GROUP):
            h = kh * GROUP + gi
            cols = slice(h * HEAD_DIM, (h + 1) * HEAD_DIM)
            s = _dot_nt(q_ref[:, cols], k)
            sk = sink_ref[h] * LOG2E
            m = jnp.maximum(jnp.max(s, axis=-1, keepdims=True), sk)
            p = jnp.exp2(s - m)
            den = jnp.sum(p, axis=-1, keepdims=True) + jnp.exp2(sk - m)
            o = jnp.dot(p.astype(BF16), v, preferred_element_type=F32) / den
            o_ref[:, cols] = o.astype(BF16)


def _attn_p(sink, proj):
    return pl.pallas_call(
        _attn_p_kernel,
        grid=(BATCH,),
        in_specs=[pl.BlockSpec(memory_space=pltpu.SMEM),
                  pl.BlockSpec((SEQ, ATTN_WIDTH), lambda b: (b, COL_Q // ATTN_WIDTH)),
                  pl.BlockSpec((SEQ, KV_WIDTH), lambda b: (b, COL_K // KV_WIDTH)),
                  pl.BlockSpec((SEQ, KV_WIDTH), lambda b: (b, COL_V // KV_WIDTH))],
        out_specs=pl.BlockSpec((SEQ, ATTN_WIDTH), lambda b: (b, 0)),
        out_shape=jax.ShapeDtypeStruct((T_P, ATTN_WIDTH), BF16),
        compiler_params=_cparams(("arbitrary",)),
        name="attn_p",
    )(sink, proj, proj, proj)


QB = 256
BAND = QB + 2 * WINDOW


def _attn_s_kernel(sink_ref, q_ref, k_ref, v_ref, kc_ref, vc_ref, o_ref):
    i = pl.program_id(1)
    start = jnp.clip(i * QB - WINDOW, 0, DEC_SEQ - BAND)
    start = pl.multiple_of(start, WINDOW)
    qpos = i * QB + lax.broadcasted_iota(I32, (QB, BAND), 0)
    kpos = start + lax.broadcasted_iota(I32, (QB, BAND), 1)
    valid = jnp.abs(qpos - kpos) <= WINDOW
    valid_g = jnp.concatenate([valid] * GROUP, axis=0)
    for kh in range(N_KV_HEADS):
        kcols = slice(kh * HEAD_DIM, (kh + 1) * HEAD_DIM)
        kb = k_ref[pl.ds(start, BAND), kcols]
        kc = kc_ref[0, :, kcols].astype(BF16)
        ones = jnp.ones((BAND, HEAD_DIM), BF16)
        vb1 = jnp.concatenate([v_ref[pl.ds(start, BAND), kcols], ones], axis=1)
        vc1 = jnp.concatenate([vc_ref[0, :, kcols].astype(BF16), ones[:PAST_LEN]], axis=1)
        heads = [kh * GROUP + gi for gi in range(GROUP)]
        q = jnp.concatenate([q_ref[:, h * HEAD_DIM:(h + 1) * HEAD_DIM] for h in heads], axis=0)
        sk = jnp.concatenate([jnp.full((QB, 1), sink_ref[h] * LOG2E, F32) for h in heads], axis=0)
        s_loc = jnp.where(valid_g, _dot_nt(q, kb), NEG_INF)
        s_ctx = _dot_nt(q, kc)
        m = jnp.maximum(jnp.maximum(jnp.max(s_loc, axis=-1, keepdims=True),
                                    jnp.max(s_ctx, axis=-1, keepdims=True)), sk)
        p_loc = jnp.exp2(s_loc - m).astype(BF16)
        p_ctx = jnp.exp2(s_ctx - m).astype(BF16)
        o1 = (jnp.dot(p_loc, vb1, preferred_element_type=F32)
              + jnp.dot(p_ctx, vc1, preferred_element_type=F32))
        den = o1[:, HEAD_DIM:HEAD_DIM + 1] + jnp.exp2(sk - m)
        o = o1[:, :HEAD_DIM] / den
        for gi, h in enumerate(heads):
            o_ref[:, h * HEAD_DIM:(h + 1) * HEAD_DIM] = o[gi * QB:(gi + 1) * QB, :].astype(BF16)


def _attn_s(sink, proj, kc, vc):
    nq = DEC_SEQ // QB
    first = T_P // QB
    seq_blk = T_P // DEC_SEQ
    return pl.pallas_call(
        _attn_s_kernel,
        grid=(DEC_BATCH, nq),
        in_specs=[pl.BlockSpec(memory_space=pltpu.SMEM),
                  pl.BlockSpec((QB, ATTN_WIDTH), lambda b, i: (first + b * nq + i, COL_Q // ATTN_WIDTH)),
                  pl.BlockSpec((DEC_SEQ, KV_WIDTH), lambda b, i: (seq_blk + b, COL_K // KV_WIDTH)),
                  pl.BlockSpec((DEC_SEQ, KV_WIDTH), lambda b, i: (seq_blk + b, COL_V // KV_WIDTH)),
                  pl.BlockSpec((1, PAST_LEN, KV_WIDTH), lambda b, i: (b, 0, 0)),
                  pl.BlockSpec((1, PAST_LEN, KV_WIDTH), lambda b, i: (b, 0, 0))],
        out_specs=pl.BlockSpec((QB, ATTN_WIDTH), lambda b, i: (b * nq + i, 0)),
        out_shape=jax.ShapeDtypeStruct((T_S, ATTN_WIDTH), BF16),
        compiler_params=_cparams(("arbitrary", "arbitrary")),
        name="attn_s",
    )(sink, proj, proj, proj, kc, vc)


HALO = 16


def _conv_kernel(cb_ref, cc_ref, cx_ref, ccp_ref, cxp_ref, ccn_ref, cxn_ref, gc_ref, cw_ref, w_hbm, o_ref,
                 w_scr, stage, sem):
    i = pl.program_id(0)

    @pl.when(i == 0)
    def _():
        _load_cast_weight(w_hbm, w_scr, stage, sem)

    n_ctx_tiles = T_P // TM
    per_seq = DEC_SEQ // TM
    is_first = jnp.logical_or(i < n_ctx_tiles, (i - n_ctx_tiles) % per_seq == 0)
    is_last = jnp.logical_or(i < n_ctx_tiles, (i - n_ctx_tiles) % per_seq == per_seq - 1)
    p = cc_ref[...].astype(F32) * cx_ref[...].astype(F32)
    prev_row = ccp_ref[HALO - 1:HALO, :].astype(F32) * cxp_ref[HALO - 1:HALO, :].astype(F32)
    next_row = ccn_ref[0:1, :].astype(F32) * cxn_ref[0:1, :].astype(F32)
    prev_row = jnp.where(is_first, 0.0, prev_row)
    next_row = jnp.where(is_last, 0.0, next_row)
    rows = lax.broadcasted_iota(I32, (TM, 1), 0)
    p_prev = jnp.where(rows == 0, prev_row, pltpu.roll(p, 1, 0))
    p_next = jnp.where(rows == TM - 1, next_row, pltpu.roll(p, TM - 1, 0))
    conv = p_prev * cw_ref[0:1, :] + p * cw_ref[1:2, :] + p_next * cw_ref[2:3, :]
    u = cb_ref[...].astype(F32) * conv
    y = jnp.dot(u.astype(BF16), w_scr[...], preferred_element_type=F32)
    o_ref[...] = (jax.nn.sigmoid(gc_ref[...].astype(F32)) * y).astype(BF16)


def _conv(proj, conv_w, w_conv_out):
    hb = TM // HALO
    last_hb = T_ALL // HALO - 1
    wide = lambda c: pl.BlockSpec((TM, D_MODEL), lambda i: (i, c // D_MODEL))
    prev = lambda c: pl.BlockSpec((HALO, D_MODEL), lambda i: (jnp.maximum(i * hb - 1, 0), c // D_MODEL))
    nxt = lambda c: pl.BlockSpec((HALO, D_MODEL), lambda i: (jnp.minimum((i + 1) * hb, last_hb), c // D_MODEL))
    return pl.pallas_call(
        _conv_kernel,
        grid=(T_ALL // TM,),
        in_specs=[wide(COL_CB), wide(COL_CC), wide(COL_CX),
                  prev(COL_CC), prev(COL_CX), nxt(COL_CC), nxt(COL_CX),
                  wide(COL_GC),
                  pl.BlockSpec((3, CONV_WIDTH), lambda i: (0, 0)),
                  pl.BlockSpec(memory_space=pl.ANY)],
        out_specs=pl.BlockSpec((TM, D_MODEL), lambda i: (i, 0)),
        out_shape=jax.ShapeDtypeStruct((T_ALL, D_MODEL), BF16),
        scratch_shapes=[pltpu.VMEM((CONV_WIDTH, D_MODEL), BF16),
                        pltpu.VMEM((CAST_ROWS, D_MODEL), F32),
                        pltpu.SemaphoreType.DMA(())],
        compiler_params=_cparams(("arbitrary",)),
        name="conv",
    )(proj, proj, proj, proj, proj, proj, proj, proj, conv_w, w_conv_out)


def _merge_kernel(xp_ref, xs_ref, attn_p_ref, attn_s_ref, z1_ref, ga_ref, mod_ref, g2_ref, wr_ref,
                  wao_hbm, wo_hbm, x1_ref, h2_ref, lg_ref, wao_scr, wo_scr, stage, sem):
    i = pl.program_id(0)

    @pl.when(i == 0)
    def _():
        _load_cast_weight(wao_hbm, wao_scr, stage, sem)
        _load_cast_weight(wo_hbm, wo_scr, stage, sem)

    is_ctx = i < T_P // TM
    attn = jnp.where(is_ctx, attn_p_ref[...], attn_s_ref[...])
    x = jnp.where(is_ctx, xp_ref[...], xs_ref[...])
    ya = jnp.dot(attn, wao_scr[...], preferred_element_type=F32)
    z = z1_ref[...].astype(F32) + jax.nn.sigmoid(ga_ref[...].astype(F32)) * ya
    mix = jnp.dot(z.astype(BF16), wo_scr[...], preferred_element_type=F32)
    x1 = x + mod_ref[0, 2:3, :] * mix
    x1_ref[...] = x1
    h = _rms_mod(x1, g2_ref[...], mod_ref[0, 3:4, :], mod_ref[0, 4:5, :])
    h2_ref[...] = h
    h_hi = h.astype(BF16)
    h_lo = (h - h_hi.astype(F32)).astype(BF16)
    wr = wr_ref[...]
    wr_hi32 = wr.astype(BF16).astype(F32)
    wr_lo32 = (wr - wr_hi32).astype(BF16).astype(F32)
    w_both = (wr_hi32 + pltpu.roll(wr_lo32, N_EXPERTS, 1)).astype(BF16)
    both = jnp.dot(h_hi, w_both, preferred_element_type=F32)
    lg_ref[...] = (both + pltpu.roll(both, N_EXPERTS, 1)
                   + jnp.dot(h_lo, wr_hi32.astype(BF16), preferred_element_type=F32))


def _merge(xp, xs, attn_p, attn_s, z1, proj, mods3, g_ffn, wr_pad, w_attn_out, w_out):
    tiles_per_group = MOD_GROUP // TM
    const = lambda shape: pl.BlockSpec(shape, lambda i: (0, 0))
    row = lambda w: pl.BlockSpec((TM, w), lambda i: (i, 0))
    xsp, xss = _two_group_specs(TM, D_MODEL)
    asp, ass = _two_group_specs(TM, ATTN_WIDTH)
    return pl.pallas_call(
        _merge_kernel,
        grid=(T_ALL // TM,),
        in_specs=[xsp, xss, asp, ass, row(D_MODEL),
                  pl.BlockSpec((TM, D_MODEL), lambda i: (i, COL_GA // D_MODEL)),
                  pl.BlockSpec((1, 6, D_MODEL), lambda i: (i // tiles_per_group, 0, 0)),
                  const((1, D_MODEL)), const((D_MODEL, 128)),
                  pl.BlockSpec(memory_space=pl.ANY), pl.BlockSpec(memory_space=pl.ANY)],
        out_specs=[row(D_MODEL), row(D_MODEL), row(128)],
        out_shape=[jax.ShapeDtypeStruct((T_ALL, D_MODEL), F32),
                   jax.ShapeDtypeStruct((T_ALL, D_MODEL), F32),
                   jax.ShapeDtypeStruct((T_ALL, 128), F32)],
        scratch_shapes=[pltpu.VMEM((ATTN_WIDTH, D_MODEL), BF16),
                        pltpu.VMEM((D_MODEL, D_MODEL), BF16),
                        pltpu.VMEM((CAST_ROWS, D_MODEL), F32),
                        pltpu.SemaphoreType.DMA(())],
        compiler_params=_cparams(("arbitrary",)),
        name="merge",
    )(xp, xs, attn_p, attn_s, z1, proj, mods3, g_ffn, wr_pad, w_attn_out, w_out)


def _route1_kernel(lg_ref, bias_ref, enc_ref, cnt_ref):
    i = pl.program_id(0)
    lt = lg_ref[...].T[:N_EXPERTS, :]
    scores = jax.nn.sigmoid(lt)
    biased = scores + bias_ref[...]
    b3 = biased.reshape(N_EXPERT_GROUPS, GROUP_SIZE, TM)
    mi = lax.broadcasted_iota(I32, b3.shape, 1)
    m1 = jnp.max(b3, axis=1, keepdims=True)
    idx1 = jnp.min(jnp.where(b3 == m1, mi, GROUP_SIZE), axis=1, keepdims=True)
    m2 = jnp.max(jnp.where(mi == idx1, -jnp.inf, b3), axis=1, keepdims=True)
    gs = (m1 + m2).reshape(N_EXPERT_GROUPS, TM)
    gidx = lax.broadcasted_iota(I32, gs.shape, 0)
    grank = jnp.zeros(gs.shape, I32)
    for j in range(N_EXPERT_GROUPS):
        gj = gs[j:j + 1, :]
        beats = jnp.logical_or(gj > gs, jnp.logical_and(gj == gs, j < gidx))
        grank = grank + beats.astype(I32)
    gsel = grank < TOPK_GROUPS
    emask = jnp.broadcast_to(gsel[:, None, :], b3.shape).reshape(N_EXPERTS, TM)
    masked = jnp.where(emask, biased, NEG_INF)
    eidx = lax.broadcasted_iota(I32, masked.shape, 0)
    erank = jnp.zeros(masked.shape, I32)
    for j in range(N_EXPERTS):
        vj = masked[j:j + 1, :]
        beats = jnp.logical_or(vj > masked, jnp.logical_and(vj == masked, j < eidx))
        erank = erank + beats.astype(I32)
    sel = erank < TOP_K
    wsel = jnp.where(sel, scores, 0.0)
    den = jnp.sum(wsel, axis=0, keepdims=True)
    wts = wsel / den * ROUTED_SCALE
    enc_ref[...] = jnp.where(sel, wts, -1.0)

    @pl.when(i == 0)
    def _():
        cnt_ref[...] = jnp.zeros_like(cnt_ref)

    cnt = jnp.sum(sel.astype(F32), axis=1, keepdims=True)
    cnt_ref[...] += jnp.broadcast_to(cnt, cnt_ref.shape)


def _route1(logits, bias_col):
    return pl.pallas_call(
        _route1_kernel,
        grid=(T_ALL // TM,),
        in_specs=[pl.BlockSpec((TM, 128), lambda i: (i, 0)),
                  pl.BlockSpec((N_EXPERTS, 1), lambda i: (0, 0))],
        out_specs=[pl.BlockSpec((N_EXPERTS, TM), lambda i: (0, i)),
                   pl.BlockSpec((N_EXPERTS, 128), lambda i: (0, 0))],
        out_shape=[jax.ShapeDtypeStruct((N_EXPERTS, T_ALL), F32),
                   jax.ShapeDtypeStruct((N_EXPERTS, 128), F32)],
        compiler_params=_cparams(("arbitrary",)),
        name="route1",
    )(logits, bias_col)


def _route2_positions(enc_ref, cnt_ref, meta_ref, tmap_ref, carry_ref):
    enc = enc_ref[...]
    sel = enc >= 0.0
    wts = jnp.maximum(enc, 0.0)
    sel_b = sel.astype(BF16)
    ntile = jnp.ceil(cnt_ref[...] * (1.0 / TILE_E))
    er = lax.broadcasted_iota(I32, (N_EXPERTS, N_EXPERTS), 0)
    ec = lax.broadcasted_iota(I32, (N_EXPERTS, N_EXPERTS), 1)
    lower = (ec < er).astype(BF16)
    off_t = jnp.dot(lower, ntile.astype(BF16), preferred_element_type=F32)
    tr = lax.broadcasted_iota(I32, (TM, TM), 0)
    tc = lax.broadcasted_iota(I32, (TM, TM), 1)
    upper = (tr < tc).astype(BF16)
    rank = jnp.dot(sel_b, upper, preferred_element_type=F32) + carry_ref[:, 0:1]
    carry_ref[...] += jnp.broadcast_to(jnp.sum(sel.astype(F32), axis=1, keepdims=True), carry_ref.shape)
    pos = off_t[:, 0:1] * float(TILE_E) + rank
    slot = jnp.dot(lower, sel_b, preferred_element_type=F32)
    rows = []
    for k in range(TOP_K):
        mk = jnp.logical_and(sel, slot == float(k))
        rows.append(jnp.sum(jnp.where(mk, wts, 0.0), axis=0, keepdims=True))
    for k in range(TOP_K):
        mk = jnp.logical_and(sel, slot == float(k))
        rows.append(jnp.sum(jnp.where(mk, pos, 0.0), axis=0, keepdims=True))
    rows.append(jnp.zeros((128 - 2 * TOP_K, TM), F32))
    meta_ref[...] = jnp.concatenate(rows, axis=0).T
    nt = ntile[:, 0:1]
    end_t = off_t[:, 0:1] + nt
    lane = lax.broadcasted_iota(I32, (N_EXPERTS, TMAP_W), 1)
    te = jnp.sum((end_t <= lane.astype(F32)).astype(F32), axis=0, keepdims=True)
    te = jnp.minimum(te, float(N_EXPERTS - 1))
    nact = jnp.sum(nt, axis=0, keepdims=True)
    erow = lax.broadcasted_iota(I32, (N_EXPERTS, TMAP_W), 0).astype(F32)
    nxt = jnp.sum(jnp.where(erow == te, end_t, 0.0), axis=0, keepdims=True)
    tmap_ref[...] = jnp.zeros_like(tmap_ref)
    tmap_ref[0:1, :] = te.astype(I32)
    tmap_ref[1:2, :] = jnp.broadcast_to(nact, (1, TMAP_W)).astype(I32)
    tmap_ref[2:3, :] = nxt.astype(I32)
    return jnp.concatenate(rows[TOP_K:2 * TOP_K], axis=0).astype(I32)


N_TOK_TILES = T_ALL // TM


def _route2_kernel(enc_ref, cnt_ref, init_hbm, meta_ref, tmap_ref, inv_hbm,
                   carry_ref, posv0, posv1, poss0, poss1, inv_smem, psem, isem):
    i = pl.program_id(0)
    posv = (posv0, posv1)
    poss = (poss0, poss1)
    per_token = (1 << TOKEN_BITS) + 1
    per_slot = T_ALL << TOKEN_BITS

    @pl.when(i == 0)
    def _():
        carry_ref[...] = jnp.zeros_like(carry_ref)
        cp = pltpu.make_async_copy(init_hbm, inv_smem, isem)
        cp.start()
        cp.wait()

    def positions(s):
        posv[s][...] = _route2_positions(enc_ref, cnt_ref, meta_ref, tmap_ref, carry_ref)
        pltpu.make_async_copy(posv[s], poss[s], psem.at[s]).start()

    def invert(s):
        pltpu.make_async_copy(posv[s], poss[s], psem.at[s]).wait()
        base = (i - 1) * (TM * per_token)
        for t in range(TM):
            for k in range(TOP_K):
                inv_smem[poss[s][k, t]] = base + (t * per_token + k * per_slot)

    @pl.when(i == 0)
    def _():
        positions(0)

    for s in range(2):
        @pl.when(jnp.logical_and(jnp.logical_and(i >= 1, i < N_TOK_TILES), i % 2 == s))
        def _():
            invert(1 - s)
            positions(s)

    @pl.when(i == N_TOK_TILES)
    def _():
        invert((N_TOK_TILES - 1) % 2)
        cp = pltpu.make_async_copy(inv_smem, inv_hbm, isem)
        cp.start()
        cp.wait()


def _route2(enc, cnt, inv_init):
    last = N_TOK_TILES - 1
    hbm = pl.BlockSpec(memory_space=pl.ANY)
    return pl.pallas_call(
        _route2_kernel,
        grid=(N_TOK_TILES + 1,),
        in_specs=[pl.BlockSpec((N_EXPERTS, TM), lambda i: (0, jnp.minimum(i, last))),
                  pl.BlockSpec((N_EXPERTS, 128), lambda i: (0, 0)),
                  hbm],
        out_specs=[pl.BlockSpec((TM, 128), lambda i: (jnp.minimum(i, last), 0)),
                   pl.BlockSpec((8, TMAP_W), lambda i: (0, 0)),
                   hbm],
        out_shape=[jax.ShapeDtypeStruct((T_ALL, 128), F32),
                   jax.ShapeDtypeStruct((8, TMAP_W), I32),
                   jax.ShapeDtypeStruct((N_SORTED,), I32)],
        scratch_shapes=[pltpu.VMEM((N_EXPERTS, 128), F32),
                        pltpu.VMEM((TOP_K, TM), I32), pltpu.VMEM((TOP_K, TM), I32),
                        pltpu.SMEM((TOP_K, TM), I32), pltpu.SMEM((TOP_K, TM), I32),
                        pltpu.SMEM((N_SORTED,), I32),
                        pltpu.SemaphoreType.DMA((2,)), pltpu.SemaphoreType.DMA(())],
        compiler_params=_cparams(("arbitrary",)),
        name="route2",
    )(enc, cnt, inv_init)


TOKEN_BITS = 14
TOKEN_MASK = (1 << TOKEN_BITS) - 1
NBUF = 3
TRASH_BASE = TOP_K * T_ALL
YS_ROWS = TRASH_BASE + NBUF * TILE_E
assert T_ALL <= 1 << TOKEN_BITS and YS_ROWS << TOKEN_BITS < 1 << 31


def _pad_codes():
    r = jnp.arange(N_SORTED, dtype=I32)
    q = r % TILE_E
    out_row = TRASH_BASE + ((r // TILE_E) % NBUF) * TILE_E + q
    return (out_row << TOKEN_BITS) | q


N_CHUNK = 256
GU_PIECES = D_EXPERT // N_CHUNK
DN_PIECES = D_MODEL // N_CHUNK
N_PIECES = GU_PIECES + DN_PIECES

PIECE_WORK = (D_MODEL * 2,) * GU_PIECES + (D_EXPERT,) * DN_PIECES
PIECE_ROWS = tuple(round(TILE_E * sum(PIECE_WORK[:p]) / sum(PIECE_WORK)) for p in range(N_PIECES + 1))


def _moe_kernel(te_ref, na_ref, nxt_ref, inv_ref, h2_hbm, wg_hbm, wu_hbm, wd_hbm, ys_hbm,
                xbuf0, xbuf1, xbuf2, ybuf0, ybuf1, ybuf2, xb, act, wg_f32, wu_f32, wd_f32,
                wg_scr, wu_scr, wd_scr, gsem, ssem, wsem):
    i = pl.program_id(0)
    na = na_ref[0]
    active = i < na
    xbufs = (xbuf0, xbuf1, xbuf2)
    ybufs = (ybuf0, ybuf1, ybuf2)

    def weight_copies(e):
        return (pltpu.make_async_copy(wg_hbm.at[e], wg_f32, wsem),
                pltpu.make_async_copy(wu_hbm.at[e], wu_f32, wsem),
                pltpu.make_async_copy(wd_hbm.at[e], wd_f32, wsem))

    def gather_row(tile, s, r):
        tok = inv_ref[tile * TILE_E + r] & TOKEN_MASK
        pltpu.make_async_copy(h2_hbm.at[pl.ds(tok, 1)], xbufs[s].at[pl.ds(r, 1)], gsem.at[s]).start()

    def scatter_row(dst, s, r):
        pltpu.make_async_copy(ybufs[s].at[pl.ds(r, 1)], ys_hbm.at[pl.ds(dst, 1)], ssem.at[s]).start()

    def wait_gather(s):
        pltpu.make_async_copy(h2_hbm.at[pl.ds(0, TILE_E)], xbufs[s], gsem.at[s]).wait()

    def wait_scatter(s):
        pltpu.make_async_copy(ybufs[s], ys_hbm.at[pl.ds(0, TILE_E)], ssem.at[s]).wait()

    @pl.when(i == 0)
    def _():
        zeros = ybufs[NBUF - 1]
        zeros[...] = jnp.zeros_like(zeros)
        for m in range(NBUF):
            cp = pltpu.make_async_copy(zeros, ys_hbm.at[pl.ds(TRASH_BASE + m * TILE_E, TILE_E)], ssem.at[0])
            cp.start()
            cp.wait()
        for t in range(2):
            def body(r, carry):
                gather_row(t, t, r)
                return carry

            lax.fori_loop(0, TILE_E, body, 0)

        for cp in weight_copies(te_ref[0]):
            cp.start()

    new_expert = jnp.logical_or(i == 0, te_ref[i] != te_ref[jnp.maximum(i - 1, 0)])

    @pl.when(jnp.logical_and(active, new_expert))
    def _():
        for cp in weight_copies(0):
            cp.wait()
        for c in range(D_MODEL // CAST_ROWS):
            rows = pl.ds(c * CAST_ROWS, CAST_ROWS)
            wg_scr[rows, :] = wg_f32[rows, :].astype(BF16)
            wu_scr[rows, :] = wu_f32[rows, :].astype(BF16)
        wd_scr[...] = wd_f32[...].astype(BF16)
        nxt_tile = nxt_ref[i]

        @pl.when(nxt_tile < na)
        def _():
            for cp in weight_copies(te_ref[nxt_tile]):
                cp.start()

    def compute_tile(slot, write_back):
        s_next = (slot + 2) % NBUF
        s_prev = (slot - 1) % NBUF
        wait_gather(slot)

        @pl.when(i >= 3)
        def _():
            wait_scatter(slot)

        xb[...] = xbufs[slot][...].astype(BF16)
        nxt = jnp.minimum(i + 2, N_TILES_E - 1)
        codes_next = inv_ref.at[pl.ds(pl.multiple_of(nxt * TILE_E, TILE_E), TILE_E)]
        codes_prev = inv_ref.at[pl.ds(pl.multiple_of(jnp.maximum(i - 1, 0) * TILE_E, TILE_E), TILE_E)]

        def issue(piece):
            for r in range(PIECE_ROWS[piece], PIECE_ROWS[piece + 1]):
                tok = codes_next[r] & TOKEN_MASK
                pltpu.make_async_copy(h2_hbm.at[pl.ds(tok, 1)], xbufs[s_next].at[pl.ds(r, 1)],
                                      gsem.at[s_next]).start()
                if write_back:
                    scatter_row(codes_prev[r] >> TOKEN_BITS, s_prev, r)

        for c in range(GU_PIECES):
            cols = slice(c * N_CHUNK, (c + 1) * N_CHUNK)
            g = jnp.dot(xb[...], wg_scr[:, cols], preferred_element_type=F32)
            u = jnp.dot(xb[...], wu_scr[:, cols], preferred_element_type=F32)
            act[:, cols] = (_silu(g) * u).astype(BF16)
            issue(c)
        for c in range(DN_PIECES):
            cols = slice(c * N_CHUNK, (c + 1) * N_CHUNK)
            ybufs[slot][:, cols] = jnp.dot(act[...], wd_scr[:, cols], preferred_element_type=F32)
            issue(GU_PIECES + c)

    def drain(slot):
        s_last = (slot - 1) % NBUF

        def body(r, carry):
            scatter_row(inv_ref[(na - 1) * TILE_E + r] >> TOKEN_BITS, s_last, r)
            return carry

        lax.fori_loop(0, TILE_E, body, 0)
        wait_scatter(s_last)

        @pl.when(na >= 2)
        def _():
            wait_scatter((slot - 2) % NBUF)

        @pl.when(na >= 3)
        def _():
            wait_scatter(slot)

        wait_gather(slot)
        wait_gather((slot + 1) % NBUF)

    @pl.when(jnp.logical_and(active, i == 0))
    def _():
        compute_tile(0, write_back=False)

    for s in range(NBUF):
        @pl.when(jnp.logical_and(jnp.logical_and(active, i > 0), i % NBUF == s))
        def _():
            compute_tile(s, write_back=True)

        @pl.when(jnp.logical_and(i == na, i % NBUF == s))
        def _():
            drain(s)


def _moe(tile_expert, n_active, next_tile, inv, h2, wg, wu, wd):
    hbm = pl.BlockSpec(memory_space=pl.ANY)
    grid_spec = pltpu.PrefetchScalarGridSpec(
        num_scalar_prefetch=4,
        grid=(N_TILES_E + 1,),
        in_specs=[hbm, hbm, hbm, hbm],
        out_specs=hbm,
        scratch_shapes=[pltpu.VMEM((TILE_E, D_MODEL), F32)] * (2 * NBUF) + [
                        pltpu.VMEM((TILE_E, D_MODEL), BF16),
                        pltpu.VMEM((TILE_E, D_EXPERT), BF16),
                        pltpu.VMEM((D_MODEL, D_EXPERT), F32),
                        pltpu.VMEM((D_MODEL, D_EXPERT), F32),
                        pltpu.VMEM((D_EXPERT, D_MODEL), F32),
                        pltpu.VMEM((D_MODEL, D_EXPERT), BF16),
                        pltpu.VMEM((D_MODEL, D_EXPERT), BF16),
                        pltpu.VMEM((D_EXPERT, D_MODEL), BF16),
                        pltpu.SemaphoreType.DMA((NBUF,)),
                        pltpu.SemaphoreType.DMA((NBUF,)),
                        pltpu.SemaphoreType.DMA(())],
    )
    return pl.pallas_call(
        _moe_kernel,
        grid_spec=grid_spec,
        out_shape=jax.ShapeDtypeStruct((YS_ROWS, D_MODEL), F32),
        compiler_params=_cparams(("arbitrary",)),
        name="moe",
    )(tile_expert, n_active, next_tile, inv, h2, wg, wu, wd)


def _combine_kernel(meta_ref, *refs):
    ys_refs = refs[:TOP_K]
    h2_ref, x1_ref, mod_ref, wsg_hbm, wsu_hbm, wsd_hbm = refs[TOP_K:TOP_K + 6]
    op_ref, os_ref, wsg_scr, wsu_scr, wsd_scr, stage_a, stage_b, wsem = refs[TOP_K + 6:]
    i = pl.program_id(0)

    @pl.when(i == 0)
    def _():
        _load_cast_weight(wsg_hbm, wsg_scr, stage_a, wsem)
        _load_cast_weight(wsu_hbm, wsu_scr, stage_a, wsem)
        _load_cast_weight(wsd_hbm, wsd_scr, stage_b, wsem)

    h = h2_ref[...].astype(BF16)
    sg = jnp.dot(h, wsg_scr[...], preferred_element_type=F32)
    su = jnp.dot(h, wsu_scr[...], preferred_element_type=F32)
    moe = jnp.dot((_silu(sg) * su).astype(BF16), wsd_scr[...], preferred_element_type=F32)
    for k in range(TOP_K):
        moe = moe + meta_ref[:, k:k + 1] * ys_refs[k][...]
    y = x1_ref[...] + mod_ref[0, 5:6, :] * moe

    @pl.when(i < T_P // TM_C)
    def _():
        op_ref[...] = y

    @pl.when(i >= T_P // TM_C)
    def _():
        os_ref[...] = y


def _combine(meta, ys, h2, x1, mods3, wsg, wsu, wsd):
    tiles_per_group = MOD_GROUP // TM_C
    row = lambda w: pl.BlockSpec((TM_C, w), lambda i: (i, 0))
    slot_rows = lambda k: pl.BlockSpec((TM_C, D_MODEL), lambda i: (k * (T_ALL // TM_C) + i, 0))
    osp, oss = _two_group_specs(TM_C, D_MODEL)
    hbm = pl.BlockSpec(memory_space=pl.ANY)
    return pl.pallas_call(
        _combine_kernel,
        grid=(T_ALL // TM_C,),
        in_specs=[row(128)] + [slot_rows(k) for k in range(TOP_K)] + [
                  row(D_MODEL), row(D_MODEL),
                  pl.BlockSpec((1, 6, D_MODEL), lambda i: (i // tiles_per_group, 0, 0)),
                  hbm, hbm, hbm],
        out_specs=[osp, oss],
        out_shape=[jax.ShapeDtypeStruct((T_P, D_MODEL), F32),
                   jax.ShapeDtypeStruct((T_S, D_MODEL), F32)],
        scratch_shapes=[pltpu.VMEM((D_MODEL, D_SHARED), BF16),
                        pltpu.VMEM((D_MODEL, D_SHARED), BF16),
                        pltpu.VMEM((D_SHARED, D_MODEL), BF16),
                        pltpu.VMEM((D_MODEL, D_SHARED), F32),
                        pltpu.VMEM((D_SHARED, D_MODEL), F32),
                        pltpu.SemaphoreType.DMA(())],
        compiler_params=_cparams(("arbitrary",)),
        name="combine",
    )(meta, *([ys] * TOP_K), h2, x1, mods3, wsg, wsu, wsd)


def _rope_tables():
    rows = DEC_SEQ // GRID_W
    row = np.repeat(np.arange(rows, dtype=np.float64), GRID_W)
    col = np.tile(np.arange(GRID_W, dtype=np.float64), rows)
    n_freq = HEAD_DIM // 4
    inv = ROPE_THETA ** (-np.arange(n_freq, dtype=np.float64) / n_freq)
    ang = np.concatenate([row[:, None] * inv, col[:, None] * inv], axis=-1)
    cos, sin = np.cos(ang), np.sin(ang)
    cos2 = np.concatenate([cos, cos], axis=-1)
    sin2 = np.concatenate([-sin, sin], axis=-1)
    cos_tab = np.concatenate([np.ones((TM_IN, HEAD_DIM)), cos2], axis=0).astype(np.float32)
    sin_tab = np.concatenate([np.zeros((TM_IN, HEAD_DIM)), sin2], axis=0).astype(np.float32)
    return jnp.asarray(cos_tab), jnp.asarray(sin_tab)


def kernel(x_prompt, x_sample, cache_k, cache_v, c, c_ctx, w_ada, b_ada, norm_mix_g, norm_ffn_g, w_in, conv_w,
           q_norm_g, k_norm_g, attn_sink, w_conv_out, w_attn_out, w_out, router_w, router_bias, w_exp_gate,
           w_exp_up, w_exp_down, w_sh_gate, w_sh_up, w_sh_down):
    l = 0
    xp = x_prompt.reshape(T_P, D_MODEL)
    xs = x_sample.reshape(T_S, D_MODEL)

    cond = jnp.concatenate([c_ctx[None, :], c, jnp.zeros((8 - N_MOD, D_MODEL), F32)], axis=0)
    mods = _ada(cond.T, w_ada[l], b_ada[l][None, :])
    mods3 = mods[:N_MOD].reshape(N_MOD, 6, D_MODEL)

    h = _prenorm(xp, xs, mods3, norm_mix_g[l][None, :])
    cos_tab, sin_tab = _rope_tables()
    proj, k32, v32 = _inproj(h, w_in[l], q_norm_g[l][None, :], k_norm_g[l][None, :], cos_tab, sin_tab)

    sink = attn_sink[l]
    kc = cache_k[:, l].reshape(DEC_BATCH, PAST_LEN, KV_WIDTH)
    vc = cache_v[:, l].reshape(DEC_BATCH, PAST_LEN, KV_WIDTH)
    attn_p = _attn_p(sink, proj)
    attn_s = _attn_s(sink, proj, kc, vc)

    z1 = _conv(proj, conv_w[l], w_conv_out[l])

    wr_pad = jnp.pad(router_w[l], ((0, 0), (0, 128 - N_EXPERTS)))
    x1, h2, logits = _merge(xp, xs, attn_p, attn_s, z1, proj, mods3, norm_ffn_g[l][None, :], wr_pad,
                            w_attn_out[l], w_out[l])

    enc, cnt = _route1(logits, router_bias[l][:, None])
    meta, tmap, inv = _route2(enc, cnt, _pad_codes())
    tile_expert = tmap[0]
    n_active = tmap[1, :1]

    ys = _moe(tile_expert, n_active, tmap[2], inv, h2, w_exp_gate[l], w_exp_up[l], w_exp_down[l])
    y_p, y_s = _combine(meta, ys, h2, x1, mods3, w_sh_gate[l], w_sh_up[l], w_sh_down[l])

    y_prompt = y_p.reshape(BATCH, SEQ, D_MODEL)
    y_sample = y_s.reshape(DEC_BATCH, DEC_SEQ, D_MODEL)
    new_k = k32.reshape(BATCH, 1, SEQ, N_KV_HEADS, HEAD_DIM)
    new_v = v32.reshape(BATCH, 1, SEQ, N_KV_HEADS, HEAD_DIM)
    return (y_prompt, y_sample, new_k, new_v)
```

```python
import jax
import jax.numpy as jnp
import numpy as np
from jax import lax
from jax.experimental import pallas as pl
from jax.experimental.pallas import tpu as pltpu

F32 = jnp.float32
BF16 = jnp.bfloat16
I32 = jnp.int32

D_MODEL = 2048
BATCH = 16
SEQ = 256
DEC_BATCH = 2
DEC_SEQ = 4096
PAST_LEN = 512
GRID_W = 64
N_HEADS = 16
N_KV_HEADS = 4
HEAD_DIM = 128
GROUP = N_HEADS // N_KV_HEADS
ATTN_WIDTH = N_HEADS * HEAD_DIM
KV_WIDTH = N_KV_HEADS * HEAD_DIM
WINDOW = 128
CONV_WIDTH = D_MODEL
N_EXPERTS = 64
TOP_K = 8
N_EXPERT_GROUPS = 8
GROUP_SIZE = N_EXPERTS // N_EXPERT_GROUPS
TOPK_GROUPS = 4
D_EXPERT = 512
D_SHARED = 512
ROUTED_SCALE = 2.5
ROPE_THETA = 10000.0
EPS = 1e-6
NEG_INF = -1e30
ATTN_SCALE = HEAD_DIM ** -0.5
LOG2E = 1.4426950408889634

T_P = BATCH * SEQ
T_S = DEC_BATCH * DEC_SEQ
T_ALL = T_P + T_S
MOD_GROUP = 4096
N_MOD = 1 + DEC_BATCH
assert T_P == MOD_GROUP and DEC_SEQ == MOD_GROUP

COL_CB, COL_CC, COL_CX, COL_Q, COL_GC, COL_GA = (i * D_MODEL for i in range(6))
COL_K = 6 * D_MODEL
COL_V = COL_K + KV_WIDTH
IN_WIDTH = COL_V + KV_WIDTH
W_COL_KV = 3 * CONV_WIDTH + ATTN_WIDTH

TM_IN = 1024
TN_IN = 1024
TM = 256
TM_PRE = 512
TM_C = 128
TILE_E = 512
N_SORTED = T_ALL * TOP_K + N_EXPERTS * TILE_E
N_TILES_E = N_SORTED // TILE_E
TMAP_W = 512
CAST_ROWS = 512
VMEM_LIMIT = 56 * 1024 * 1024


def _cparams(sem):
    return pltpu.CompilerParams(dimension_semantics=sem, vmem_limit_bytes=VMEM_LIMIT)


def _silu(x):
    return x * jax.nn.sigmoid(x)


def _load_cast_weight(w_hbm, w_scr, stage, sem):
    rows = stage.shape[0]
    for c in range(w_hbm.shape[0] // rows):
        cp = pltpu.make_async_copy(w_hbm.at[pl.ds(c * rows, rows)], stage, sem)
        cp.start()
        cp.wait()
        w_scr[pl.ds(c * rows, rows), :] = stage[...].astype(BF16)


ADA_TN = 1024
ADA_CHUNK = 256


def _ada_kernel(ct_ref, w_ref, b_ref, o_ref):
    tn = w_ref.shape[1]

    def body(c, accs):
        k0 = pl.multiple_of(c * ADA_CHUNK, ADA_CHUNK)
        wch = w_ref[pl.ds(k0, ADA_CHUNK), :]
        sch = _silu(ct_ref[pl.ds(k0, ADA_CHUNK), :])
        out = []
        for r in range(N_MOD):
            p = wch * sch[:, r:r + 1]
            out.append(accs[r] + p.reshape(ADA_CHUNK // 8, 8, tn).sum(axis=0))
        return tuple(out)

    accs = lax.fori_loop(0, D_MODEL // ADA_CHUNK, body,
                         tuple(jnp.zeros((8, tn), F32) for _ in range(N_MOD)))
    o_ref[...] = jnp.zeros_like(o_ref)
    for r in range(N_MOD):
        o_ref[r:r + 1, :] = jnp.sum(accs[r], axis=0, keepdims=True) + b_ref[...]


def _ada(cond_t, w_ada, b_ada):
    n = w_ada.shape[1]
    return pl.pallas_call(
        _ada_kernel,
        grid=(n // ADA_TN,),
        in_specs=[pl.BlockSpec((D_MODEL, 8), lambda j: (0, 0)),
                  pl.BlockSpec((D_MODEL, ADA_TN), lambda j: (0, j)),
                  pl.BlockSpec((1, ADA_TN), lambda j: (0, j))],
        out_specs=pl.BlockSpec((8, ADA_TN), lambda j: (0, j)),
        out_shape=jax.ShapeDtypeStruct((8, n), F32),
        compiler_params=_cparams(("arbitrary",)),
        name="ada",
    )(cond_t, w_ada, b_ada)


def _rms_mod(x, g, shift, scale):
    ms = jnp.mean(x * x, axis=-1, keepdims=True)
    y = x * lax.rsqrt(ms + EPS) * g
    return y * (1.0 + scale) + shift


def _two_group_specs(tile, width):
    n_ctx = T_P // tile
    return (pl.BlockSpec((tile, width), lambda i: (jnp.minimum(i, n_ctx - 1), 0)),
            pl.BlockSpec((tile, width), lambda i: (jnp.maximum(i - n_ctx, 0), 0)))


def _prenorm_kernel(xp_ref, xs_ref, mod_ref, g_ref, h_ref):
    i = pl.program_id(0)
    shift = mod_ref[0, 0:1, :]
    scale = mod_ref[0, 1:2, :]
    g = g_ref[...]

    @pl.when(i < T_P // TM_PRE)
    def _():
        h_ref[...] = _rms_mod(xp_ref[...], g, shift, scale).astype(BF16)

    @pl.when(i >= T_P // TM_PRE)
    def _():
        h_ref[...] = _rms_mod(xs_ref[...], g, shift, scale).astype(BF16)


def _prenorm(xp, xs, mods3, g_mix):
    tiles_per_group = MOD_GROUP // TM_PRE
    sp, ss = _two_group_specs(TM_PRE, D_MODEL)
    return pl.pallas_call(
        _prenorm_kernel,
        grid=(T_ALL // TM_PRE,),
        in_specs=[sp, ss,
                  pl.BlockSpec((1, 6, D_MODEL), lambda i: (i // tiles_per_group, 0, 0)),
                  pl.BlockSpec((1, D_MODEL), lambda i: (0, 0))],
        out_specs=pl.BlockSpec((TM_PRE, D_MODEL), lambda i: (i, 0)),
        out_shape=jax.ShapeDtypeStruct((T_ALL, D_MODEL), BF16),
        compiler_params=_cparams(("arbitrary",)),
        name="prenorm",
    )(xp, xs, mods3, g_mix)


def _head_norm_rope(a, g, cos2, sin2, scale):
    ms = jnp.mean(a * a, axis=-1, keepdims=True)
    a = a * lax.rsqrt(ms + EPS) * g
    a = a * cos2 + pltpu.roll(a, HEAD_DIM // 2, 1) * sin2
    return a * scale


N_J = IN_WIDTH // TN_IN
J_Q0 = COL_Q // TN_IN
J_Q1 = J_Q0 + ATTN_WIDTH // TN_IN
J_KV = COL_K // TN_IN
assert J_KV == N_J - 1
N_CTX_IN = T_P // TM_IN


def _inproj_kernel(h_ref, w_ref, qg_ref, kg_ref, cos_ref, sin_ref, o_ref, k_ref, v_ref, w_scr):
    j = pl.program_id(0)
    i = pl.program_id(1)

    @pl.when(i == 0)
    def _():
        for c in range(D_MODEL // CAST_ROWS):
            rows = pl.ds(c * CAST_ROWS, CAST_ROWS)
            w_scr[rows, :] = w_ref[rows, :].astype(BF16)

    acc = jnp.dot(h_ref[...], w_scr[...], preferred_element_type=F32)
    is_q = jnp.logical_and(j >= J_Q0, j < J_Q1)
    is_kv = j == J_KV

    @pl.when(jnp.logical_not(jnp.logical_or(is_q, is_kv)))
    def _():
        o_ref[...] = acc.astype(BF16)

    @pl.when(is_q)
    def _():
        cos2 = cos_ref[...]
        sin2 = sin_ref[...]
        g = qg_ref[...]
        for h in range(TN_IN // HEAD_DIM):
            cols = slice(h * HEAD_DIM, (h + 1) * HEAD_DIM)
            o_ref[:, cols] = _head_norm_rope(acc[:, cols], g, cos2, sin2, ATTN_SCALE * LOG2E).astype(BF16)

    @pl.when(is_kv)
    def _():
        cos2 = cos_ref[...]
        sin2 = sin_ref[...]
        g = kg_ref[...]
        is_ctx = i < N_CTX_IN
        for h in range(N_KV_HEADS):
            cols = slice(h * HEAD_DIM, (h + 1) * HEAD_DIM)
            kh = _head_norm_rope(acc[:, cols], g, cos2, sin2, 1.0)
            o_ref[:, cols] = kh.astype(BF16)

            @pl.when(is_ctx)
            def _():
                k_ref[:, cols] = kh

        v = acc[:, KV_WIDTH:]
        o_ref[:, KV_WIDTH:] = v.astype(BF16)

        @pl.when(is_ctx)
        def _():
            v_ref[...] = v


def _inproj(h, w_in, qg, kg, cos_tab, sin_tab):
    pos_tiles = DEC_SEQ // TM_IN

    def tab_map(j, i):
        return (jnp.where(i < N_CTX_IN, 0, 1 + i % pos_tiles), 0)

    def w_map(j, i):
        return (0, jnp.where(j < W_COL_KV // TN_IN, j, jnp.where(j == J_KV, W_COL_KV // TN_IN, j + 1)))

    def kv_map(j, i):
        return (jnp.where(j == J_KV, jnp.minimum(i, N_CTX_IN - 1), 0), 0)

    return pl.pallas_call(
        _inproj_kernel,
        grid=(N_J, T_ALL // TM_IN),
        in_specs=[pl.BlockSpec((TM_IN, D_MODEL), lambda j, i: (i, 0)),
                  pl.BlockSpec((D_MODEL, TN_IN), w_map, pipeline_mode=pl.Buffered(1)),
                  pl.BlockSpec((1, HEAD_DIM), lambda j, i: (0, 0)),
                  pl.BlockSpec((1, HEAD_DIM), lambda j, i: (0, 0)),
                  pl.BlockSpec((TM_IN, HEAD_DIM), tab_map),
                  pl.BlockSpec((TM_IN, HEAD_DIM), tab_map)],
        out_specs=[pl.BlockSpec((TM_IN, TN_IN), lambda j, i: (i, j)),
                   pl.BlockSpec((TM_IN, KV_WIDTH), kv_map),
                   pl.BlockSpec((TM_IN, KV_WIDTH), kv_map)],
        out_shape=[jax.ShapeDtypeStruct((T_ALL, IN_WIDTH), BF16),
                   jax.ShapeDtypeStruct((T_P, KV_WIDTH), F32),
                   jax.ShapeDtypeStruct((T_P, KV_WIDTH), F32)],
        scratch_shapes=[pltpu.VMEM((D_MODEL, TN_IN), BF16)],
        compiler_params=_cparams(("arbitrary", "arbitrary")),
        name="inproj",
    )(h, w_in, qg, kg, cos_tab, sin_tab)


def _dot_nt(a, b):
    return lax.dot_general(a, b, (((1,), (1,)), ((), ())), preferred_element_type=F32)


def _attn_p_kernel(sink_ref, q_ref, k_ref, v_ref, o_ref):
    for kh in range(N_KV_HEADS):
        kcols = slice(kh * HEAD_DIM, (kh + 1) * HEAD_DIM)
        k = k_ref[:, kcols]
        v = v_ref[:, kcols]
        for gi in range(GROUP):
            h = kh * GROUP + gi
            cols = slice(h * HEAD_DIM, (h + 1) * HEAD_DIM)
            s = _dot_nt(q_ref[:, cols], k)
            sk = sink_ref[h] * LOG2E
            m = jnp.maximum(jnp.max(s, axis=-1, keepdims=True), sk)
            p = jnp.exp2(s - m)
            den = jnp.sum(p, axis=-1, keepdims=True) + jnp.exp2(sk - m)
            o = jnp.dot(p.astype(BF16), v, preferred_element_type=F32) / den
            o_ref[:, cols] = o.astype(BF16)


def _attn_p(sink, proj):
    return pl.pallas_call(
        _attn_p_kernel,
        grid=(BATCH,),
        in_specs=[pl.BlockSpec(memory_space=pltpu.SMEM),
                  pl.BlockSpec((SEQ, ATTN_WIDTH), lambda b: (b, COL_Q // ATTN_WIDTH)),
                  pl.BlockSpec((SEQ, KV_WIDTH), lambda b: (b, COL_K // KV_WIDTH)),
                  pl.BlockSpec((SEQ, KV_WIDTH), lambda b: (b, COL_V // KV_WIDTH))],
        out_specs=pl.BlockSpec((SEQ, ATTN_WIDTH), lambda b: (b, 0)),
        out_shape=jax.ShapeDtypeStruct((T_P, ATTN_WIDTH), BF16),
        compiler_params=_cparams(("arbitrary",)),
        name="attn_p",
    )(sink, proj, proj, proj)


QB = 256
BAND = QB + 2 * WINDOW


def _attn_s_kernel(sink_ref, q_ref, k_ref, v_ref, kc_ref, vc_ref, o_ref):
    i = pl.program_id(1)
    start = jnp.clip(i * QB - WINDOW, 0, DEC_SEQ - BAND)
    start = pl.multiple_of(start, WINDOW)
    qpos = i * QB + lax.broadcasted_iota(I32, (QB, BAND), 0)
    kpos = start + lax.broadcasted_iota(I32, (QB, BAND), 1)
    valid = jnp.abs(qpos - kpos) <= WINDOW
    valid_g = jnp.concatenate([valid] * GROUP, axis=0)
    for kh in range(N_KV_HEADS):
        kcols = slice(kh * HEAD_DIM, (kh + 1) * HEAD_DIM)
        kb = k_ref[pl.ds(start, BAND), kcols]
        kc = kc_ref[0, :, kcols].astype(BF16)
        ones = jnp.ones((BAND, HEAD_DIM), BF16)
        vb1 = jnp.concatenate([v_ref[pl.ds(start, BAND), kcols], ones], axis=1)
        vc1 = jnp.concatenate([vc_ref[0, :, kcols].astype(BF16), ones[:PAST_LEN]], axis=1)
        heads = [kh * GROUP + gi for gi in range(GROUP)]
        q = jnp.concatenate([q_ref[:, h * HEAD_DIM:(h + 1) * HEAD_DIM] for h in heads], axis=0)
        sk = jnp.concatenate([jnp.full((QB, 1), sink_ref[h] * LOG2E, F32) for h in heads], axis=0)
        s_loc = jnp.where(valid_g, _dot_nt(q, kb), NEG_INF)
        s_ctx = _dot_nt(q, kc)
        m = jnp.maximum(jnp.maximum(jnp.max(s_loc, axis=-1, keepdims=True),
                                    jnp.max(s_ctx, axis=-1, keepdims=True)), sk)
        p_loc = jnp.exp2(s_loc - m).astype(BF16)
        p_ctx = jnp.exp2(s_ctx - m).astype(BF16)
        o1 = (jnp.dot(p_loc, vb1, preferred_element_type=F32)
              + jnp.dot(p_ctx, vc1, preferred_element_type=F32))
        den = o1[:, HEAD_DIM:HEAD_DIM + 1] + jnp.exp2(sk - m)
        o = o1[:, :HEAD_DIM] / den
        for gi, h in enumerate(heads):
            o_ref[:, h * HEAD_DIM:(h + 1) * HEAD_DIM] = o[gi * QB:(gi + 1) * QB, :].astype(BF16)


def _attn_s(sink, proj, kc, vc):
    nq = DEC_SEQ // QB
    first = T_P // QB
    seq_blk = T_P // DEC_SEQ
    return pl.pallas_call(
        _attn_s_kernel,
        grid=(DEC_BATCH, nq),
        in_specs=[pl.BlockSpec(memory_space=pltpu.SMEM),
                  pl.BlockSpec((QB, ATTN_WIDTH), lambda b, i: (first + b * nq + i, COL_Q // ATTN_WIDTH)),
                  pl.BlockSpec((DEC_SEQ, KV_WIDTH), lambda b, i: (seq_blk + b, COL_K // KV_WIDTH)),
                  pl.BlockSpec((DEC_SEQ, KV_WIDTH), lambda b, i: (seq_blk + b, COL_V // KV_WIDTH)),
                  pl.BlockSpec((1, PAST_LEN, KV_WIDTH), lambda b, i: (b, 0, 0)),
                  pl.BlockSpec((1, PAST_LEN, KV_WIDTH), lambda b, i: (b, 0, 0))],
        out_specs=pl.BlockSpec((QB, ATTN_WIDTH), lambda b, i: (b * nq + i, 0)),
        out_shape=jax.ShapeDtypeStruct((T_S, ATTN_WIDTH), BF16),
        compiler_params=_cparams(("arbitrary", "arbitrary")),
        name="attn_s",
    )(sink, proj, proj, proj, kc, vc)


HALO = 16


def _conv_kernel(cb_ref, cc_ref, cx_ref, ccp_ref, cxp_ref, ccn_ref, cxn_ref, gc_ref, cw_ref, w_hbm, o_ref,
                 w_scr, stage, sem):
    i = pl.program_id(0)

    @pl.when(i == 0)
    def _():
        _load_cast_weight(w_hbm, w_scr, stage, sem)

    n_ctx_tiles = T_P // TM
    per_seq = DEC_SEQ // TM
    is_first = jnp.logical_or(i < n_ctx_tiles, (i - n_ctx_tiles) % per_seq == 0)
    is_last = jnp.logical_or(i < n_ctx_tiles, (i - n_ctx_tiles) % per_seq == per_seq - 1)
    p = cc_ref[...].astype(F32) * cx_ref[...].astype(F32)
    prev_row = ccp_ref[HALO - 1:HALO, :].astype(F32) * cxp_ref[HALO - 1:HALO, :].astype(F32)
    next_row = ccn_ref[0:1, :].astype(F32) * cxn_ref[0:1, :].astype(F32)
    prev_row = jnp.where(is_first, 0.0, prev_row)
    next_row = jnp.where(is_last, 0.0, next_row)
    rows = lax.broadcasted_iota(I32, (TM, 1), 0)
    p_prev = jnp.where(rows == 0, prev_row, pltpu.roll(p, 1, 0))
    p_next = jnp.where(rows == TM - 1, next_row, pltpu.roll(p, TM - 1, 0))
    conv = p_prev * cw_ref[0:1, :] + p * cw_ref[1:2, :] + p_next * cw_ref[2:3, :]
    u = cb_ref[...].astype(F32) * conv
    y = jnp.dot(u.astype(BF16), w_scr[...], preferred_element_type=F32)
    o_ref[...] = (jax.nn.sigmoid(gc_ref[...].astype(F32)) * y).astype(BF16)


def _conv(proj, conv_w, w_conv_out):
    hb = TM // HALO
    last_hb = T_ALL // HALO - 1
    wide = lambda c: pl.BlockSpec((TM, D_MODEL), lambda i: (i, c // D_MODEL))
    prev = lambda c: pl.BlockSpec((HALO, D_MODEL), lambda i: (jnp.maximum(i * hb - 1, 0), c // D_MODEL))
    nxt = lambda c: pl.BlockSpec((HALO, D_MODEL), lambda i: (jnp.minimum((i + 1) * hb, last_hb), c // D_MODEL))
    return pl.pallas_call(
        _conv_kernel,
        grid=(T_ALL // TM,),
        in_specs=[wide(COL_CB), wide(COL_CC), wide(COL_CX),
                  prev(COL_CC), prev(COL_CX), nxt(COL_CC), nxt(COL_CX),
                  wide(COL_GC),
                  pl.BlockSpec((3, CONV_WIDTH), lambda i: (0, 0)),
                  pl.BlockSpec(memory_space=pl.ANY)],
        out_specs=pl.BlockSpec((TM, D_MODEL), lambda i: (i, 0)),
        out_shape=jax.ShapeDtypeStruct((T_ALL, D_MODEL), BF16),
        scratch_shapes=[pltpu.VMEM((CONV_WIDTH, D_MODEL), BF16),
                        pltpu.VMEM((CAST_ROWS, D_MODEL), F32),
                        pltpu.SemaphoreType.DMA(())],
        compiler_params=_cparams(("arbitrary",)),
        name="conv",
    )(proj, proj, proj, proj, proj, proj, proj, proj, conv_w, w_conv_out)


def _merge_kernel(xp_ref, xs_ref, attn_p_ref, attn_s_ref, z1_ref, ga_ref, mod_ref, g2_ref, wr_ref,
                  wao_hbm, wo_hbm, x1_ref, h2_ref, lg_ref, wao_scr, wo_scr, stage, sem):
    i = pl.program_id(0)

    @pl.when(i == 0)
    def _():
        _load_cast_weight(wao_hbm, wao_scr, stage, sem)
        _load_cast_weight(wo_hbm, wo_scr, stage, sem)

    is_ctx = i < T_P // TM
    attn = jnp.where(is_ctx, attn_p_ref[...], attn_s_ref[...])
    x = jnp.where(is_ctx, xp_ref[...], xs_ref[...])
    ya = jnp.dot(attn, wao_scr[...], preferred_element_type=F32)
    z = z1_ref[...].astype(F32) + jax.nn.sigmoid(ga_ref[...].astype(F32)) * ya
    mix = jnp.dot(z.astype(BF16), wo_scr[...], preferred_element_type=F32)
    x1 = x + mod_ref[0, 2:3, :] * mix
    x1_ref[...] = x1
    h = _rms_mod(x1, g2_ref[...], mod_ref[0, 3:4, :], mod_ref[0, 4:5, :])
    h2_ref[...] = h
    h_hi = h.astype(BF16)
    h_lo = (h - h_hi.astype(F32)).astype(BF16)
    wr = wr_ref[...]
    wr_hi32 = wr.astype(BF16).astype(F32)
    wr_lo32 = (wr - wr_hi32).astype(BF16).astype(F32)
    w_both = (wr_hi32 + pltpu.roll(wr_lo32, N_EXPERTS, 1)).astype(BF16)
    both = jnp.dot(h_hi, w_both, preferred_element_type=F32)
    lg_ref[...] = (both + pltpu.roll(both, N_EXPERTS, 1)
                   + jnp.dot(h_lo, wr_hi32.astype(BF16), preferred_element_type=F32))


def _merge(xp, xs, attn_p, attn_s, z1, proj, mods3, g_ffn, wr_pad, w_attn_out, w_out):
    tiles_per_group = MOD_GROUP // TM
    const = lambda shape: pl.BlockSpec(shape, lambda i: (0, 0))
    row = lambda w: pl.BlockSpec((TM, w), lambda i: (i, 0))
    xsp, xss = _two_group_specs(TM, D_MODEL)
    asp, ass = _two_group_specs(TM, ATTN_WIDTH)
    return pl.pallas_call(
        _merge_kernel,
        grid=(T_ALL // TM,),
        in_specs=[xsp, xss, asp, ass, row(D_MODEL),
                  pl.BlockSpec((TM, D_MODEL), lambda i: (i, COL_GA // D_MODEL)),
                  pl.BlockSpec((1, 6, D_MODEL), lambda i: (i // tiles_per_group, 0, 0)),
                  const((1, D_MODEL)), const((D_MODEL, 128)),
                  pl.BlockSpec(memory_space=pl.ANY), pl.BlockSpec(memory_space=pl.ANY)],
        out_specs=[row(D_MODEL), row(D_MODEL), row(128)],
        out_shape=[jax.ShapeDtypeStruct((T_ALL, D_MODEL), F32),
                   jax.ShapeDtypeStruct((T_ALL, D_MODEL), F32),
                   jax.ShapeDtypeStruct((T_ALL, 128), F32)],
        scratch_shapes=[pltpu.VMEM((ATTN_WIDTH, D_MODEL), BF16),
                        pltpu.VMEM((D_MODEL, D_MODEL), BF16),
                        pltpu.VMEM((CAST_ROWS, D_MODEL), F32),
                        pltpu.SemaphoreType.DMA(())],
        compiler_params=_cparams(("arbitrary",)),
        name="merge",
    )(xp, xs, attn_p, attn_s, z1, proj, mods3, g_ffn, wr_pad, w_attn_out, w_out)


def _route1_kernel(lg_ref, bias_ref, enc_ref, cnt_ref):
    i = pl.program_id(0)
    lt = lg_ref[...].T[:N_EXPERTS, :]
    scores = jax.nn.sigmoid(lt)
    biased = scores + bias_ref[...]
    b3 = biased.reshape(N_EXPERT_GROUPS, GROUP_SIZE, TM)
    mi = lax.broadcasted_iota(I32, b3.shape, 1)
    m1 = jnp.max(b3, axis=1, keepdims=True)
    idx1 = jnp.min(jnp.where(b3 == m1, mi, GROUP_SIZE), axis=1, keepdims=True)
    m2 = jnp.max(jnp.where(mi == idx1, -jnp.inf, b3), axis=1, keepdims=True)
    gs = (m1 + m2).reshape(N_EXPERT_GROUPS, TM)
    gidx = lax.broadcasted_iota(I32, gs.shape, 0)
    grank = jnp.zeros(gs.shape, I32)
    for j in range(N_EXPERT_GROUPS):
        gj = gs[j:j + 1, :]
        beats = jnp.logical_or(gj > gs, jnp.logical_and(gj == gs, j < gidx))
        grank = grank + beats.astype(I32)
    gsel = grank < TOPK_GROUPS
    emask = jnp.broadcast_to(gsel[:, None, :], b3.shape).reshape(N_EXPERTS, TM)
    masked = jnp.where(emask, biased, NEG_INF)
    eidx = lax.broadcasted_iota(I32, masked.shape, 0)
    erank = jnp.zeros(masked.shape, I32)
    for j in range(N_EXPERTS):
        vj = masked[j:j + 1, :]
        beats = jnp.logical_or(vj > masked, jnp.logical_and(vj == masked, j < eidx))
        erank = erank + beats.astype(I32)
    sel = erank < TOP_K
    wsel = jnp.where(sel, scores, 0.0)
    den = jnp.sum(wsel, axis=0, keepdims=True)
    wts = wsel / den * ROUTED_SCALE
    enc_ref[...] = jnp.where(sel, wts, -1.0)

    @pl.when(i == 0)
    def _():
        cnt_ref[...] = jnp.zeros_like(cnt_ref)

    cnt = jnp.sum(sel.astype(F32), axis=1, keepdims=True)
    cnt_ref[...] += jnp.broadcast_to(cnt, cnt_ref.shape)


def _route1(logits, bias_col):
    return pl.pallas_call(
        _route1_kernel,
        grid=(T_ALL // TM,),
        in_specs=[pl.BlockSpec((TM, 128), lambda i: (i, 0)),
                  pl.BlockSpec((N_EXPERTS, 1), lambda i: (0, 0))],
        out_specs=[pl.BlockSpec((N_EXPERTS, TM), lambda i: (0, i)),
                   pl.BlockSpec((N_EXPERTS, 128), lambda i: (0, 0))],
        out_shape=[jax.ShapeDtypeStruct((N_EXPERTS, T_ALL), F32),
                   jax.ShapeDtypeStruct((N_EXPERTS, 128), F32)],
        compiler_params=_cparams(("arbitrary",)),
        name="route1",
    )(logits, bias_col)


def _route2_positions(enc_ref, cnt_ref, meta_ref, tmap_ref, carry_ref):
    enc = enc_ref[...]
    sel = enc >= 0.0
    wts = jnp.maximum(enc, 0.0)
    sel_b = sel.astype(BF16)
    ntile = jnp.ceil(cnt_ref[...] * (1.0 / TILE_E))
    er = lax.broadcasted_iota(I32, (N_EXPERTS, N_EXPERTS), 0)
    ec = lax.broadcasted_iota(I32, (N_EXPERTS, N_EXPERTS), 1)
    lower = (ec < er).astype(BF16)
    off_t = jnp.dot(lower, ntile.astype(BF16), preferred_element_type=F32)
    tr = lax.broadcasted_iota(I32, (TM, TM), 0)
    tc = lax.broadcasted_iota(I32, (TM, TM), 1)
    upper = (tr < tc).astype(BF16)
    rank = jnp.dot(sel_b, upper, preferred_element_type=F32) + carry_ref[:, 0:1]
    carry_ref[...] += jnp.broadcast_to(jnp.sum(sel.astype(F32), axis=1, keepdims=True), carry_ref.shape)
    pos = off_t[:, 0:1] * float(TILE_E) + rank
    slot = jnp.dot(lower, sel_b, preferred_element_type=F32)
    rows = []
    for k in range(TOP_K):
        mk = jnp.logical_and(sel, slot == float(k))
        rows.append(jnp.sum(jnp.where(mk, wts, 0.0), axis=0, keepdims=True))
    for k in range(TOP_K):
        mk = jnp.logical_and(sel, slot == float(k))
        rows.append(jnp.sum(jnp.where(mk, pos, 0.0), axis=0, keepdims=True))
    rows.append(jnp.zeros((128 - 2 * TOP_K, TM), F32))
    meta_ref[...] = jnp.concatenate(rows, axis=0).T
    nt = ntile[:, 0:1]
    end_t = off_t[:, 0:1] + nt
    lane = lax.broadcasted_iota(I32, (N_EXPERTS, TMAP_W), 1)
    te = jnp.sum((end_t <= lane.astype(F32)).astype(F32), axis=0, keepdims=True)
    te = jnp.minimum(te, float(N_EXPERTS - 1))
    nact = jnp.sum(nt, axis=0, keepdims=True)
    erow = lax.broadcasted_iota(I32, (N_EXPERTS, TMAP_W), 0).astype(F32)
    nxt = jnp.sum(jnp.where(erow == te, end_t, 0.0), axis=0, keepdims=True)
    tmap_ref[...] = jnp.zeros_like(tmap_ref)
    tmap_ref[0:1, :] = te.astype(I32)
    tmap_ref[1:2, :] = jnp.broadcast_to(nact, (1, TMAP_W)).astype(I32)
    tmap_ref[2:3, :] = nxt.astype(I32)
    return jnp.concatenate(rows[TOP_K:2 * TOP_K], axis=0).astype(I32)


N_TOK_TILES = T_ALL // TM


def _route2_kernel(enc_ref, cnt_ref, init_hbm, meta_ref, tmap_ref, inv_hbm,
                   carry_ref, posv0, posv1, poss0, poss1, inv_smem, psem, isem):
    i = pl.program_id(0)
    posv = (posv0, posv1)
    poss = (poss0, poss1)
    per_token = (1 << TOKEN_BITS) + 1
    per_slot = T_ALL << TOKEN_BITS

    @pl.when(i == 0)
    def _():
        carry_ref[...] = jnp.zeros_like(carry_ref)
        cp = pltpu.make_async_copy(init_hbm, inv_smem, isem)
        cp.start()
        cp.wait()

    def positions(s):
        posv[s][...] = _route2_positions(enc_ref, cnt_ref, meta_ref, tmap_ref, carry_ref)
        pltpu.make_async_copy(posv[s], poss[s], psem.at[s]).start()

    def invert(s):
        pltpu.make_async_copy(posv[s], poss[s], psem.at[s]).wait()
        base = (i - 1) * (TM * per_token)
        for t in range(TM):
            for k in range(TOP_K):
                inv_smem[poss[s][k, t]] = base + (t * per_token + k * per_slot)

    @pl.when(i == 0)
    def _():
        positions(0)

    for s in range(2):
        @pl.when(jnp.logical_and(jnp.logical_and(i >= 1, i < N_TOK_TILES), i % 2 == s))
        def _():
            invert(1 - s)
            positions(s)

    @pl.when(i == N_TOK_TILES)
    def _():
        invert((N_TOK_TILES - 1) % 2)
        cp = pltpu.make_async_copy(inv_smem, inv_hbm, isem)
        cp.start()
        cp.wait()


def _route2(enc, cnt, inv_init):
    last = N_TOK_TILES - 1
    hbm = pl.BlockSpec(memory_space=pl.ANY)
    return pl.pallas_call(
        _route2_kernel,
        grid=(N_TOK_TILES + 1,),
        in_specs=[pl.BlockSpec((N_EXPERTS, TM), lambda i: (0, jnp.minimum(i, last))),
                  pl.BlockSpec((N_EXPERTS, 128), lambda i: (0, 0)),
                  hbm],
        out_specs=[pl.BlockSpec((TM, 128), lambda i: (jnp.minimum(i, last), 0)),
                   pl.BlockSpec((8, TMAP_W), lambda i: (0, 0)),
                   hbm],
        out_shape=[jax.ShapeDtypeStruct((T_ALL, 128), F32),
                   jax.ShapeDtypeStruct((8, TMAP_W), I32),
                   jax.ShapeDtypeStruct((N_SORTED,), I32)],
        scratch_shapes=[pltpu.VMEM((N_EXPERTS, 128), F32),
                        pltpu.VMEM((TOP_K, TM), I32), pltpu.VMEM((TOP_K, TM), I32),
                        pltpu.SMEM((TOP_K, TM), I32), pltpu.SMEM((TOP_K, TM), I32),
                        pltpu.SMEM((N_SORTED,), I32),
                        pltpu.SemaphoreType.DMA((2,)), pltpu.SemaphoreType.DMA(())],
        compiler_params=_cparams(("arbitrary",)),
        name="route2",
    )(enc, cnt, inv_init)


TOKEN_BITS = 14
TOKEN_MASK = (1 << TOKEN_BITS) - 1
NBUF = 3
TRASH_BASE = TOP_K * T_ALL
YS_ROWS = TRASH_BASE + NBUF * TILE_E
assert T_ALL <= 1 << TOKEN_BITS and YS_ROWS << TOKEN_BITS < 1 << 31


def _pad_codes():
    r = jnp.arange(N_SORTED, dtype=I32)
    q = r % TILE_E
    out_row = TRASH_BASE + ((r // TILE_E) % NBUF) * TILE_E + q
    return (out_row << TOKEN_BITS) | q


N_CHUNK = 256
GU_PIECES = D_EXPERT // N_CHUNK
DN_PIECES = D_MODEL // N_CHUNK
N_PIECES = GU_PIECES + DN_PIECES

PIECE_WORK = (D_MODEL * 2,) * GU_PIECES + (D_EXPERT,) * DN_PIECES
PIECE_ROWS = tuple(round(TILE_E * sum(PIECE_WORK[:p]) / sum(PIECE_WORK)) for p in range(N_PIECES + 1))


def _moe_kernel(te_ref, na_ref, nxt_ref, inv_ref, h2_hbm, wg_hbm, wu_hbm, wd_hbm, ys_hbm,
                xbuf0, xbuf1, xbuf2, ybuf0, ybuf1, ybuf2, xb, act, wg_f32, wu_f32, wd_f32,
                wg_scr, wu_scr, wd_scr, gsem, ssem, wsem):
    i = pl.program_id(0)
    na = na_ref[0]
    active = i < na
    xbufs = (xbuf0, xbuf1, xbuf2)
    ybufs = (ybuf0, ybuf1, ybuf2)

    def weight_copies(e):
        return (pltpu.make_async_copy(wg_hbm.at[e], wg_f32, wsem),
                pltpu.make_async_copy(wu_hbm.at[e], wu_f32, wsem),
                pltpu.make_async_copy(wd_hbm.at[e], wd_f32, wsem))

    def gather_row(tile, s, r):
        tok = inv_ref[tile * TILE_E + r] & TOKEN_MASK
        pltpu.make_async_copy(h2_hbm.at[pl.ds(tok, 1)], xbufs[s].at[pl.ds(r, 1)], gsem.at[s]).start()

    def scatter_row(dst, s, r):
        pltpu.make_async_copy(ybufs[s].at[pl.ds(r, 1)], ys_hbm.at[pl.ds(dst, 1)], ssem.at[s]).start()

    def wait_gather(s):
        pltpu.make_async_copy(h2_hbm.at[pl.ds(0, TILE_E)], xbufs[s], gsem.at[s]).wait()

    def wait_scatter(s):
        pltpu.make_async_copy(ybufs[s], ys_hbm.at[pl.ds(0, TILE_E)], ssem.at[s]).wait()

    @pl.when(i == 0)
    def _():
        zeros = ybufs[NBUF - 1]
        zeros[...] = jnp.zeros_like(zeros)
        for m in range(NBUF):
            cp = pltpu.make_async_copy(zeros, ys_hbm.at[pl.ds(TRASH_BASE + m * TILE_E, TILE_E)], ssem.at[0])
            cp.start()
            cp.wait()
        for t in range(2):
            def body(r, carry):
                gather_row(t, t, r)
                return carry

            lax.fori_loop(0, TILE_E, body, 0)

        for cp in weight_copies(te_ref[0]):
            cp.start()

    new_expert = jnp.logical_or(i == 0, te_ref[i] != te_ref[jnp.maximum(i - 1, 0)])

    @pl.when(jnp.logical_and(active, new_expert))
    def _():
        for cp in weight_copies(0):
            cp.wait()

    def compute_tile(slot, write_back, round_weights):
        s_next = (slot + 2) % NBUF
        s_prev = (slot - 1) % NBUF
        wait_gather(slot)

        @pl.when(i >= 3)
        def _():
            wait_scatter(slot)

        xb[...] = xbufs[slot][...].astype(BF16)
        nxt = jnp.minimum(i + 2, N_TILES_E - 1)
        codes_next = inv_ref.at[pl.ds(pl.multiple_of(nxt * TILE_E, TILE_E), TILE_E)]
        codes_prev = inv_ref.at[pl.ds(pl.multiple_of(jnp.maximum(i - 1, 0) * TILE_E, TILE_E), TILE_E)]

        def issue(piece):
            for r in range(PIECE_ROWS[piece], PIECE_ROWS[piece + 1]):
                tok = codes_next[r] & TOKEN_MASK
                pltpu.make_async_copy(h2_hbm.at[pl.ds(tok, 1)], xbufs[s_next].at[pl.ds(r, 1)],
                                      gsem.at[s_next]).start()
                if write_back:
                    scatter_row(codes_prev[r] >> TOKEN_BITS, s_prev, r)

        for c in range(GU_PIECES):
            cols = slice(c * N_CHUNK, (c + 1) * N_CHUNK)
            if round_weights:
                wg_scr[:, cols] = wg_f32[:, cols].astype(BF16)
                wu_scr[:, cols] = wu_f32[:, cols].astype(BF16)
            g = jnp.dot(xb[...], wg_scr[:, cols], preferred_element_type=F32)
            u = jnp.dot(xb[...], wu_scr[:, cols], preferred_element_type=F32)
            act[:, cols] = (_silu(g) * u).astype(BF16)
            issue(c)
        for c in range(DN_PIECES):
            cols = slice(c * N_CHUNK, (c + 1) * N_CHUNK)
            if round_weights:
                wd_scr[:, cols] = wd_f32[:, cols].astype(BF16)
            ybufs[slot][:, cols] = jnp.dot(act[...], wd_scr[:, cols], preferred_element_type=F32)
            issue(GU_PIECES + c)

    def drain(slot):
        s_last = (slot - 1) % NBUF

        def body(r, carry):
            scatter_row(inv_ref[(na - 1) * TILE_E + r] >> TOKEN_BITS, s_last, r)
            return carry

        lax.fori_loop(0, TILE_E, body, 0)
        wait_scatter(s_last)

        @pl.when(na >= 2)
        def _():
            wait_scatter((slot - 2) % NBUF)

        @pl.when(na >= 3)
        def _():
            wait_scatter(slot)

        wait_gather(slot)
        wait_gather((slot + 1) % NBUF)

    @pl.when(jnp.logical_and(active, i == 0))
    def _():
        compute_tile(0, write_back=False, round_weights=True)

    for s in range(NBUF):
        later = jnp.logical_and(jnp.logical_and(active, i > 0), i % NBUF == s)

        @pl.when(jnp.logical_and(later, new_expert))
        def _():
            compute_tile(s, write_back=True, round_weights=True)

        @pl.when(jnp.logical_and(later, jnp.logical_not(new_expert)))
        def _():
            compute_tile(s, write_back=True, round_weights=False)

        @pl.when(jnp.logical_and(i == na, i % NBUF == s))
        def _():
            drain(s)

    @pl.when(jnp.logical_and(active, new_expert))
    def _():
        nxt_tile = nxt_ref[i]

        @pl.when(nxt_tile < na)
        def _():
            for cp in weight_copies(te_ref[nxt_tile]):
                cp.start()


def _moe(tile_expert, n_active, next_tile, inv, h2, wg, wu, wd):
    hbm = pl.BlockSpec(memory_space=pl.ANY)
    grid_spec = pltpu.PrefetchScalarGridSpec(
        num_scalar_prefetch=4,
        grid=(N_TILES_E + 1,),
        in_specs=[hbm, hbm, hbm, hbm],
        out_specs=hbm,
        scratch_shapes=[pltpu.VMEM((TILE_E, D_MODEL), F32)] * (2 * NBUF) + [
                        pltpu.VMEM((TILE_E, D_MODEL), BF16),
                        pltpu.VMEM((TILE_E, D_EXPERT), BF16),
                        pltpu.VMEM((D_MODEL, D_EXPERT), F32),
                        pltpu.VMEM((D_MODEL, D_EXPERT), F32),
                        pltpu.VMEM((D_EXPERT, D_MODEL), F32),
                        pltpu.VMEM((D_MODEL, D_EXPERT), BF16),
                        pltpu.VMEM((D_MODEL, D_EXPERT), BF16),
                        pltpu.VMEM((D_EXPERT, D_MODEL), BF16),
                        pltpu.SemaphoreType.DMA((NBUF,)),
                        pltpu.SemaphoreType.DMA((NBUF,)),
                        pltpu.SemaphoreType.DMA(())],
    )
    return pl.pallas_call(
        _moe_kernel,
        grid_spec=grid_spec,
        out_shape=jax.ShapeDtypeStruct((YS_ROWS, D_MODEL), F32),
        compiler_params=_cparams(("arbitrary",)),
        name="moe",
    )(tile_expert, n_active, next_tile, inv, h2, wg, wu, wd)


def _combine_kernel(meta_ref, *refs):
    ys_refs = refs[:TOP_K]
    h2_ref, x1_ref, mod_ref, wsg_hbm, wsu_hbm, wsd_hbm = refs[TOP_K:TOP_K + 6]
    op_ref, os_ref, wsg_scr, wsu_scr, wsd_scr, stage_a, stage_b, wsem = refs[TOP_K + 6:]
    i = pl.program_id(0)

    @pl.when(i == 0)
    def _():
        _load_cast_weight(wsg_hbm, wsg_scr, stage_a, wsem)
        _load_cast_weight(wsu_hbm, wsu_scr, stage_a, wsem)
        _load_cast_weight(wsd_hbm, wsd_scr, stage_b, wsem)

    h = h2_ref[...].astype(BF16)
    sg = jnp.dot(h, wsg_scr[...], preferred_element_type=F32)
    su = jnp.dot(h, wsu_scr[...], preferred_element_type=F32)
    moe = jnp.dot((_silu(sg) * su).astype(BF16), wsd_scr[...], preferred_element_type=F32)
    for k in range(TOP_K):
        moe = moe + meta_ref[:, k:k + 1] * ys_refs[k][...]
    y = x1_ref[...] + mod_ref[0, 5:6, :] * moe

    @pl.when(i < T_P // TM_C)
    def _():
        op_ref[...] = y

    @pl.when(i >= T_P // TM_C)
    def _():
        os_ref[...] = y


def _combine(meta, ys, h2, x1, mods3, wsg, wsu, wsd):
    tiles_per_group = MOD_GROUP // TM_C
    row = lambda w: pl.BlockSpec((TM_C, w), lambda i: (i, 0))
    slot_rows = lambda k: pl.BlockSpec((TM_C, D_MODEL), lambda i: (k * (T_ALL // TM_C) + i, 0))
    osp, oss = _two_group_specs(TM_C, D_MODEL)
    hbm = pl.BlockSpec(memory_space=pl.ANY)
    return pl.pallas_call(
        _combine_kernel,
        grid=(T_ALL // TM_C,),
        in_specs=[row(128)] + [slot_rows(k) for k in range(TOP_K)] + [
                  row(D_MODEL), row(D_MODEL),
                  pl.BlockSpec((1, 6, D_MODEL), lambda i: (i // tiles_per_group, 0, 0)),
                  hbm, hbm, hbm],
        out_specs=[osp, oss],
        out_shape=[jax.ShapeDtypeStruct((T_P, D_MODEL), F32),
                   jax.ShapeDtypeStruct((T_S, D_MODEL), F32)],
        scratch_shapes=[pltpu.VMEM((D_MODEL, D_SHARED), BF16),
                        pltpu.VMEM((D_MODEL, D_SHARED), BF16),
                        pltpu.VMEM((D_SHARED, D_MODEL), BF16),
                        pltpu.VMEM((D_MODEL, D_SHARED), F32),
                        pltpu.VMEM((D_SHARED, D_MODEL), F32),
                        pltpu.SemaphoreType.DMA(())],
        compiler_params=_cparams(("arbitrary",)),
        name="combine",
    )(meta, *([ys] * TOP_K), h2, x1, mods3, wsg, wsu, wsd)


def _rope_tables():
    rows = DEC_SEQ // GRID_W
    row = np.repeat(np.arange(rows, dtype=np.float64), GRID_W)
    col = np.tile(np.arange(GRID_W, dtype=np.float64), rows)
    n_freq = HEAD_DIM // 4
    inv = ROPE_THETA ** (-np.arange(n_freq, dtype=np.float64) / n_freq)
    ang = np.concatenate([row[:, None] * inv, col[:, None] * inv], axis=-1)
    cos, sin = np.cos(ang), np.sin(ang)
    cos2 = np.concatenate([cos, cos], axis=-1)
    sin2 = np.concatenate([-sin, sin], axis=-1)
    cos_tab = np.concatenate([np.ones((TM_IN, HEAD_DIM)), cos2], axis=0).astype(np.float32)
    sin_tab = np.concatenate([np.zeros((TM_IN, HEAD_DIM)), sin2], axis=0).astype(np.float32)
    return jnp.asarray(cos_tab), jnp.asarray(sin_tab)


def kernel(x_prompt, x_sample, cache_k, cache_v, c, c_ctx, w_ada, b_ada, norm_mix_g, norm_ffn_g, w_in, conv_w,
           q_norm_g, k_norm_g, attn_sink, w_conv_out, w_attn_out, w_out, router_w, router_bias, w_exp_gate,
           w_exp_up, w_exp_down, w_sh_gate, w_sh_up, w_sh_down):
    l = 0
    xp = x_prompt.reshape(T_P, D_MODEL)
    xs = x_sample.reshape(T_S, D_MODEL)

    cond = jnp.concatenate([c_ctx[None, :], c, jnp.zeros((8 - N_MOD, D_MODEL), F32)], axis=0)
    mods = _ada(cond.T, w_ada[l], b_ada[l][None, :])
    mods3 = mods[:N_MOD].reshape(N_MOD, 6, D_MODEL)

    h = _prenorm(xp, xs, mods3, norm_mix_g[l][None, :])
    cos_tab, sin_tab = _rope_tables()
    proj, k32, v32 = _inproj(h, w_in[l], q_norm_g[l][None, :], k_norm_g[l][None, :], cos_tab, sin_tab)

    sink = attn_sink[l]
    kc = cache_k[:, l].reshape(DEC_BATCH, PAST_LEN, KV_WIDTH)
    vc = cache_v[:, l].reshape(DEC_BATCH, PAST_LEN, KV_WIDTH)
    attn_p = _attn_p(sink, proj)
    attn_s = _attn_s(sink, proj, kc, vc)

    z1 = _conv(proj, conv_w[l], w_conv_out[l])

    wr_pad = jnp.pad(router_w[l], ((0, 0), (0, 128 - N_EXPERTS)))
    x1, h2, logits = _merge(xp, xs, attn_p, attn_s, z1, proj, mods3, norm_ffn_g[l][None, :], wr_pad,
                            w_attn_out[l], w_out[l])

    enc, cnt = _route1(logits, router_bias[l][:, None])
    meta, tmap, inv = _route2(enc, cnt, _pad_codes())
    tile_expert = tmap[0]
    n_active = tmap[1, :1]

    ys = _moe(tile_expert, n_active, tmap[2], inv, h2, w_exp_gate[l], w_exp_up[l], w_exp_down[l])
    y_p, y_s = _combine(meta, ys, h2, x1, mods3, w_sh_gate[l], w_sh_up[l], w_sh_down[l])

    y_prompt = y_p.reshape(BATCH, SEQ, D_MODEL)
    y_sample = y_s.reshape(DEC_BATCH, DEC_SEQ, D_MODEL)
    new_k = k32.reshape(BATCH, 1, SEQ, N_KV_HEADS, HEAD_DIM)
    new_v = v32.reshape(BATCH, 1, SEQ, N_KV_HEADS, HEAD_DIM)
    return (y_prompt, y_sample, new_k, new_v)
```

```python
import jax
import jax.numpy as jnp
import numpy as np
from jax import lax
from jax.experimental import pallas as pl
from jax.experimental.pallas import tpu as pltpu

F32 = jnp.float32
BF16 = jnp.bfloat16
I32 = jnp.int32

D_MODEL = 2048
BATCH = 16
SEQ = 256
DEC_BATCH = 2
DEC_SEQ = 4096
PAST_LEN = 512
GRID_W = 64
N_HEADS = 16
N_KV_HEADS = 4
HEAD_DIM = 128
GROUP = N_HEADS // N_KV_HEADS
ATTN_WIDTH = N_HEADS * HEAD_DIM
KV_WIDTH = N_KV_HEADS * HEAD_DIM
WINDOW = 128
CONV_WIDTH = D_MODEL
N_EXPERTS = 64
TOP_K = 8
N_EXPERT_GROUPS = 8
GROUP_SIZE = N_EXPERTS // N_EXPERT_GROUPS
TOPK_GROUPS = 4
D_EXPERT = 512
D_SHARED = 512
ROUTED_SCALE = 2.5
ROPE_THETA = 10000.0
EPS = 1e-6
NEG_INF = -1e30
ATTN_SCALE = HEAD_DIM ** -0.5
LOG2E = 1.4426950408889634

T_P = BATCH * SEQ
T_S = DEC_BATCH * DEC_SEQ
T_ALL = T_P + T_S
MOD_GROUP = 4096
N_MOD = 1 + DEC_BATCH
assert T_P == MOD_GROUP and DEC_SEQ == MOD_GROUP

COL_CB, COL_CC, COL_CX, COL_Q, COL_GC, COL_GA = (i * D_MODEL for i in range(6))
COL_K = 6 * D_MODEL
COL_V = COL_K + KV_WIDTH
IN_WIDTH = COL_V + KV_WIDTH
W_COL_KV = 3 * CONV_WIDTH + ATTN_WIDTH

TM_IN = 1024
TN_IN = 1024
TM = 256
TM_PRE = 512
TM_C = 128
TILE_E = 512
N_SORTED = T_ALL * TOP_K + N_EXPERTS * TILE_E
N_TILES_E = N_SORTED // TILE_E
TMAP_W = 512
CAST_ROWS = 512
VMEM_LIMIT = 56 * 1024 * 1024


def _cparams(sem):
    return pltpu.CompilerParams(dimension_semantics=sem, vmem_limit_bytes=VMEM_LIMIT)


def _silu(x):
    return x * jax.nn.sigmoid(x)


def _load_cast_weight(w_hbm, w_scr, stage, sem):
    rows = stage.shape[0]
    for c in range(w_hbm.shape[0] // rows):
        cp = pltpu.make_async_copy(w_hbm.at[pl.ds(c * rows, rows)], stage, sem)
        cp.start()
        cp.wait()
        w_scr[pl.ds(c * rows, rows), :] = stage[...].astype(BF16)


ADA_TN = 1024
ADA_CHUNK = 256


def _ada_kernel(ct_ref, w_ref, b_ref, o_ref):
    tn = w_ref.shape[1]

    def body(c, accs):
        k0 = pl.multiple_of(c * ADA_CHUNK, ADA_CHUNK)
        wch = w_ref[pl.ds(k0, ADA_CHUNK), :]
        sch = _silu(ct_ref[pl.ds(k0, ADA_CHUNK), :])
        out = []
        for r in range(N_MOD):
            p = wch * sch[:, r:r + 1]
            out.append(accs[r] + p.reshape(ADA_CHUNK // 8, 8, tn).sum(axis=0))
        return tuple(out)

    accs = lax.fori_loop(0, D_MODEL // ADA_CHUNK, body,
                         tuple(jnp.zeros((8, tn), F32) for _ in range(N_MOD)))
    o_ref[...] = jnp.zeros_like(o_ref)
    for r in range(N_MOD):
        o_ref[r:r + 1, :] = jnp.sum(accs[r], axis=0, keepdims=True) + b_ref[...]


def _ada(cond_t, w_ada, b_ada):
    n = w_ada.shape[1]
    return pl.pallas_call(
        _ada_kernel,
        grid=(n // ADA_TN,),
        in_specs=[pl.BlockSpec((D_MODEL, 8), lambda j: (0, 0)),
                  pl.BlockSpec((D_MODEL, ADA_TN), lambda j: (0, j)),
                  pl.BlockSpec((1, ADA_TN), lambda j: (0, j))],
        out_specs=pl.BlockSpec((8, ADA_TN), lambda j: (0, j)),
        out_shape=jax.ShapeDtypeStruct((8, n), F32),
        compiler_params=_cparams(("arbitrary",)),
        name="ada",
    )(cond_t, w_ada, b_ada)


def _rms_mod(x, g, shift, scale):
    ms = jnp.mean(x * x, axis=-1, keepdims=True)
    y = x * lax.rsqrt(ms + EPS) * g
    return y * (1.0 + scale) + shift


def _two_group_specs(tile, width):
    n_ctx = T_P // tile
    return (pl.BlockSpec((tile, width), lambda i: (jnp.minimum(i, n_ctx - 1), 0)),
            pl.BlockSpec((tile, width), lambda i: (jnp.maximum(i - n_ctx, 0), 0)))


def _prenorm_kernel(xp_ref, xs_ref, mod_ref, g_ref, h_ref):
    i = pl.program_id(0)
    shift = mod_ref[0, 0:1, :]
    scale = mod_ref[0, 1:2, :]
    g = g_ref[...]

    @pl.when(i < T_P // TM_PRE)
    def _():
        h_ref[...] = _rms_mod(xp_ref[...], g, shift, scale).astype(BF16)

    @pl.when(i >= T_P // TM_PRE)
    def _():
        h_ref[...] = _rms_mod(xs_ref[...], g, shift, scale).astype(BF16)


def _prenorm(xp, xs, mods3, g_mix):
    tiles_per_group = MOD_GROUP // TM_PRE
    sp, ss = _two_group_specs(TM_PRE, D_MODEL)
    return pl.pallas_call(
        _prenorm_kernel,
        grid=(T_ALL // TM_PRE,),
        in_specs=[sp, ss,
                  pl.BlockSpec((1, 6, D_MODEL), lambda i: (i // tiles_per_group, 0, 0)),
                  pl.BlockSpec((1, D_MODEL), lambda i: (0, 0))],
        out_specs=pl.BlockSpec((TM_PRE, D_MODEL), lambda i: (i, 0)),
        out_shape=jax.ShapeDtypeStruct((T_ALL, D_MODEL), BF16),
        compiler_params=_cparams(("arbitrary",)),
        name="prenorm",
    )(xp, xs, mods3, g_mix)


def _head_norm_rope(a, g, cos2, sin2, scale):
    ms = jnp.mean(a * a, axis=-1, keepdims=True)
    a = a * lax.rsqrt(ms + EPS) * g
    a = a * cos2 + pltpu.roll(a, HEAD_DIM // 2, 1) * sin2
    return a * scale


N_J = IN_WIDTH // TN_IN
J_Q0 = COL_Q // TN_IN
J_Q1 = J_Q0 + ATTN_WIDTH // TN_IN
J_KV = COL_K // TN_IN
assert J_KV == N_J - 1
N_CTX_IN = T_P // TM_IN


def _inproj_kernel(h_ref, w_ref, qg_ref, kg_ref, cos_ref, sin_ref, o_ref, k_ref, v_ref, w_scr):
    j = pl.program_id(0)
    i = pl.program_id(1)

    @pl.when(i == 0)
    def _():
        for c in range(D_MODEL // CAST_ROWS):
            rows = pl.ds(c * CAST_ROWS, CAST_ROWS)
            w_scr[rows, :] = w_ref[rows, :].astype(BF16)

    acc = jnp.dot(h_ref[...], w_scr[...], preferred_element_type=F32)
    is_q = jnp.logical_and(j >= J_Q0, j < J_Q1)
    is_kv = j == J_KV

    @pl.when(jnp.logical_not(jnp.logical_or(is_q, is_kv)))
    def _():
        o_ref[...] = acc.astype(BF16)

    @pl.when(is_q)
    def _():
        cos2 = cos_ref[...]
        sin2 = sin_ref[...]
        g = qg_ref[...]
        for h in range(TN_IN // HEAD_DIM):
            cols = slice(h * HEAD_DIM, (h + 1) * HEAD_DIM)
            o_ref[:, cols] = _head_norm_rope(acc[:, cols], g, cos2, sin2, ATTN_SCALE * LOG2E).astype(BF16)

    @pl.when(is_kv)
    def _():
        cos2 = cos_ref[...]
        sin2 = sin_ref[...]
        g = kg_ref[...]
        is_ctx = i < N_CTX_IN
        for h in range(N_KV_HEADS):
            cols = slice(h * HEAD_DIM, (h + 1) * HEAD_DIM)
            kh = _head_norm_rope(acc[:, cols], g, cos2, sin2, 1.0)
            o_ref[:, cols] = kh.astype(BF16)

            @pl.when(is_ctx)
            def _():
                k_ref[:, cols] = kh

        v = acc[:, KV_WIDTH:]
        o_ref[:, KV_WIDTH:] = v.astype(BF16)

        @pl.when(is_ctx)
        def _():
            v_ref[...] = v


def _inproj(h, w_in, qg, kg, cos_tab, sin_tab):
    pos_tiles = DEC_SEQ // TM_IN

    def tab_map(j, i):
        return (jnp.where(i < N_CTX_IN, 0, 1 + i % pos_tiles), 0)

    def w_map(j, i):
        return (0, jnp.where(j < W_COL_KV // TN_IN, j, jnp.where(j == J_KV, W_COL_KV // TN_IN, j + 1)))

    def kv_map(j, i):
        return (jnp.where(j == J_KV, jnp.minimum(i, N_CTX_IN - 1), 0), 0)

    return pl.pallas_call(
        _inproj_kernel,
        grid=(N_J, T_ALL // TM_IN),
        in_specs=[pl.BlockSpec((TM_IN, D_MODEL), lambda j, i: (i, 0)),
                  pl.BlockSpec((D_MODEL, TN_IN), w_map, pipeline_mode=pl.Buffered(1)),
                  pl.BlockSpec((1, HEAD_DIM), lambda j, i: (0, 0)),
                  pl.BlockSpec((1, HEAD_DIM), lambda j, i: (0, 0)),
                  pl.BlockSpec((TM_IN, HEAD_DIM), tab_map),
                  pl.BlockSpec((TM_IN, HEAD_DIM), tab_map)],
        out_specs=[pl.BlockSpec((TM_IN, TN_IN), lambda j, i: (i, j)),
                   pl.BlockSpec((TM_IN, KV_WIDTH), kv_map),
                   pl.BlockSpec((TM_IN, KV_WIDTH), kv_map)],
        out_shape=[jax.ShapeDtypeStruct((T_ALL, IN_WIDTH), BF16),
                   jax.ShapeDtypeStruct((T_P, KV_WIDTH), F32),
                   jax.ShapeDtypeStruct((T_P, KV_WIDTH), F32)],
        scratch_shapes=[pltpu.VMEM((D_MODEL, TN_IN), BF16)],
        compiler_params=_cparams(("arbitrary", "arbitrary")),
        name="inproj",
    )(h, w_in, qg, kg, cos_tab, sin_tab)


def _dot_nt(a, b):
    return lax.dot_general(a, b, (((1,), (1,)), ((), ())), preferred_element_type=F32)


def _attn_p_kernel(sink_ref, q_ref, k_ref, v_ref, o_ref):
    for kh in range(N_KV_HEADS):
        kcols = slice(kh * HEAD_DIM, (kh + 1) * HEAD_DIM)
        k = k_ref[:, kcols]
        v = v_ref[:, kcols]
        for gi in range(GROUP):
            h = kh * GROUP + gi
            cols = slice(h * HEAD_DIM, (h + 1) * HEAD_DIM)
            s = _dot_nt(q_ref[:, cols], k)
            sk = sink_ref[h] * LOG2E
            m = jnp.maximum(jnp.max(s, axis=-1, keepdims=True), sk)
            p = jnp.exp2(s - m)
            den = jnp.sum(p, axis=-1, keepdims=True) + jnp.exp2(sk - m)
            o = jnp.dot(p.astype(BF16), v, preferred_element_type=F32) / den
            o_ref[:, cols] = o.astype(BF16)


def _attn_p(sink, proj):
    return pl.pallas_call(
        _attn_p_kernel,
        grid=(BATCH,),
        in_specs=[pl.BlockSpec(memory_space=pltpu.SMEM),
                  pl.BlockSpec((SEQ, ATTN_WIDTH), lambda b: (b, COL_Q // ATTN_WIDTH)),
                  pl.BlockSpec((SEQ, KV_WIDTH), lambda b: (b, COL_K // KV_WIDTH)),
                  pl.BlockSpec((SEQ, KV_WIDTH), lambda b: (b, COL_V // KV_WIDTH))],
        out_specs=pl.BlockSpec((SEQ, ATTN_WIDTH), lambda b: (b, 0)),
        out_shape=jax.ShapeDtypeStruct((T_P, ATTN_WIDTH), BF16),
        compiler_params=_cparams(("arbitrary",)),
        name="attn_p",
    )(sink, proj, proj, proj)


QB = 256
BAND = QB + 2 * WINDOW


def _attn_s_kernel(sink_ref, q_ref, k_ref, v_ref, kc_ref, vc_ref, o_ref):
    i = pl.program_id(1)
    start = jnp.clip(i * QB - WINDOW, 0, DEC_SEQ - BAND)
    start = pl.multiple_of(start, WINDOW)
    qpos = i * QB + lax.broadcasted_iota(I32, (QB, BAND), 0)
    kpos = start + lax.broadcasted_iota(I32, (QB, BAND), 1)
    valid = jnp.abs(qpos - kpos) <= WINDOW
    valid_g = jnp.concatenate([valid] * GROUP, axis=0)
    for kh in range(N_KV_HEADS):
        kcols = slice(kh * HEAD_DIM, (kh + 1) * HEAD_DIM)
        kb = k_ref[pl.ds(start, BAND), kcols]
        kc = kc_ref[0, :, kcols].astype(BF16)
        ones = jnp.ones((BAND, HEAD_DIM), BF16)
        vb1 = jnp.concatenate([v_ref[pl.ds(start, BAND), kcols], ones], axis=1)
        vc1 = jnp.concatenate([vc_ref[0, :, kcols].astype(BF16), ones[:PAST_LEN]], axis=1)
        heads = [kh * GROUP + gi for gi in range(GROUP)]
        q = jnp.concatenate([q_ref[:, h * HEAD_DIM:(h + 1) * HEAD_DIM] for h in heads], axis=0)
        sk = jnp.concatenate([jnp.full((QB, 1), sink_ref[h] * LOG2E, F32) for h in heads], axis=0)
        s_loc = jnp.where(valid_g, _dot_nt(q, kb), NEG_INF)
        s_ctx = _dot_nt(q, kc)
        m = jnp.maximum(jnp.maximum(jnp.max(s_loc, axis=-1, keepdims=True),
                                    jnp.max(s_ctx, axis=-1, keepdims=True)), sk)
        p_loc = jnp.exp2(s_loc - m).astype(BF16)
        p_ctx = jnp.exp2(s_ctx - m).astype(BF16)
        o1 = (jnp.dot(p_loc, vb1, preferred_element_type=F32)
              + jnp.dot(p_ctx, vc1, preferred_element_type=F32))
        den = o1[:, HEAD_DIM:HEAD_DIM + 1] + jnp.exp2(sk - m)
        o = o1[:, :HEAD_DIM] / den
        for gi, h in enumerate(heads):
            o_ref[:, h * HEAD_DIM:(h + 1) * HEAD_DIM] = o[gi * QB:(gi + 1) * QB, :].astype(BF16)


def _attn_s(sink, proj, kc, vc):
    nq = DEC_SEQ // QB
    first = T_P // QB
    seq_blk = T_P // DEC_SEQ
    return pl.pallas_call(
        _attn_s_kernel,
        grid=(DEC_BATCH, nq),
        in_specs=[pl.BlockSpec(memory_space=pltpu.SMEM),
                  pl.BlockSpec((QB, ATTN_WIDTH), lambda b, i: (first + b * nq + i, COL_Q // ATTN_WIDTH)),
                  pl.BlockSpec((DEC_SEQ, KV_WIDTH), lambda b, i: (seq_blk + b, COL_K // KV_WIDTH)),
                  pl.BlockSpec((DEC_SEQ, KV_WIDTH), lambda b, i: (seq_blk + b, COL_V // KV_WIDTH)),
                  pl.BlockSpec((1, PAST_LEN, KV_WIDTH), lambda b, i: (b, 0, 0)),
                  pl.BlockSpec((1, PAST_LEN, KV_WIDTH), lambda b, i: (b, 0, 0))],
        out_specs=pl.BlockSpec((QB, ATTN_WIDTH), lambda b, i: (b * nq + i, 0)),
        out_shape=jax.ShapeDtypeStruct((T_S, ATTN_WIDTH), BF16),
        compiler_params=_cparams(("arbitrary", "arbitrary")),
        name="attn_s",
    )(sink, proj, proj, proj, kc, vc)


HALO = 16


def _conv_kernel(cb_ref, cc_ref, cx_ref, ccp_ref, cxp_ref, ccn_ref, cxn_ref, gc_ref, cw_ref, w_hbm, o_ref,
                 w_scr, stage, sem):
    i = pl.program_id(0)

    @pl.when(i == 0)
    def _():
        _load_cast_weight(w_hbm, w_scr, stage, sem)

    n_ctx_tiles = T_P // TM
    per_seq = DEC_SEQ // TM
    is_first = jnp.logical_or(i < n_ctx_tiles, (i - n_ctx_tiles) % per_seq == 0)
    is_last = jnp.logical_or(i < n_ctx_tiles, (i - n_ctx_tiles) % per_seq == per_seq - 1)
    p = cc_ref[...].astype(F32) * cx_ref[...].astype(F32)
    prev_row = ccp_ref[HALO - 1:HALO, :].astype(F32) * cxp_ref[HALO - 1:HALO, :].astype(F32)
    next_row = ccn_ref[0:1, :].astype(F32) * cxn_ref[0:1, :].astype(F32)
    prev_row = jnp.where(is_first, 0.0, prev_row)
    next_row = jnp.where(is_last, 0.0, next_row)
    rows = lax.broadcasted_iota(I32, (TM, 1), 0)
    p_prev = jnp.where(rows == 0, prev_row, pltpu.roll(p, 1, 0))
    p_next = jnp.where(rows == TM - 1, next_row, pltpu.roll(p, TM - 1, 0))
    conv = p_prev * cw_ref[0:1, :] + p * cw_ref[1:2, :] + p_next * cw_ref[2:3, :]
    u = cb_ref[...].astype(F32) * conv
    y = jnp.dot(u.astype(BF16), w_scr[...], preferred_element_type=F32)
    o_ref[...] = (jax.nn.sigmoid(gc_ref[...].astype(F32)) * y).astype(BF16)


def _conv(proj, conv_w, w_conv_out):
    hb = TM // HALO
    last_hb = T_ALL // HALO - 1
    wide = lambda c: pl.BlockSpec((TM, D_MODEL), lambda i: (i, c // D_MODEL))
    prev = lambda c: pl.BlockSpec((HALO, D_MODEL), lambda i: (jnp.maximum(i * hb - 1, 0), c // D_MODEL))
    nxt = lambda c: pl.BlockSpec((HALO, D_MODEL), lambda i: (jnp.minimum((i + 1) * hb, last_hb), c // D_MODEL))
    return pl.pallas_call(
        _conv_kernel,
        grid=(T_ALL // TM,),
        in_specs=[wide(COL_CB), wide(COL_CC), wide(COL_CX),
                  prev(COL_CC), prev(COL_CX), nxt(COL_CC), nxt(COL_CX),
                  wide(COL_GC),
                  pl.BlockSpec((3, CONV_WIDTH), lambda i: (0, 0)),
                  pl.BlockSpec(memory_space=pl.ANY)],
        out_specs=pl.BlockSpec((TM, D_MODEL), lambda i: (i, 0)),
        out_shape=jax.ShapeDtypeStruct((T_ALL, D_MODEL), BF16),
        scratch_shapes=[pltpu.VMEM((CONV_WIDTH, D_MODEL), BF16),
                        pltpu.VMEM((CAST_ROWS, D_MODEL), F32),
                        pltpu.SemaphoreType.DMA(())],
        compiler_params=_cparams(("arbitrary",)),
        name="conv",
    )(proj, proj, proj, proj, proj, proj, proj, proj, conv_w, w_conv_out)


def _merge_kernel(xp_ref, xs_ref, attn_p_ref, attn_s_ref, z1_ref, ga_ref, mod_ref, g2_ref, wr_ref,
                  wao_hbm, wo_hbm, x1_ref, h2_ref, lg_ref, wao_scr, wo_scr, stage, sem):
    i = pl.program_id(0)

    @pl.when(i == 0)
    def _():
        _load_cast_weight(wao_hbm, wao_scr, stage, sem)
        _load_cast_weight(wo_hbm, wo_scr, stage, sem)

    is_ctx = i < T_P // TM
    attn = jnp.where(is_ctx, attn_p_ref[...], attn_s_ref[...])
    x = jnp.where(is_ctx, xp_ref[...], xs_ref[...])
    ya = jnp.dot(attn, wao_scr[...], preferred_element_type=F32)
    z = z1_ref[...].astype(F32) + jax.nn.sigmoid(ga_ref[...].astype(F32)) * ya
    mix = jnp.dot(z.astype(BF16), wo_scr[...], preferred_element_type=F32)
    x1 = x + mod_ref[0, 2:3, :] * mix
    x1_ref[...] = x1
    h = _rms_mod(x1, g2_ref[...], mod_ref[0, 3:4, :], mod_ref[0, 4:5, :])
    h2_ref[...] = h
    h_hi = h.astype(BF16)
    h_lo = (h - h_hi.astype(F32)).astype(BF16)
    wr = wr_ref[...]
    wr_hi32 = wr.astype(BF16).astype(F32)
    wr_lo32 = (wr - wr_hi32).astype(BF16).astype(F32)
    w_both = (wr_hi32 + pltpu.roll(wr_lo32, N_EXPERTS, 1)).astype(BF16)
    both = jnp.dot(h_hi, w_both, preferred_element_type=F32)
    lg_ref[...] = (both + pltpu.roll(both, N_EXPERTS, 1)
                   + jnp.dot(h_lo, wr_hi32.astype(BF16), preferred_element_type=F32))


def _merge(xp, xs, attn_p, attn_s, z1, proj, mods3, g_ffn, wr_pad, w_attn_out, w_out):
    tiles_per_group = MOD_GROUP // TM
    const = lambda shape: pl.BlockSpec(shape, lambda i: (0, 0))
    row = lambda w: pl.BlockSpec((TM, w), lambda i: (i, 0))
    xsp, xss = _two_group_specs(TM, D_MODEL)
    asp, ass = _two_group_specs(TM, ATTN_WIDTH)
    return pl.pallas_call(
        _merge_kernel,
        grid=(T_ALL // TM,),
        in_specs=[xsp, xss, asp, ass, row(D_MODEL),
                  pl.BlockSpec((TM, D_MODEL), lambda i: (i, COL_GA // D_MODEL)),
                  pl.BlockSpec((1, 6, D_MODEL), lambda i: (i // tiles_per_group, 0, 0)),
                  const((1, D_MODEL)), const((D_MODEL, 128)),
                  pl.BlockSpec(memory_space=pl.ANY), pl.BlockSpec(memory_space=pl.ANY)],
        out_specs=[row(D_MODEL), row(D_MODEL), row(128)],
        out_shape=[jax.ShapeDtypeStruct((T_ALL, D_MODEL), F32),
                   jax.ShapeDtypeStruct((T_ALL, D_MODEL), F32),
                   jax.ShapeDtypeStruct((T_ALL, 128), F32)],
        scratch_shapes=[pltpu.VMEM((ATTN_WIDTH, D_MODEL), BF16),
                        pltpu.VMEM((D_MODEL, D_MODEL), BF16),
                        pltpu.VMEM((CAST_ROWS, D_MODEL), F32),
                        pltpu.SemaphoreType.DMA(())],
        compiler_params=_cparams(("arbitrary",)),
        name="merge",
    )(xp, xs, attn_p, attn_s, z1, proj, mods3, g_ffn, wr_pad, w_attn_out, w_out)


def _route1_kernel(lg_ref, bias_ref, enc_ref, cnt_ref):
    i = pl.program_id(0)
    lt = lg_ref[...].T[:N_EXPERTS, :]
    scores = jax.nn.sigmoid(lt)
    biased = scores + bias_ref[...]
    b3 = biased.reshape(N_EXPERT_GROUPS, GROUP_SIZE, TM)
    mi = lax.broadcasted_iota(I32, b3.shape, 1)
    m1 = jnp.max(b3, axis=1, keepdims=True)
    idx1 = jnp.min(jnp.where(b3 == m1, mi, GROUP_SIZE), axis=1, keepdims=True)
    m2 = jnp.max(jnp.where(mi == idx1, -jnp.inf, b3), axis=1, keepdims=True)
    gs = (m1 + m2).reshape(N_EXPERT_GROUPS, TM)
    gidx = lax.broadcasted_iota(I32, gs.shape, 0)
    grank = jnp.zeros(gs.shape, I32)
    for j in range(N_EXPERT_GROUPS):
        gj = gs[j:j + 1, :]
        beats = jnp.logical_or(gj > gs, jnp.logical_and(gj == gs, j < gidx))
        grank = grank + beats.astype(I32)
    gsel = grank < TOPK_GROUPS
    emask = jnp.broadcast_to(gsel[:, None, :], b3.shape).reshape(N_EXPERTS, TM)
    masked = jnp.where(emask, biased, NEG_INF)
    eidx = lax.broadcasted_iota(I32, masked.shape, 0)
    erank = jnp.zeros(masked.shape, I32)
    for j in range(N_EXPERTS):
        vj = masked[j:j + 1, :]
        beats = jnp.logical_or(vj > masked, jnp.logical_and(vj == masked, j < eidx))
        erank = erank + beats.astype(I32)
    sel = erank < TOP_K
    wsel = jnp.where(sel, scores, 0.0)
    den = jnp.sum(wsel, axis=0, keepdims=True)
    wts = wsel / den * ROUTED_SCALE
    enc_ref[...] = jnp.where(sel, wts, -1.0)

    @pl.when(i == 0)
    def _():
        cnt_ref[...] = jnp.zeros_like(cnt_ref)

    cnt = jnp.sum(sel.astype(F32), axis=1, keepdims=True)
    cnt_ref[...] += jnp.broadcast_to(cnt, cnt_ref.shape)


def _route1(logits, bias_col):
    return pl.pallas_call(
        _route1_kernel,
        grid=(T_ALL // TM,),
        in_specs=[pl.BlockSpec((TM, 128), lambda i: (i, 0)),
                  pl.BlockSpec((N_EXPERTS, 1), lambda i: (0, 0))],
        out_specs=[pl.BlockSpec((N_EXPERTS, TM), lambda i: (0, i)),
                   pl.BlockSpec((N_EXPERTS, 128), lambda i: (0, 0))],
        out_shape=[jax.ShapeDtypeStruct((N_EXPERTS, T_ALL), F32),
                   jax.ShapeDtypeStruct((N_EXPERTS, 128), F32)],
        compiler_params=_cparams(("arbitrary",)),
        name="route1",
    )(logits, bias_col)


def _route2_positions(enc_ref, cnt_ref, meta_ref, tmap_ref, carry_ref):
    enc = enc_ref[...]
    sel = enc >= 0.0
    wts = jnp.maximum(enc, 0.0)
    sel_b = sel.astype(BF16)
    ntile = jnp.ceil(cnt_ref[...] * (1.0 / TILE_E))
    er = lax.broadcasted_iota(I32, (N_EXPERTS, N_EXPERTS), 0)
    ec = lax.broadcasted_iota(I32, (N_EXPERTS, N_EXPERTS), 1)
    lower = (ec < er).astype(BF16)
    off_t = jnp.dot(lower, ntile.astype(BF16), preferred_element_type=F32)
    tr = lax.broadcasted_iota(I32, (TM, TM), 0)
    tc = lax.broadcasted_iota(I32, (TM, TM), 1)
    upper = (tr < tc).astype(BF16)
    rank = jnp.dot(sel_b, upper, preferred_element_type=F32) + carry_ref[:, 0:1]
    carry_ref[...] += jnp.broadcast_to(jnp.sum(sel.astype(F32), axis=1, keepdims=True), carry_ref.shape)
    pos = off_t[:, 0:1] * float(TILE_E) + rank
    slot = jnp.dot(lower, sel_b, preferred_element_type=F32)
    rows = []
    for k in range(TOP_K):
        mk = jnp.logical_and(sel, slot == float(k))
        rows.append(jnp.sum(jnp.where(mk, wts, 0.0), axis=0, keepdims=True))
    for k in range(TOP_K):
        mk = jnp.logical_and(sel, slot == float(k))
        rows.append(jnp.sum(jnp.where(mk, pos, 0.0), axis=0, keepdims=True))
    rows.append(jnp.zeros((128 - 2 * TOP_K, TM), F32))
    meta_ref[...] = jnp.concatenate(rows, axis=0).T
    nt = ntile[:, 0:1]
    end_t = off_t[:, 0:1] + nt
    lane = lax.broadcasted_iota(I32, (N_EXPERTS, TMAP_W), 1)
    te = jnp.sum((end_t <= lane.astype(F32)).astype(F32), axis=0, keepdims=True)
    te = jnp.minimum(te, float(N_EXPERTS - 1))
    nact = jnp.sum(nt, axis=0, keepdims=True)
    erow = lax.broadcasted_iota(I32, (N_EXPERTS, TMAP_W), 0).astype(F32)
    nxt = jnp.sum(jnp.where(erow == te, end_t, 0.0), axis=0, keepdims=True)
    tmap_ref[...] = jnp.zeros_like(tmap_ref)
    tmap_ref[0:1, :] = te.astype(I32)
    tmap_ref[1:2, :] = jnp.broadcast_to(nact, (1, TMAP_W)).astype(I32)
    tmap_ref[2:3, :] = nxt.astype(I32)
    return jnp.concatenate(rows[TOP_K:2 * TOP_K], axis=0).astype(I32)


N_TOK_TILES = T_ALL // TM


def _route2_kernel(enc_ref, cnt_ref, init_hbm, meta_ref, tmap_ref, inv_hbm,
                   carry_ref, posv0, posv1, poss0, poss1, inv_smem, psem, isem):
    i = pl.program_id(0)
    posv = (posv0, posv1)
    poss = (poss0, poss1)
    per_token = (1 << TOKEN_BITS) + 1
    per_slot = T_ALL << TOKEN_BITS

    @pl.when(i == 0)
    def _():
        carry_ref[...] = jnp.zeros_like(carry_ref)
        cp = pltpu.make_async_copy(init_hbm, inv_smem, isem)
        cp.start()
        cp.wait()

    def positions(s):
        posv[s][...] = _route2_positions(enc_ref, cnt_ref, meta_ref, tmap_ref, carry_ref)
        pltpu.make_async_copy(posv[s], poss[s], psem.at[s]).start()

    def invert(s):
        pltpu.make_async_copy(posv[s], poss[s], psem.at[s]).wait()
        base = (i - 1) * (TM * per_token)
        for t in range(TM):
            for k in range(TOP_K):
                inv_smem[poss[s][k, t]] = base + (t * per_token + k * per_slot)

    @pl.when(i == 0)
    def _():
        positions(0)

    for s in range(2):
        @pl.when(jnp.logical_and(jnp.logical_and(i >= 1, i < N_TOK_TILES), i % 2 == s))
        def _():
            invert(1 - s)
            positions(s)

    @pl.when(i == N_TOK_TILES)
    def _():
        invert((N_TOK_TILES - 1) % 2)
        cp = pltpu.make_async_copy(inv_smem, inv_hbm, isem)
        cp.start()
        cp.wait()


def _route2(enc, cnt, inv_init):
    last = N_TOK_TILES - 1
    hbm = pl.BlockSpec(memory_space=pl.ANY)
    return pl.pallas_call(
        _route2_kernel,
        grid=(N_TOK_TILES + 1,),
        in_specs=[pl.BlockSpec((N_EXPERTS, TM), lambda i: (0, jnp.minimum(i, last))),
                  pl.BlockSpec((N_EXPERTS, 128), lambda i: (0, 0)),
                  hbm],
        out_specs=[pl.BlockSpec((TM, 128), lambda i: (jnp.minimum(i, last), 0)),
                   pl.BlockSpec((8, TMAP_W), lambda i: (0, 0)),
                   hbm],
        out_shape=[jax.ShapeDtypeStruct((T_ALL, 128), F32),
                   jax.ShapeDtypeStruct((8, TMAP_W), I32),
                   jax.ShapeDtypeStruct((N_SORTED,), I32)],
        scratch_shapes=[pltpu.VMEM((N_EXPERTS, 128), F32),
                        pltpu.VMEM((TOP_K, TM), I32), pltpu.VMEM((TOP_K, TM), I32),
                        pltpu.SMEM((TOP_K, TM), I32), pltpu.SMEM((TOP_K, TM), I32),
                        pltpu.SMEM((N_SORTED,), I32),
                        pltpu.SemaphoreType.DMA((2,)), pltpu.SemaphoreType.DMA(())],
        compiler_params=_cparams(("arbitrary",)),
        name="route2",
    )(enc, cnt, inv_init)


TOKEN_BITS = 14
TOKEN_MASK = (1 << TOKEN_BITS) - 1
NBUF = 3
TRASH_BASE = TOP_K * T_ALL
YS_ROWS = TRASH_BASE + NBUF * TILE_E
assert T_ALL <= 1 << TOKEN_BITS and YS_ROWS << TOKEN_BITS < 1 << 31


def _pad_codes():
    r = jnp.arange(N_SORTED, dtype=I32)
    q = r % TILE_E
    out_row = TRASH_BASE + ((r // TILE_E) % NBUF) * TILE_E + q
    return (out_row << TOKEN_BITS) | q


N_CHUNK = 256
GU_PIECES = D_EXPERT // N_CHUNK
DN_PIECES = D_MODEL // N_CHUNK
N_PIECES = GU_PIECES + DN_PIECES

PIECE_WORK = (D_MODEL * 2,) * GU_PIECES + (D_EXPERT,) * DN_PIECES
PIECE_ROWS = tuple(round(TILE_E * sum(PIECE_WORK[:p]) / sum(PIECE_WORK)) for p in range(N_PIECES + 1))


def _moe_kernel(te_ref, na_ref, nxt_ref, inv_ref, h2_hbm, wg_hbm, wu_hbm, wd_hbm, ys_hbm,
                xbuf0, xbuf1, xbuf2, ybuf0, ybuf1, ybuf2, xb, act, wg_f32, wu_f32, wd_f32,
                wg_scr, wu_scr, wd_scr, gsem, ssem, wsem):
    i = pl.program_id(0)
    na = na_ref[0]
    active = i < na
    xbufs = (xbuf0, xbuf1, xbuf2)
    ybufs = (ybuf0, ybuf1, ybuf2)

    def weight_copies(e):
        return (pltpu.make_async_copy(wg_hbm.at[e], wg_f32, wsem),
                pltpu.make_async_copy(wu_hbm.at[e], wu_f32, wsem),
                pltpu.make_async_copy(wd_hbm.at[e], wd_f32, wsem))

    def gather_row(tile, s, r):
        tok = inv_ref[tile * TILE_E + r] & TOKEN_MASK
        pltpu.make_async_copy(h2_hbm.at[pl.ds(tok, 1)], xbufs[s].at[pl.ds(r, 1)], gsem.at[s]).start()

    def scatter_row(dst, s, r):
        pltpu.make_async_copy(ybufs[s].at[pl.ds(r, 1)], ys_hbm.at[pl.ds(dst, 1)], ssem.at[s]).start()

    def wait_gather(s):
        pltpu.make_async_copy(h2_hbm.at[pl.ds(0, TILE_E)], xbufs[s], gsem.at[s]).wait()

    def wait_scatter(s):
        pltpu.make_async_copy(ybufs[s], ys_hbm.at[pl.ds(0, TILE_E)], ssem.at[s]).wait()

    @pl.when(i == 0)
    def _():
        zeros = ybufs[NBUF - 1]
        zeros[...] = jnp.zeros_like(zeros)
        for m in range(NBUF):
            cp = pltpu.make_async_copy(zeros, ys_hbm.at[pl.ds(TRASH_BASE + m * TILE_E, TILE_E)], ssem.at[0])
            cp.start()
            cp.wait()
        for t in range(2):
            def body(r, carry):
                gather_row(t, t, r)
                return carry

            lax.fori_loop(0, TILE_E, body, 0)

        for cp in weight_copies(te_ref[0]):
            cp.start()

    new_expert = jnp.logical_or(i == 0, te_ref[i] != te_ref[jnp.maximum(i - 1, 0)])

    @pl.when(jnp.logical_and(active, new_expert))
    def _():
        for cp in weight_copies(0):
            cp.wait()

    def compute_tile(slot, write_back, round_weights):
        s_next = (slot + 2) % NBUF
        s_prev = (slot - 1) % NBUF
        wait_gather(slot)

        @pl.when(i >= 3)
        def _():
            wait_scatter(slot)

        xb[...] = xbufs[slot][...].astype(BF16)
        nxt = jnp.minimum(i + 2, N_TILES_E - 1)
        codes_next = inv_ref.at[pl.ds(pl.multiple_of(nxt * TILE_E, TILE_E), TILE_E)]
        codes_prev = inv_ref.at[pl.ds(pl.multiple_of(jnp.maximum(i - 1, 0) * TILE_E, TILE_E), TILE_E)]

        def issue(piece):
            for r in range(PIECE_ROWS[piece], PIECE_ROWS[piece + 1]):
                tok = codes_next[r] & TOKEN_MASK
                pltpu.make_async_copy(h2_hbm.at[pl.ds(tok, 1)], xbufs[s_next].at[pl.ds(r, 1)],
                                      gsem.at[s_next]).start()
                if write_back:
                    scatter_row(codes_prev[r] >> TOKEN_BITS, s_prev, r)

        for c in range(GU_PIECES):
            cols = slice(c * N_CHUNK, (c + 1) * N_CHUNK)
            if round_weights:
                wg_scr[:, cols] = wg_f32[:, cols].astype(BF16)
                wu_scr[:, cols] = wu_f32[:, cols].astype(BF16)
            g = jnp.dot(xb[...], wg_scr[:, cols], preferred_element_type=F32)
            u = jnp.dot(xb[...], wu_scr[:, cols], preferred_element_type=F32)
            act[:, cols] = (_silu(g) * u).astype(BF16)
            if round_weights:
                per = DN_PIECES // GU_PIECES
                for d in range(c * per, (c + 1) * per):
                    dcols = slice(d * N_CHUNK, (d + 1) * N_CHUNK)
                    wd_scr[:, dcols] = wd_f32[:, dcols].astype(BF16)
            issue(c)
        if round_weights:
            nxt_tile = nxt_ref[i]

            @pl.when(nxt_tile < na)
            def _():
                for cp in weight_copies(te_ref[nxt_tile]):
                    cp.start()
        for c in range(DN_PIECES):
            cols = slice(c * N_CHUNK, (c + 1) * N_CHUNK)
            ybufs[slot][:, cols] = jnp.dot(act[...], wd_scr[:, cols], preferred_element_type=F32)
            issue(GU_PIECES + c)

    def drain(slot):
        s_last = (slot - 1) % NBUF

        def body(r, carry):
            scatter_row(inv_ref[(na - 1) * TILE_E + r] >> TOKEN_BITS, s_last, r)
            return carry

        lax.fori_loop(0, TILE_E, body, 0)
        wait_scatter(s_last)

        @pl.when(na >= 2)
        def _():
            wait_scatter((slot - 2) % NBUF)

        @pl.when(na >= 3)
        def _():
            wait_scatter(slot)

        wait_gather(slot)
        wait_gather((slot + 1) % NBUF)

    @pl.when(jnp.logical_and(active, i == 0))
    def _():
        compute_tile(0, write_back=False, round_weights=True)

    for s in range(NBUF):
        later = jnp.logical_and(jnp.logical_and(active, i > 0), i % NBUF == s)

        @pl.when(jnp.logical_and(later, new_expert))
        def _():
            compute_tile(s, write_back=True, round_weights=True)

        @pl.when(jnp.logical_and(later, jnp.logical_not(new_expert)))
        def _():
            compute_tile(s, write_back=True, round_weights=False)

        @pl.when(jnp.logical_and(i == na, i % NBUF == s))
        def _():
            drain(s)


def _moe(tile_expert, n_active, next_tile, inv, h2, wg, wu, wd):
    hbm = pl.BlockSpec(memory_space=pl.ANY)
    grid_spec = pltpu.PrefetchScalarGridSpec(
        num_scalar_prefetch=4,
        grid=(N_TILES_E + 1,),
        in_specs=[hbm, hbm, hbm, hbm],
        out_specs=hbm,
        scratch_shapes=[pltpu.VMEM((TILE_E, D_MODEL), F32)] * (2 * NBUF) + [
                        pltpu.VMEM((TILE_E, D_MODEL), BF16),
                        pltpu.VMEM((TILE_E, D_EXPERT), BF16),
                        pltpu.VMEM((D_MODEL, D_EXPERT), F32),
                        pltpu.VMEM((D_MODEL, D_EXPERT), F32),
                        pltpu.VMEM((D_EXPERT, D_MODEL), F32),
                        pltpu.VMEM((D_MODEL, D_EXPERT), BF16),
                        pltpu.VMEM((D_MODEL, D_EXPERT), BF16),
                        pltpu.VMEM((D_EXPERT, D_MODEL), BF16),
                        pltpu.SemaphoreType.DMA((NBUF,)),
                        pltpu.SemaphoreType.DMA((NBUF,)),
                        pltpu.SemaphoreType.DMA(())],
    )
    return pl.pallas_call(
        _moe_kernel,
        grid_spec=grid_spec,
        out_shape=jax.ShapeDtypeStruct((YS_ROWS, D_MODEL), F32),
        compiler_params=_cparams(("arbitrary",)),
        name="moe",
    )(tile_expert, n_active, next_tile, inv, h2, wg, wu, wd)


def _combine_kernel(meta_ref, *refs):
    ys_refs = refs[:TOP_K]
    h2_ref, x1_ref, mod_ref, wsg_hbm, wsu_hbm, wsd_hbm = refs[TOP_K:TOP_K + 6]
    op_ref, os_ref, wsg_scr, wsu_scr, wsd_scr, stage_a, stage_b, wsem = refs[TOP_K + 6:]
    i = pl.program_id(0)

    @pl.when(i == 0)
    def _():
        _load_cast_weight(wsg_hbm, wsg_scr, stage_a, wsem)
        _load_cast_weight(wsu_hbm, wsu_scr, stage_a, wsem)
        _load_cast_weight(wsd_hbm, wsd_scr, stage_b, wsem)

    h = h2_ref[...].astype(BF16)
    sg = jnp.dot(h, wsg_scr[...], preferred_element_type=F32)
    su = jnp.dot(h, wsu_scr[...], preferred_element_type=F32)
    moe = jnp.dot((_silu(sg) * su).astype(BF16), wsd_scr[...], preferred_element_type=F32)
    for k in range(TOP_K):
        moe = moe + meta_ref[:, k:k + 1] * ys_refs[k][...]
    y = x1_ref[...] + mod_ref[0, 5:6, :] * moe

    @pl.when(i < T_P // TM_C)
    def _():
        op_ref[...] = y

    @pl.when(i >= T_P // TM_C)
    def _():
        os_ref[...] = y


def _combine(meta, ys, h2, x1, mods3, wsg, wsu, wsd):
    tiles_per_group = MOD_GROUP // TM_C
    row = lambda w: pl.BlockSpec((TM_C, w), lambda i: (i, 0))
    slot_rows = lambda k: pl.BlockSpec((TM_C, D_MODEL), lambda i: (k * (T_ALL // TM_C) + i, 0))
    osp, oss = _two_group_specs(TM_C, D_MODEL)
    hbm = pl.BlockSpec(memory_space=pl.ANY)
    return pl.pallas_call(
        _combine_kernel,
        grid=(T_ALL // TM_C,),
        in_specs=[row(128)] + [slot_rows(k) for k in range(TOP_K)] + [
                  row(D_MODEL), row(D_MODEL),
                  pl.BlockSpec((1, 6, D_MODEL), lambda i: (i // tiles_per_group, 0, 0)),
                  hbm, hbm, hbm],
        out_specs=[osp, oss],
        out_shape=[jax.ShapeDtypeStruct((T_P, D_MODEL), F32),
                   jax.ShapeDtypeStruct((T_S, D_MODEL), F32)],
        scratch_shapes=[pltpu.VMEM((D_MODEL, D_SHARED), BF16),
                        pltpu.VMEM((D_MODEL, D_SHARED), BF16),
                        pltpu.VMEM((D_SHARED, D_MODEL), BF16),
                        pltpu.VMEM((D_MODEL, D_SHARED), F32),
                        pltpu.VMEM((D_SHARED, D_MODEL), F32),
                        pltpu.SemaphoreType.DMA(())],
        compiler_params=_cparams(("arbitrary",)),
        name="combine",
    )(meta, *([ys] * TOP_K), h2, x1, mods3, wsg, wsu, wsd)


def _rope_tables():
    rows = DEC_SEQ // GRID_W
    row = np.repeat(np.arange(rows, dtype=np.float64), GRID_W)
    col = np.tile(np.arange(GRID_W, dtype=np.float64), rows)
    n_freq = HEAD_DIM // 4
    inv = ROPE_THETA ** (-np.arange(n_freq, dtype=np.float64) / n_freq)
    ang = np.concatenate([row[:, None] * inv, col[:, None] * inv], axis=-1)
    cos, sin = np.cos(ang), np.sin(ang)
    cos2 = np.concatenate([cos, cos], axis=-1)
    sin2 = np.concatenate([-sin, sin], axis=-1)
    cos_tab = np.concatenate([np.ones((TM_IN, HEAD_DIM)), cos2], axis=0).astype(np.float32)
    sin_tab = np.concatenate([np.zeros((TM_IN, HEAD_DIM)), sin2], axis=0).astype(np.float32)
    return jnp.asarray(cos_tab), jnp.asarray(sin_tab)


def kernel(x_prompt, x_sample, cache_k, cache_v, c, c_ctx, w_ada, b_ada, norm_mix_g, norm_ffn_g, w_in, conv_w,
           q_norm_g, k_norm_g, attn_sink, w_conv_out, w_attn_out, w_out, router_w, router_bias, w_exp_gate,
           w_exp_up, w_exp_down, w_sh_gate, w_sh_up, w_sh_down):
    l = 0
    xp = x_prompt.reshape(T_P, D_MODEL)
    xs = x_sample.reshape(T_S, D_MODEL)

    cond = jnp.concatenate([c_ctx[None, :], c, jnp.zeros((8 - N_MOD, D_MODEL), F32)], axis=0)
    mods = _ada(cond.T, w_ada[l], b_ada[l][None, :])
    mods3 = mods[:N_MOD].reshape(N_MOD, 6, D_MODEL)

    h = _prenorm(xp, xs, mods3, norm_mix_g[l][None, :])
    cos_tab, sin_tab = _rope_tables()
    proj, k32, v32 = _inproj(h, w_in[l], q_norm_g[l][None, :], k_norm_g[l][None, :], cos_tab, sin_tab)

    sink = attn_sink[l]
    kc = cache_k[:, l].reshape(DEC_BATCH, PAST_LEN, KV_WIDTH)
    vc = cache_v[:, l].reshape(DEC_BATCH, PAST_LEN, KV_WIDTH)
    attn_p = _attn_p(sink, proj)
    attn_s = _attn_s(sink, proj, kc, vc)

    z1 = _conv(proj, conv_w[l], w_conv_out[l])

    wr_pad = jnp.pad(router_w[l], ((0, 0), (0, 128 - N_EXPERTS)))
    x1, h2, logits = _merge(xp, xs, attn_p, attn_s, z1, proj, mods3, norm_ffn_g[l][None, :], wr_pad,
                            w_attn_out[l], w_out[l])

    enc, cnt = _route1(logits, router_bias[l][:, None])
    meta, tmap, inv = _route2(enc, cnt, _pad_codes())
    tile_expert = tmap[0]
    n_active = tmap[1, :1]

    ys = _moe(tile_expert, n_active, tmap[2], inv, h2, w_exp_gate[l], w_exp_up[l], w_exp_down[l])
    y_p, y_s = _combine(meta, ys, h2, x1, mods3, w_sh_gate[l], w_sh_up[l], w_sh_down[l])

    y_prompt = y_p.reshape(BATCH, SEQ, D_MODEL)
    y_sample = y_s.reshape(DEC_BATCH, DEC_SEQ, D_MODEL)
    new_k = k32.reshape(BATCH, 1, SEQ, N_KV_HEADS, HEAD_DIM)
    new_v = v32.reshape(BATCH, 1, SEQ, N_KV_HEADS, HEAD_DIM)
    return (y_prompt, y_sample, new_k, new_v)
```

```python
import jax
import jax.numpy as jnp
import numpy as np
from jax import lax
from jax.experimental import pallas as pl
from jax.experimental.pallas import tpu as pltpu

F32 = jnp.float32
BF16 = jnp.bfloat16
I32 = jnp.int32

D_MODEL = 2048
BATCH = 16
SEQ = 256
DEC_BATCH = 2
DEC_SEQ = 4096
PAST_LEN = 512
GRID_W = 64
N_HEADS = 16
N_KV_HEADS = 4
HEAD_DIM = 128
GROUP = N_HEADS // N_KV_HEADS
ATTN_WIDTH = N_HEADS * HEAD_DIM
KV_WIDTH = N_KV_HEADS * HEAD_DIM
WINDOW = 128
CONV_WIDTH = D_MODEL
N_EXPERTS = 64
TOP_K = 8
N_EXPERT_GROUPS = 8
GROUP_SIZE = N_EXPERTS // N_EXPERT_GROUPS
TOPK_GROUPS = 4
D_EXPERT = 512
D_SHARED = 512
ROUTED_SCALE = 2.5
ROPE_THETA = 10000.0
EPS = 1e-6
NEG_INF = -1e30
ATTN_SCALE = HEAD_DIM ** -0.5
LOG2E = 1.4426950408889634

T_P = BATCH * SEQ
T_S = DEC_BATCH * DEC_SEQ
T_ALL = T_P + T_S
MOD_GROUP = 4096
N_MOD = 1 + DEC_BATCH
assert T_P == MOD_GROUP and DEC_SEQ == MOD_GROUP

COL_CB, COL_CC, COL_CX, COL_Q, COL_GC, COL_GA = (i * D_MODEL for i in range(6))
COL_K = 6 * D_MODEL
COL_V = COL_K + KV_WIDTH
IN_WIDTH = COL_V + KV_WIDTH
W_COL_KV = 3 * CONV_WIDTH + ATTN_WIDTH

TM_IN = 1024
TN_IN = 1024
TM = 256
TM_PRE = 512
TM_C = 128
TILE_E = 512
N_SORTED = T_ALL * TOP_K + N_EXPERTS * TILE_E
N_TILES_E = N_SORTED // TILE_E
TMAP_W = 512
CAST_ROWS = 512
VMEM_LIMIT = 56 * 1024 * 1024


def _cparams(sem):
    return pltpu.CompilerParams(dimension_semantics=sem, vmem_limit_bytes=VMEM_LIMIT)


def _silu(x):
    return x * jax.nn.sigmoid(x)


def _load_cast_weight(w_hbm, w_scr, stage, sem):
    rows = stage.shape[0]
    for c in range(w_hbm.shape[0] // rows):
        cp = pltpu.make_async_copy(w_hbm.at[pl.ds(c * rows, rows)], stage, sem)
        cp.start()
        cp.wait()
        w_scr[pl.ds(c * rows, rows), :] = stage[...].astype(BF16)


ADA_TN = 1024
ADA_CHUNK = 256


def _ada_kernel(ct_ref, w_ref, b_ref, o_ref):
    tn = w_ref.shape[1]

    def body(c, accs):
        k0 = pl.multiple_of(c * ADA_CHUNK, ADA_CHUNK)
        wch = w_ref[pl.ds(k0, ADA_CHUNK), :]
        sch = _silu(ct_ref[pl.ds(k0, ADA_CHUNK), :])
        out = []
        for r in range(N_MOD):
            p = wch * sch[:, r:r + 1]
            out.append(accs[r] + p.reshape(ADA_CHUNK // 8, 8, tn).sum(axis=0))
        return tuple(out)

    accs = lax.fori_loop(0, D_MODEL // ADA_CHUNK, body,
                         tuple(jnp.zeros((8, tn), F32) for _ in range(N_MOD)))
    o_ref[...] = jnp.zeros_like(o_ref)
    for r in range(N_MOD):
        o_ref[r:r + 1, :] = jnp.sum(accs[r], axis=0, keepdims=True) + b_ref[...]


def _ada(cond_t, w_ada, b_ada):
    n = w_ada.shape[1]
    return pl.pallas_call(
        _ada_kernel,
        grid=(n // ADA_TN,),
        in_specs=[pl.BlockSpec((D_MODEL, 8), lambda j: (0, 0)),
                  pl.BlockSpec((D_MODEL, ADA_TN), lambda j: (0, j)),
                  pl.BlockSpec((1, ADA_TN), lambda j: (0, j))],
        out_specs=pl.BlockSpec((8, ADA_TN), lambda j: (0, j)),
        out_shape=jax.ShapeDtypeStruct((8, n), F32),
        compiler_params=_cparams(("arbitrary",)),
        name="ada",
    )(cond_t, w_ada, b_ada)


def _rms_mod(x, g, shift, scale):
    ms = jnp.mean(x * x, axis=-1, keepdims=True)
    y = x * lax.rsqrt(ms + EPS) * g
    return y * (1.0 + scale) + shift


def _two_group_specs(tile, width):
    n_ctx = T_P // tile
    return (pl.BlockSpec((tile, width), lambda i: (jnp.minimum(i, n_ctx - 1), 0)),
            pl.BlockSpec((tile, width), lambda i: (jnp.maximum(i - n_ctx, 0), 0)))


def _prenorm_kernel(xp_ref, xs_ref, mod_ref, g_ref, h_ref):
    i = pl.program_id(0)
    shift = mod_ref[0, 0:1, :]
    scale = mod_ref[0, 1:2, :]
    g = g_ref[...]

    @pl.when(i < T_P // TM_PRE)
    def _():
        h_ref[...] = _rms_mod(xp_ref[...], g, shift, scale).astype(BF16)

    @pl.when(i >= T_P // TM_PRE)
    def _():
        h_ref[...] = _rms_mod(xs_ref[...], g, shift, scale).astype(BF16)


def _prenorm(xp, xs, mods3, g_mix):
    tiles_per_group = MOD_GROUP // TM_PRE
    sp, ss = _two_group_specs(TM_PRE, D_MODEL)
    return pl.pallas_call(
        _prenorm_kernel,
        grid=(T_ALL // TM_PRE,),
        in_specs=[sp, ss,
                  pl.BlockSpec((1, 6, D_MODEL), lambda i: (i // tiles_per_group, 0, 0)),
                  pl.BlockSpec((1, D_MODEL), lambda i: (0, 0))],
        out_specs=pl.BlockSpec((TM_PRE, D_MODEL), lambda i: (i, 0)),
        out_shape=jax.ShapeDtypeStruct((T_ALL, D_MODEL), BF16),
        compiler_params=_cparams(("arbitrary",)),
        name="prenorm",
    )(xp, xs, mods3, g_mix)


def _head_norm_rope(a, g, cos2, sin2, scale):
    ms = jnp.mean(a * a, axis=-1, keepdims=True)
    a = a * lax.rsqrt(ms + EPS) * g
    a = a * cos2 + pltpu.roll(a, HEAD_DIM // 2, 1) * sin2
    return a * scale


N_J = IN_WIDTH // TN_IN
J_Q0 = COL_Q // TN_IN
J_Q1 = J_Q0 + ATTN_WIDTH // TN_IN
J_KV = COL_K // TN_IN
assert J_KV == N_J - 1
N_CTX_IN = T_P // TM_IN


def _inproj_kernel(h_ref, w_ref, qg_ref, kg_ref, cos_ref, sin_ref, o_ref, k_ref, v_ref, w_scr):
    j = pl.program_id(0)
    i = pl.program_id(1)

    @pl.when(i == 0)
    def _():
        for c in range(D_MODEL // CAST_ROWS):
            rows = pl.ds(c * CAST_ROWS, CAST_ROWS)
            w_scr[rows, :] = w_ref[rows, :].astype(BF16)

    acc = jnp.dot(h_ref[...], w_scr[...], preferred_element_type=F32)
    is_q = jnp.logical_and(j >= J_Q0, j < J_Q1)
    is_kv = j == J_KV

    @pl.when(jnp.logical_not(jnp.logical_or(is_q, is_kv)))
    def _():
        o_ref[...] = acc.astype(BF16)

    @pl.when(is_q)
    def _():
        cos2 = cos_ref[...]
        sin2 = sin_ref[...]
        g = qg_ref[...]
        for h in range(TN_IN // HEAD_DIM):
            cols = slice(h * HEAD_DIM, (h + 1) * HEAD_DIM)
            o_ref[:, cols] = _head_norm_rope(acc[:, cols], g, cos2, sin2, ATTN_SCALE * LOG2E).astype(BF16)

    @pl.when(is_kv)
    def _():
        cos2 = cos_ref[...]
        sin2 = sin_ref[...]
        g = kg_ref[...]
        is_ctx = i < N_CTX_IN
        for h in range(N_KV_HEADS):
            cols = slice(h * HEAD_DIM, (h + 1) * HEAD_DIM)
            kh = _head_norm_rope(acc[:, cols], g, cos2, sin2, 1.0)
            o_ref[:, cols] = kh.astype(BF16)

            @pl.when(is_ctx)
            def _():
                k_ref[:, cols] = kh

        v = acc[:, KV_WIDTH:]
        o_ref[:, KV_WIDTH:] = v.astype(BF16)

        @pl.when(is_ctx)
        def _():
            v_ref[...] = v


def _inproj(h, w_in, qg, kg, cos_tab, sin_tab):
    pos_tiles = DEC_SEQ // TM_IN

    def tab_map(j, i):
        return (jnp.where(i < N_CTX_IN, 0, 1 + i % pos_tiles), 0)

    def w_map(j, i):
        return (0, jnp.where(j < W_COL_KV // TN_IN, j, jnp.where(j == J_KV, W_COL_KV // TN_IN, j + 1)))

    def kv_map(j, i):
        return (jnp.where(j == J_KV, jnp.minimum(i, N_CTX_IN - 1), 0), 0)

    return pl.pallas_call(
        _inproj_kernel,
        grid=(N_J, T_ALL // TM_IN),
        in_specs=[pl.BlockSpec((TM_IN, D_MODEL), lambda j, i: (i, 0)),
                  pl.BlockSpec((D_MODEL, TN_IN), w_map, pipeline_mode=pl.Buffered(1)),
                  pl.BlockSpec((1, HEAD_DIM), lambda j, i: (0, 0)),
                  pl.BlockSpec((1, HEAD_DIM), lambda j, i: (0, 0)),
                  pl.BlockSpec((TM_IN, HEAD_DIM), tab_map),
                  pl.BlockSpec((TM_IN, HEAD_DIM), tab_map)],
        out_specs=[pl.BlockSpec((TM_IN, TN_IN), lambda j, i: (i, j)),
                   pl.BlockSpec((TM_IN, KV_WIDTH), kv_map),
                   pl.BlockSpec((TM_IN, KV_WIDTH), kv_map)],
        out_shape=[jax.ShapeDtypeStruct((T_ALL, IN_WIDTH), BF16),
                   jax.ShapeDtypeStruct((T_P, KV_WIDTH), F32),
                   jax.ShapeDtypeStruct((T_P, KV_WIDTH), F32)],
        scratch_shapes=[pltpu.VMEM((D_MODEL, TN_IN), BF16)],
        compiler_params=_cparams(("arbitrary", "arbitrary")),
        name="inproj",
    )(h, w_in, qg, kg, cos_tab, sin_tab)


def _dot_nt(a, b):
    return lax.dot_general(a, b, (((1,), (1,)), ((), ())), preferred_element_type=F32)


def _attn_p_kernel(sink_ref, q_ref, k_ref, v_ref, o_ref):
    for kh in range(N_KV_HEADS):
        kcols = slice(kh * HEAD_DIM, (kh + 1) * HEAD_DIM)
        k = k_ref[:, kcols]
        v = v_ref[:, kcols]
        for gi in range(GROUP):
            h = kh * GROUP + gi
            cols = slice(h * HEAD_DIM, (h + 1) * HEAD_DIM)
            s = _dot_nt(q_ref[:, cols], k)
            sk = sink_ref[h] * LOG2E
            m = jnp.maximum(jnp.max(s, axis=-1, keepdims=True), sk)
            p = jnp.exp2(s - m)
            den = jnp.sum(p, axis=-1, keepdims=True) + jnp.exp2(sk - m)
            o = jnp.dot(p.astype(BF16), v, preferred_element_type=F32) / den
            o_ref[:, cols] = o.astype(BF16)


def _attn_p(sink, proj):
    return pl.pallas_call(
        _attn_p_kernel,
        grid=(BATCH,),
        in_specs=[pl.BlockSpec(memory_space=pltpu.SMEM),
                  pl.BlockSpec((SEQ, ATTN_WIDTH), lambda b: (b, COL_Q // ATTN_WIDTH)),
                  pl.BlockSpec((SEQ, KV_WIDTH), lambda b: (b, COL_K // KV_WIDTH)),
                  pl.BlockSpec((SEQ, KV_WIDTH), lambda b: (b, COL_V // KV_WIDTH))],
        out_specs=pl.BlockSpec((SEQ, ATTN_WIDTH), lambda b: (b, 0)),
        out_shape=jax.ShapeDtypeStruct((T_P, ATTN_WIDTH), BF16),
        compiler_params=_cparams(("arbitrary",)),
        name="attn_p",
    )(sink, proj, proj, proj)


QB = 256
BAND = QB + 2 * WINDOW


def _attn_s_kernel(sink_ref, q_ref, k_ref, v_ref, kc_ref, vc_ref, o_ref):
    i = pl.program_id(1)
    start = jnp.clip(i * QB - WINDOW, 0, DEC_SEQ - BAND)
    start = pl.multiple_of(start, WINDOW)
    qpos = i * QB + lax.broadcasted_iota(I32, (QB, BAND), 0)
    kpos = start + lax.broadcasted_iota(I32, (QB, BAND), 1)
    valid = jnp.abs(qpos - kpos) <= WINDOW
    valid_g = jnp.concatenate([valid] * GROUP, axis=0)
    for kh in range(N_KV_HEADS):
        kcols = slice(kh * HEAD_DIM, (kh + 1) * HEAD_DIM)
        kb = k_ref[pl.ds(start, BAND), kcols]
        kc = kc_ref[0, :, kcols].astype(BF16)
        ones = jnp.ones((BAND, HEAD_DIM), BF16)
        vb1 = jnp.concatenate([v_ref[pl.ds(start, BAND), kcols], ones], axis=1)
        vc1 = jnp.concatenate([vc_ref[0, :, kcols].astype(BF16), ones[:PAST_LEN]], axis=1)
        heads = [kh * GROUP + gi for gi in range(GROUP)]
        q = jnp.concatenate([q_ref[:, h * HEAD_DIM:(h + 1) * HEAD_DIM] for h in heads], axis=0)
        sk = jnp.concatenate([jnp.full((QB, 1), sink_ref[h] * LOG2E, F32) for h in heads], axis=0)
        s_loc = jnp.where(valid_g, _dot_nt(q, kb), NEG_INF)
        s_ctx = _dot_nt(q, kc)
        m = jnp.maximum(jnp.maximum(jnp.max(s_loc, axis=-1, keepdims=True),
                                    jnp.max(s_ctx, axis=-1, keepdims=True)), sk)
        p_loc = jnp.exp2(s_loc - m).astype(BF16)
        p_ctx = jnp.exp2(s_ctx - m).astype(BF16)
        o1 = (jnp.dot(p_loc, vb1, preferred_element_type=F32)
              + jnp.dot(p_ctx, vc1, preferred_element_type=F32))
        den = o1[:, HEAD_DIM:HEAD_DIM + 1] + jnp.exp2(sk - m)
        o = o1[:, :HEAD_DIM] / den
        for gi, h in enumerate(heads):
            o_ref[:, h * HEAD_DIM:(h + 1) * HEAD_DIM] = o[gi * QB:(gi + 1) * QB, :].astype(BF16)


def _attn_s(sink, proj, kc, vc):
    nq = DEC_SEQ // QB
    first = T_P // QB
    seq_blk = T_P // DEC_SEQ
    return pl.pallas_call(
        _attn_s_kernel,
        grid=(DEC_BATCH, nq),
        in_specs=[pl.BlockSpec(memory_space=pltpu.SMEM),
                  pl.BlockSpec((QB, ATTN_WIDTH), lambda b, i: (first + b * nq + i, COL_Q // ATTN_WIDTH)),
                  pl.BlockSpec((DEC_SEQ, KV_WIDTH), lambda b, i: (seq_blk + b, COL_K // KV_WIDTH)),
                  pl.BlockSpec((DEC_SEQ, KV_WIDTH), lambda b, i: (seq_blk + b, COL_V // KV_WIDTH)),
                  pl.BlockSpec((1, PAST_LEN, KV_WIDTH), lambda b, i: (b, 0, 0)),
                  pl.BlockSpec((1, PAST_LEN, KV_WIDTH), lambda b, i: (b, 0, 0))],
        out_specs=pl.BlockSpec((QB, ATTN_WIDTH), lambda b, i: (b * nq + i, 0)),
        out_shape=jax.ShapeDtypeStruct((T_S, ATTN_WIDTH), BF16),
        compiler_params=_cparams(("arbitrary", "arbitrary")),
        name="attn_s",
    )(sink, proj, proj, proj, kc, vc)


HALO = 16


def _conv_kernel(cb_ref, cc_ref, cx_ref, ccp_ref, cxp_ref, ccn_ref, cxn_ref, gc_ref, cw_ref, w_hbm, o_ref,
                 w_scr, stage, sem):
    i = pl.program_id(0)

    @pl.when(i == 0)
    def _():
        _load_cast_weight(w_hbm, w_scr, stage, sem)

    n_ctx_tiles = T_P // TM
    per_seq = DEC_SEQ // TM
    is_first = jnp.logical_or(i < n_ctx_tiles, (i - n_ctx_tiles) % per_seq == 0)
    is_last = jnp.logical_or(i < n_ctx_tiles, (i - n_ctx_tiles) % per_seq == per_seq - 1)
    p = cc_ref[...].astype(F32) * cx_ref[...].astype(F32)
    prev_row = ccp_ref[HALO - 1:HALO, :].astype(F32) * cxp_ref[HALO - 1:HALO, :].astype(F32)
    next_row = ccn_ref[0:1, :].astype(F32) * cxn_ref[0:1, :].astype(F32)
    prev_row = jnp.where(is_first, 0.0, prev_row)
    next_row = jnp.where(is_last, 0.0, next_row)
    rows = lax.broadcasted_iota(I32, (TM, 1), 0)
    p_prev = jnp.where(rows == 0, prev_row, pltpu.roll(p, 1, 0))
    p_next = jnp.where(rows == TM - 1, next_row, pltpu.roll(p, TM - 1, 0))
    conv = p_prev * cw_ref[0:1, :] + p * cw_ref[1:2, :] + p_next * cw_ref[2:3, :]
    u = cb_ref[...].astype(F32) * conv
    y = jnp.dot(u.astype(BF16), w_scr[...], preferred_element_type=F32)
    o_ref[...] = (jax.nn.sigmoid(gc_ref[...].astype(F32)) * y).astype(BF16)


def _conv(proj, conv_w, w_conv_out):
    hb = TM // HALO
    last_hb = T_ALL // HALO - 1
    wide = lambda c: pl.BlockSpec((TM, D_MODEL), lambda i: (i, c // D_MODEL))
    prev = lambda c: pl.BlockSpec((HALO, D_MODEL), lambda i: (jnp.maximum(i * hb - 1, 0), c // D_MODEL))
    nxt = lambda c: pl.BlockSpec((HALO, D_MODEL), lambda i: (jnp.minimum((i + 1) * hb, last_hb), c // D_MODEL))
    return pl.pallas_call(
        _conv_kernel,
        grid=(T_ALL // TM,),
        in_specs=[wide(COL_CB), wide(COL_CC), wide(COL_CX),
                  prev(COL_CC), prev(COL_CX), nxt(COL_CC), nxt(COL_CX),
                  wide(COL_GC),
                  pl.BlockSpec((3, CONV_WIDTH), lambda i: (0, 0)),
                  pl.BlockSpec(memory_space=pl.ANY)],
        out_specs=pl.BlockSpec((TM, D_MODEL), lambda i: (i, 0)),
        out_shape=jax.ShapeDtypeStruct((T_ALL, D_MODEL), BF16),
        scratch_shapes=[pltpu.VMEM((CONV_WIDTH, D_MODEL), BF16),
                        pltpu.VMEM((CAST_ROWS, D_MODEL), F32),
                        pltpu.SemaphoreType.DMA(())],
        compiler_params=_cparams(("arbitrary",)),
        name="conv",
    )(proj, proj, proj, proj, proj, proj, proj, proj, conv_w, w_conv_out)


def _merge_kernel(xp_ref, xs_ref, attn_p_ref, attn_s_ref, z1_ref, ga_ref, mod_ref, g2_ref, wr_ref,
                  wao_hbm, wo_hbm, x1_ref, h2_ref, lg_ref, wao_scr, wo_scr, stage, sem):
    i = pl.program_id(0)

    @pl.when(i == 0)
    def _():
        _load_cast_weight(wao_hbm, wao_scr, stage, sem)
        _load_cast_weight(wo_hbm, wo_scr, stage, sem)

    is_ctx = i < T_P // TM
    attn = jnp.where(is_ctx, attn_p_ref[...], attn_s_ref[...])
    x = jnp.where(is_ctx, xp_ref[...], xs_ref[...])
    ya = jnp.dot(attn, wao_scr[...], preferred_element_type=F32)
    z = z1_ref[...].astype(F32) + jax.nn.sigmoid(ga_ref[...].astype(F32)) * ya
    mix = jnp.dot(z.astype(BF16), wo_scr[...], preferred_element_type=F32)
    x1 = x + mod_ref[0, 2:3, :] * mix
    x1_ref[...] = x1
    h = _rms_mod(x1, g2_ref[...], mod_ref[0, 3:4, :], mod_ref[0, 4:5, :])
    h2_ref[...] = h
    h_hi = h.astype(BF16)
    h_lo = (h - h_hi.astype(F32)).astype(BF16)
    wr = wr_ref[...]
    wr_hi32 = wr.astype(BF16).astype(F32)
    wr_lo32 = (wr - wr_hi32).astype(BF16).astype(F32)
    w_both = (wr_hi32 + pltpu.roll(wr_lo32, N_EXPERTS, 1)).astype(BF16)
    both = jnp.dot(h_hi, w_both, preferred_element_type=F32)
    lg_ref[...] = (both + pltpu.roll(both, N_EXPERTS, 1)
                   + jnp.dot(h_lo, wr_hi32.astype(BF16), preferred_element_type=F32))


def _merge(xp, xs, attn_p, attn_s, z1, proj, mods3, g_ffn, wr_pad, w_attn_out, w_out):
    tiles_per_group = MOD_GROUP // TM
    const = lambda shape: pl.BlockSpec(shape, lambda i: (0, 0))
    row = lambda w: pl.BlockSpec((TM, w), lambda i: (i, 0))
    xsp, xss = _two_group_specs(TM, D_MODEL)
    asp, ass = _two_group_specs(TM, ATTN_WIDTH)
    return pl.pallas_call(
        _merge_kernel,
        grid=(T_ALL // TM,),
        in_specs=[xsp, xss, asp, ass, row(D_MODEL),
                  pl.BlockSpec((TM, D_MODEL), lambda i: (i, COL_GA // D_MODEL)),
                  pl.BlockSpec((1, 6, D_MODEL), lambda i: (i // tiles_per_group, 0, 0)),
                  const((1, D_MODEL)), const((D_MODEL, 128)),
                  pl.BlockSpec(memory_space=pl.ANY), pl.BlockSpec(memory_space=pl.ANY)],
        out_specs=[row(D_MODEL), row(D_MODEL), row(128)],
        out_shape=[jax.ShapeDtypeStruct((T_ALL, D_MODEL), F32),
                   jax.ShapeDtypeStruct((T_ALL, D_MODEL), F32),
                   jax.ShapeDtypeStruct((T_ALL, 128), F32)],
        scratch_shapes=[pltpu.VMEM((ATTN_WIDTH, D_MODEL), BF16),
                        pltpu.VMEM((D_MODEL, D_MODEL), BF16),
                        pltpu.VMEM((CAST_ROWS, D_MODEL), F32),
                        pltpu.SemaphoreType.DMA(())],
        compiler_params=_cparams(("arbitrary",)),
        name="merge",
    )(xp, xs, attn_p, attn_s, z1, proj, mods3, g_ffn, wr_pad, w_attn_out, w_out)


def _route1_kernel(lg_ref, bias_ref, enc_ref, cnt_ref):
    i = pl.program_id(0)
    lt = lg_ref[...].T[:N_EXPERTS, :]
    scores = jax.nn.sigmoid(lt)
    biased = scores + bias_ref[...]
    b3 = biased.reshape(N_EXPERT_GROUPS, GROUP_SIZE, TM)
    mi = lax.broadcasted_iota(I32, b3.shape, 1)
    m1 = jnp.max(b3, axis=1, keepdims=True)
    idx1 = jnp.min(jnp.where(b3 == m1, mi, GROUP_SIZE), axis=1, keepdims=True)
    m2 = jnp.max(jnp.where(mi == idx1, -jnp.inf, b3), axis=1, keepdims=True)
    gs = (m1 + m2).reshape(N_EXPERT_GROUPS, TM)
    gidx = lax.broadcasted_iota(I32, gs.shape, 0)
    grank = jnp.zeros(gs.shape, I32)
    for j in range(N_EXPERT_GROUPS):
        gj = gs[j:j + 1, :]
        beats = jnp.logical_or(gj > gs, jnp.logical_and(gj == gs, j < gidx))
        grank = grank + beats.astype(I32)
    gsel = grank < TOPK_GROUPS
    emask = jnp.broadcast_to(gsel[:, None, :], b3.shape).reshape(N_EXPERTS, TM)
    masked = jnp.where(emask, biased, NEG_INF)
    eidx = lax.broadcasted_iota(I32, masked.shape, 0)
    erank = jnp.zeros(masked.shape, I32)
    for j in range(N_EXPERTS):
        vj = masked[j:j + 1, :]
        beats = jnp.logical_or(vj > masked, jnp.logical_and(vj == masked, j < eidx))
        erank = erank + beats.astype(I32)
    sel = erank < TOP_K
    wsel = jnp.where(sel, scores, 0.0)
    den = jnp.sum(wsel, axis=0, keepdims=True)
    wts = wsel / den * ROUTED_SCALE
    enc_ref[...] = jnp.where(sel, wts, -1.0)

    @pl.when(i == 0)
    def _():
        cnt_ref[...] = jnp.zeros_like(cnt_ref)

    cnt = jnp.sum(sel.astype(F32), axis=1, keepdims=True)
    cnt_ref[...] += jnp.broadcast_to(cnt, cnt_ref.shape)


def _route1(logits, bias_col):
    return pl.pallas_call(
        _route1_kernel,
        grid=(T_ALL // TM,),
        in_specs=[pl.BlockSpec((TM, 128), lambda i: (i, 0)),
                  pl.BlockSpec((N_EXPERTS, 1), lambda i: (0, 0))],
        out_specs=[pl.BlockSpec((N_EXPERTS, TM), lambda i: (0, i)),
                   pl.BlockSpec((N_EXPERTS, 128), lambda i: (0, 0))],
        out_shape=[jax.ShapeDtypeStruct((N_EXPERTS, T_ALL), F32),
                   jax.ShapeDtypeStruct((N_EXPERTS, 128), F32)],
        compiler_params=_cparams(("arbitrary",)),
        name="route1",
    )(logits, bias_col)


def _route2_positions(enc_ref, cnt_ref, meta_ref, tmap_ref, carry_ref):
    enc = enc_ref[...]
    sel = enc >= 0.0
    wts = jnp.maximum(enc, 0.0)
    sel_b = sel.astype(BF16)
    ntile = jnp.ceil(cnt_ref[...] * (1.0 / TILE_E))
    er = lax.broadcasted_iota(I32, (N_EXPERTS, N_EXPERTS), 0)
    ec = lax.broadcasted_iota(I32, (N_EXPERTS, N_EXPERTS), 1)
    lower = (ec < er).astype(BF16)
    off_t = jnp.dot(lower, ntile.astype(BF16), preferred_element_type=F32)
    tr = lax.broadcasted_iota(I32, (TM, TM), 0)
    tc = lax.broadcasted_iota(I32, (TM, TM), 1)
    upper = (tr < tc).astype(BF16)
    rank = jnp.dot(sel_b, upper, preferred_element_type=F32) + carry_ref[:, 0:1]
    carry_ref[...] += jnp.broadcast_to(jnp.sum(sel.astype(F32), axis=1, keepdims=True), carry_ref.shape)
    pos = off_t[:, 0:1] * float(TILE_E) + rank
    slot = jnp.dot(lower, sel_b, preferred_element_type=F32)
    rows = []
    for k in range(TOP_K):
        mk = jnp.logical_and(sel, slot == float(k))
        rows.append(jnp.sum(jnp.where(mk, wts, 0.0), axis=0, keepdims=True))
    for k in range(TOP_K):
        mk = jnp.logical_and(sel, slot == float(k))
        rows.append(jnp.sum(jnp.where(mk, pos, 0.0), axis=0, keepdims=True))
    rows.append(jnp.zeros((128 - 2 * TOP_K, TM), F32))
    meta_ref[...] = jnp.concatenate(rows, axis=0).T
    nt = ntile[:, 0:1]
    end_t = off_t[:, 0:1] + nt
    lane = lax.broadcasted_iota(I32, (N_EXPERTS, TMAP_W), 1)
    te = jnp.sum((end_t <= lane.astype(F32)).astype(F32), axis=0, keepdims=True)
    te = jnp.minimum(te, float(N_EXPERTS - 1))
    nact = jnp.sum(nt, axis=0, keepdims=True)
    erow = lax.broadcasted_iota(I32, (N_EXPERTS, TMAP_W), 0).astype(F32)
    nxt = jnp.sum(jnp.where(erow == te, end_t, 0.0), axis=0, keepdims=True)
    tmap_ref[...] = jnp.zeros_like(tmap_ref)
    tmap_ref[0:1, :] = te.astype(I32)
    tmap_ref[1:2, :] = jnp.broadcast_to(nact, (1, TMAP_W)).astype(I32)
    tmap_ref[2:3, :] = nxt.astype(I32)
    return jnp.concatenate(rows[TOP_K:2 * TOP_K], axis=0).astype(I32)


N_TOK_TILES = T_ALL // TM


def _route2_kernel(enc_ref, cnt_ref, init_hbm, meta_ref, tmap_ref, inv_hbm,
                   carry_ref, posv0, posv1, poss0, poss1, inv_smem, psem, isem):
    i = pl.program_id(0)
    posv = (posv0, posv1)
    poss = (poss0, poss1)
    per_token = (1 << TOKEN_BITS) + 1
    per_slot = T_ALL << TOKEN_BITS

    @pl.when(i == 0)
    def _():
        carry_ref[...] = jnp.zeros_like(carry_ref)
        cp = pltpu.make_async_copy(init_hbm, inv_smem, isem)
        cp.start()
        cp.wait()

    def positions(s):
        posv[s][...] = _route2_positions(enc_ref, cnt_ref, meta_ref, tmap_ref, carry_ref)
        pltpu.make_async_copy(posv[s], poss[s], psem.at[s]).start()

    def invert(s):
        pltpu.make_async_copy(posv[s], poss[s], psem.at[s]).wait()
        base = (i - 1) * (TM * per_token)
        for t in range(TM):
            for k in range(TOP_K):
                inv_smem[poss[s][k, t]] = base + (t * per_token + k * per_slot)

    @pl.when(i == 0)
    def _():
        positions(0)

    for s in range(2):
        @pl.when(jnp.logical_and(jnp.logical_and(i >= 1, i < N_TOK_TILES), i % 2 == s))
        def _():
            invert(1 - s)
            positions(s)

    @pl.when(i == N_TOK_TILES)
    def _():
        invert((N_TOK_TILES - 1) % 2)
        cp = pltpu.make_async_copy(inv_smem, inv_hbm, isem)
        cp.start()
        cp.wait()


def _route2(enc, cnt, inv_init):
    last = N_TOK_TILES - 1
    hbm = pl.BlockSpec(memory_space=pl.ANY)
    return pl.pallas_call(
        _route2_kernel,
        grid=(N_TOK_TILES + 1,),
        in_specs=[pl.BlockSpec((N_EXPERTS, TM), lambda i: (0, jnp.minimum(i, last))),
                  pl.BlockSpec((N_EXPERTS, 128), lambda i: (0, 0)),
                  hbm],
        out_specs=[pl.BlockSpec((TM, 128), lambda i: (jnp.minimum(i, last), 0)),
                   pl.BlockSpec((8, TMAP_W), lambda i: (0, 0)),
                   hbm],
        out_shape=[jax.ShapeDtypeStruct((T_ALL, 128), F32),
                   jax.ShapeDtypeStruct((8, TMAP_W), I32),
                   jax.ShapeDtypeStruct((N_SORTED,), I32)],
        scratch_shapes=[pltpu.VMEM((N_EXPERTS, 128), F32),
                        pltpu.VMEM((TOP_K, TM), I32), pltpu.VMEM((TOP_K, TM), I32),
                        pltpu.SMEM((TOP_K, TM), I32), pltpu.SMEM((TOP_K, TM), I32),
                        pltpu.SMEM((N_SORTED,), I32),
                        pltpu.SemaphoreType.DMA((2,)), pltpu.SemaphoreType.DMA(())],
        compiler_params=_cparams(("arbitrary",)),
        name="route2",
    )(enc, cnt, inv_init)


TOKEN_BITS = 14
TOKEN_MASK = (1 << TOKEN_BITS) - 1
NBUF = 3
TRASH_BASE = TOP_K * T_ALL
YS_ROWS = TRASH_BASE + NBUF * TILE_E
assert T_ALL <= 1 << TOKEN_BITS and YS_ROWS << TOKEN_BITS < 1 << 31


def _pad_codes():
    r = jnp.arange(N_SORTED, dtype=I32)
    q = r % TILE_E
    out_row = TRASH_BASE + ((r // TILE_E) % NBUF) * TILE_E + q
    return (out_row << TOKEN_BITS) | q


N_CHUNK = 256
GU_PIECES = D_EXPERT // N_CHUNK
DN_PIECES = D_MODEL // N_CHUNK
N_PIECES = GU_PIECES + DN_PIECES

PIECE_WORK = (D_MODEL * 2,) * GU_PIECES + (D_EXPERT,) * DN_PIECES
PIECE_ROWS = tuple(round(TILE_E * sum(PIECE_WORK[:p]) / sum(PIECE_WORK)) for p in range(N_PIECES + 1))


def _moe_kernel(te_ref, na_ref, nxt_ref, inv_ref, h2_hbm, wg_hbm, wu_hbm, wd_hbm, ys_hbm,
                xbuf0, xbuf1, xbuf2, ybuf0, ybuf1, ybuf2, xb, act, wg_f32, wu_f32, wd_f32,
                wg_scr, wu_scr, wd_scr, gsem, ssem, wsem):
    i = pl.program_id(0)
    na = na_ref[0]
    active = i < na
    xbufs = (xbuf0, xbuf1, xbuf2)
    ybufs = (ybuf0, ybuf1, ybuf2)

    def weight_copies(e):
        return (pltpu.make_async_copy(wg_hbm.at[e], wg_f32, wsem),
                pltpu.make_async_copy(wu_hbm.at[e], wu_f32, wsem),
                pltpu.make_async_copy(wd_hbm.at[e], wd_f32, wsem))

    def gather_row(tile, s, r):
        tok = inv_ref[tile * TILE_E + r] & TOKEN_MASK
        pltpu.make_async_copy(h2_hbm.at[pl.ds(tok, 1)], xbufs[s].at[pl.ds(r, 1)], gsem.at[s]).start()

    def scatter_row(dst, s, r, priority=0):
        pltpu.make_async_copy(ybufs[s].at[pl.ds(r, 1)], ys_hbm.at[pl.ds(dst, 1)], ssem.at[s]).start(priority)

    def wait_gather(s):
        pltpu.make_async_copy(h2_hbm.at[pl.ds(0, TILE_E)], xbufs[s], gsem.at[s]).wait()

    def wait_scatter(s):
        pltpu.make_async_copy(ybufs[s], ys_hbm.at[pl.ds(0, TILE_E)], ssem.at[s]).wait()

    @pl.when(i == 0)
    def _():
        zeros = ybufs[NBUF - 1]
        zeros[...] = jnp.zeros_like(zeros)
        for m in range(NBUF):
            cp = pltpu.make_async_copy(zeros, ys_hbm.at[pl.ds(TRASH_BASE + m * TILE_E, TILE_E)], ssem.at[0])
            cp.start()
            cp.wait()
        for t in range(2):
            def body(r, carry):
                gather_row(t, t, r)
                return carry

            lax.fori_loop(0, TILE_E, body, 0)

        for cp in weight_copies(te_ref[0]):
            cp.start()

    new_expert = jnp.logical_or(i == 0, te_ref[i] != te_ref[jnp.maximum(i - 1, 0)])

    @pl.when(jnp.logical_and(active, new_expert))
    def _():
        for cp in weight_copies(0):
            cp.wait()
        for c in range(D_MODEL // CAST_ROWS):
            rows = pl.ds(c * CAST_ROWS, CAST_ROWS)
            wg_scr[rows, :] = wg_f32[rows, :].astype(BF16)
            wu_scr[rows, :] = wu_f32[rows, :].astype(BF16)
        wd_scr[...] = wd_f32[...].astype(BF16)
        nxt_tile = nxt_ref[i]

        @pl.when(nxt_tile < na)
        def _():
            for cp in weight_copies(te_ref[nxt_tile]):
                cp.start(priority=1)

    def compute_tile(slot, write_back):
        s_next = (slot + 2) % NBUF
        s_prev = (slot - 1) % NBUF
        wait_gather(slot)

        @pl.when(i >= 3)
        def _():
            wait_scatter(slot)

        xb[...] = xbufs[slot][...].astype(BF16)
        nxt = jnp.minimum(i + 2, N_TILES_E - 1)
        codes_next = inv_ref.at[pl.ds(pl.multiple_of(nxt * TILE_E, TILE_E), TILE_E)]
        codes_prev = inv_ref.at[pl.ds(pl.multiple_of(jnp.maximum(i - 1, 0) * TILE_E, TILE_E), TILE_E)]

        def issue(piece):
            for r in range(PIECE_ROWS[piece], PIECE_ROWS[piece + 1]):
                tok = codes_next[r] & TOKEN_MASK
                pltpu.make_async_copy(h2_hbm.at[pl.ds(tok, 1)], xbufs[s_next].at[pl.ds(r, 1)],
                                      gsem.at[s_next]).start()
                if write_back:
                    scatter_row(codes_prev[r] >> TOKEN_BITS, s_prev, r, priority=r % 2)

        for c in range(GU_PIECES):
            cols = slice(c * N_CHUNK, (c + 1) * N_CHUNK)
            g = jnp.dot(xb[...], wg_scr[:, cols], preferred_element_type=F32)
            u = jnp.dot(xb[...], wu_scr[:, cols], preferred_element_type=F32)
            act[:, cols] = (_silu(g) * u).astype(BF16)
            issue(c)
        for c in range(DN_PIECES):
            cols = slice(c * N_CHUNK, (c + 1) * N_CHUNK)
            ybufs[slot][:, cols] = jnp.dot(act[...], wd_scr[:, cols], preferred_element_type=F32)
            issue(GU_PIECES + c)

    def drain(slot):
        s_last = (slot - 1) % NBUF

        def body(r, carry):
            scatter_row(inv_ref[(na - 1) * TILE_E + r] >> TOKEN_BITS, s_last, r)
            return carry

        lax.fori_loop(0, TILE_E, body, 0)
        wait_scatter(s_last)

        @pl.when(na >= 2)
        def _():
            wait_scatter((slot - 2) % NBUF)

        @pl.when(na >= 3)
        def _():
            wait_scatter(slot)

        wait_gather(slot)
        wait_gather((slot + 1) % NBUF)

    @pl.when(jnp.logical_and(active, i == 0))
    def _():
        compute_tile(0, write_back=False)

    for s in range(NBUF):
        @pl.when(jnp.logical_and(jnp.logical_and(active, i > 0), i % NBUF == s))
        def _():
            compute_tile(s, write_back=True)

        @pl.when(jnp.logical_and(i == na, i % NBUF == s))
        def _():
            drain(s)


def _moe(tile_expert, n_active, next_tile, inv, h2, wg, wu, wd):
    hbm = pl.BlockSpec(memory_space=pl.ANY)
    grid_spec = pltpu.PrefetchScalarGridSpec(
        num_scalar_prefetch=4,
        grid=(N_TILES_E + 1,),
        in_specs=[hbm, hbm, hbm, hbm],
        out_specs=hbm,
        scratch_shapes=[pltpu.VMEM((TILE_E, D_MODEL), F32)] * (2 * NBUF) + [
                        pltpu.VMEM((TILE_E, D_MODEL), BF16),
                        pltpu.VMEM((TILE_E, D_EXPERT), BF16),
                        pltpu.VMEM((D_MODEL, D_EXPERT), F32),
                        pltpu.VMEM((D_MODEL, D_EXPERT), F32),
                        pltpu.VMEM((D_EXPERT, D_MODEL), F32),
                        pltpu.VMEM((D_MODEL, D_EXPERT), BF16),
                        pltpu.VMEM((D_MODEL, D_EXPERT), BF16),
                        pltpu.VMEM((D_EXPERT, D_MODEL), BF16),
                        pltpu.SemaphoreType.DMA((NBUF,)),
                        pltpu.SemaphoreType.DMA((NBUF,)),
                        pltpu.SemaphoreType.DMA(())],
    )
    return pl.pallas_call(
        _moe_kernel,
        grid_spec=grid_spec,
        out_shape=jax.ShapeDtypeStruct((YS_ROWS, D_MODEL), F32),
        compiler_params=_cparams(("arbitrary",)),
        name="moe",
    )(tile_expert, n_active, next_tile, inv, h2, wg, wu, wd)


def _combine_kernel(meta_ref, *refs):
    ys_refs = refs[:TOP_K]
    h2_ref, x1_ref, mod_ref, wsg_hbm, wsu_hbm, wsd_hbm = refs[TOP_K:TOP_K + 6]
    op_ref, os_ref, wsg_scr, wsu_scr, wsd_scr, stage_a, stage_b, wsem = refs[TOP_K + 6:]
    i = pl.program_id(0)

    @pl.when(i == 0)
    def _():
        _load_cast_weight(wsg_hbm, wsg_scr, stage_a, wsem)
        _load_cast_weight(wsu_hbm, wsu_scr, stage_a, wsem)
        _load_cast_weight(wsd_hbm, wsd_scr, stage_b, wsem)

    h = h2_ref[...].astype(BF16)
    sg = jnp.dot(h, wsg_scr[...], preferred_element_type=F32)
    su = jnp.dot(h, wsu_scr[...], preferred_element_type=F32)
    moe = jnp.dot((_silu(sg) * su).astype(BF16), wsd_scr[...], preferred_element_type=F32)
    for k in range(TOP_K):
        moe = moe + meta_ref[:, k:k + 1] * ys_refs[k][...]
    y = x1_ref[...] + mod_ref[0, 5:6, :] * moe

    @pl.when(i < T_P // TM_C)
    def _():
        op_ref[...] = y

    @pl.when(i >= T_P // TM_C)
    def _():
        os_ref[...] = y


def _combine(meta, ys, h2, x1, mods3, wsg, wsu, wsd):
    tiles_per_group = MOD_GROUP // TM_C
    row = lambda w: pl.BlockSpec((TM_C, w), lambda i: (i, 0))
    slot_rows = lambda k: pl.BlockSpec((TM_C, D_MODEL), lambda i: (k * (T_ALL // TM_C) + i, 0))
    osp, oss = _two_group_specs(TM_C, D_MODEL)
    hbm = pl.BlockSpec(memory_space=pl.ANY)
    return pl.pallas_call(
        _combine_kernel,
        grid=(T_ALL // TM_C,),
        in_specs=[row(128)] + [slot_rows(k) for k in range(TOP_K)] + [
                  row(D_MODEL), row(D_MODEL),
                  pl.BlockSpec((1, 6, D_MODEL), lambda i: (i // tiles_per_group, 0, 0)),
                  hbm, hbm, hbm],
        out_specs=[osp, oss],
        out_shape=[jax.ShapeDtypeStruct((T_P, D_MODEL), F32),
                   jax.ShapeDtypeStruct((T_S, D_MODEL), F32)],
        scratch_shapes=[pltpu.VMEM((D_MODEL, D_SHARED), BF16),
                        pltpu.VMEM((D_MODEL, D_SHARED), BF16),
                        pltpu.VMEM((D_SHARED, D_MODEL), BF16),
                        pltpu.VMEM((D_MODEL, D_SHARED), F32),
                        pltpu.VMEM((D_SHARED, D_MODEL), F32),
                        pltpu.SemaphoreType.DMA(())],
        compiler_params=_cparams(("arbitrary",)),
        name="combine",
    )(meta, *([ys] * TOP_K), h2, x1, mods3, wsg, wsu, wsd)


def _rope_tables():
    rows = DEC_SEQ // GRID_W
    row = np.repeat(np.arange(rows, dtype=np.float64), GRID_W)
    col = np.tile(np.arange(GRID_W, dtype=np.float64), rows)
    n_freq = HEAD_DIM // 4
    inv = ROPE_THETA ** (-np.arange(n_freq, dtype=np.float64) / n_freq)
    ang = np.concatenate([row[:, None] * inv, col[:, None] * inv], axis=-1)
    cos, sin = np.cos(ang), np.sin(ang)
    cos2 = np.concatenate([cos, cos], axis=-1)
    sin2 = np.concatenate([-sin, sin], axis=-1)
    cos_tab = np.concatenate([np.ones((TM_IN, HEAD_DIM)), cos2], axis=0).astype(np.float32)
    sin_tab = np.concatenate([np.zeros((TM_IN, HEAD_DIM)), sin2], axis=0).astype(np.float32)
    return jnp.asarray(cos_tab), jnp.asarray(sin_tab)


def kernel(x_prompt, x_sample, cache_k, cache_v, c, c_ctx, w_ada, b_ada, norm_mix_g, norm_ffn_g, w_in, conv_w,
           q_norm_g, k_norm_g, attn_sink, w_conv_out, w_attn_out, w_out, router_w, router_bias, w_exp_gate,
           w_exp_up, w_exp_down, w_sh_gate, w_sh_up, w_sh_down):
    l = 0
    xp = x_prompt.reshape(T_P, D_MODEL)
    xs = x_sample.reshape(T_S, D_MODEL)

    cond = jnp.concatenate([c_ctx[None, :], c, jnp.zeros((8 - N_MOD, D_MODEL), F32)], axis=0)
    mods = _ada(cond.T, w_ada[l], b_ada[l][None, :])
    mods3 = mods[:N_MOD].reshape(N_MOD, 6, D_MODEL)

    h = _prenorm(xp, xs, mods3, norm_mix_g[l][None, :])
    cos_tab, sin_tab = _rope_tables()
    proj, k32, v32 = _inproj(h, w_in[l], q_norm_g[l][None, :], k_norm_g[l][None, :], cos_tab, sin_tab)

    sink = attn_sink[l]
    kc = cache_k[:, l].reshape(DEC_BATCH, PAST_LEN, KV_WIDTH)
    vc = cache_v[:, l].reshape(DEC_BATCH, PAST_LEN, KV_WIDTH)
    attn_p = _attn_p(sink, proj)
    attn_s = _attn_s(sink, proj, kc, vc)

    z1 = _conv(proj, conv_w[l], w_conv_out[l])

    wr_pad = jnp.pad(router_w[l], ((0, 0), (0, 128 - N_EXPERTS)))
    x1, h2, logits = _merge(xp, xs, attn_p, attn_s, z1, proj, mods3, norm_ffn_g[l][None, :], wr_pad,
                            w_attn_out[l], w_out[l])

    enc, cnt = _route1(logits, router_bias[l][:, None])
    meta, tmap, inv = _route2(enc, cnt, _pad_codes())
    tile_expert = tmap[0]
    n_active = tmap[1, :1]

    ys = _moe(tile_expert, n_active, tmap[2], inv, h2, w_exp_gate[l], w_exp_up[l], w_exp_down[l])
    y_p, y_s = _combine(meta, ys, h2, x1, mods3, w_sh_gate[l], w_sh_up[l], w_sh_down[l])

    y_prompt = y_p.reshape(BATCH, SEQ, D_MODEL)
    y_sample = y_s.reshape(DEC_BATCH, DEC_SEQ, D_MODEL)
    new_k = k32.reshape(BATCH, 1, SEQ, N_KV_HEADS, HEAD_DIM)
    new_v = v32.reshape(BATCH, 1, SEQ, N_KV_HEADS, HEAD_DIM)
    return (y_prompt, y_sample, new_k, new_v)
```
